```python
import jax, jax.numpy as jnp
from jax import lax
import numpy as np

D_MODEL = 4096
BATCH = 1
SEQ = 8192
DEPTH = 2

N_MIXERS = 2
HEAD_DIM = 128
N_GROUP_HEADS = D_MODEL // 256
GROUP_WIDTH = N_GROUP_HEADS * HEAD_DIM
DILATED_GROUPS = ((128, 1), (512, 4), (2048, 16))
BLOCK = 128
CONV_WIDTH = 3 * D_MODEL // 4
SHORT_CONV = 3
MEM_TOKENS = 256
MEM_HEADS = 4
MEM_HEAD_DIM = D_MODEL // 16
MEM_WIDTH = MEM_HEADS * MEM_HEAD_DIM
D_FF = 11008
ROPE_THETA = 10000.0
EPS = 1e-6
NEG_INF = -1e30

kernel_name = "hybrid_shortconv_dilated_swa_memxattn_convffn"


def rmsnorm(x, g):
    xf = x.astype(jnp.float32)
    y = xf * lax.rsqrt(jnp.mean(xf * xf, axis=-1, keepdims=True) + EPS)
    return (y * g.astype(jnp.float32)).astype(x.dtype)


def causal_dwconv3(u, w):
    up = jnp.pad(u, ((0, 0), (SHORT_CONV - 1, 0), (0, 0)))
    return w[0] * up[:, :-2] + w[1] * up[:, 1:-1] + w[2] * up[:, 2:]


def rope_tables(positions):
    half = HEAD_DIM // 2
    inv_freq = ROPE_THETA ** (-jnp.arange(half, dtype=jnp.float32) * 2.0 / HEAD_DIM)
    ang = positions.astype(jnp.float32)[..., None] * inv_freq
    return jnp.cos(ang)[:, :, None, None, :], jnp.sin(ang)[:, :, None, None, :]


def apply_rope(t, cos, sin):
    tf = t.astype(jnp.float32)
    t1, t2 = jnp.split(tf, 2, axis=-1)
    return jnp.concatenate([t1 * cos - t2 * sin, t2 * cos + t1 * sin], axis=-1).astype(t.dtype)


def memory_attend(q, mem_k, mem_v):
    B, S, _ = q.shape
    qh = q.reshape(B, S, MEM_HEADS, MEM_HEAD_DIM)
    s = jnp.einsum("bshe,bmhe->bhsm", qh, mem_k, preferred_element_type=jnp.float32) * (MEM_HEAD_DIM ** -0.5)
    p = jax.nn.softmax(s, axis=-1).astype(mem_v.dtype)
    o = jnp.einsum("bhsm,bmhe->bshe", p, mem_v)
    return o.reshape(B, S, MEM_WIDTH)


def dilated_window_attention(q, k, v, dilation, steps):
    B, S, H, Dh = q.shape
    L = S // dilation
    nb = -(-L // BLOCK)
    Lp = nb * BLOCK

    def blocks(t):
        t = t.reshape(B, L, dilation, H, Dh)
        t = jnp.pad(t, ((0, 0), (0, Lp - L), (0, 0), (0, 0), (0, 0)))
        return t.reshape(B, nb, BLOCK, dilation, H, Dh)

    def with_prev(t):
        prev = jnp.pad(t, ((0, 0), (1, 0), (0, 0), (0, 0), (0, 0), (0, 0)))[:, :-1]
        return jnp.concatenate([prev, t], axis=2)

    qb = blocks(q)
    kw = with_prev(blocks(k))
    vw = with_prev(blocks(v))
    s = jnp.einsum("bnqrhe,bnkrhe->bnrhqk", qb, kw, preferred_element_type=jnp.float32) * (Dh ** -0.5)
    qi = jnp.arange(BLOCK)[:, None]
    ki = jnp.arange(2 * BLOCK)[None, :]
    back = BLOCK + qi - ki
    blk = jnp.arange(nb)[:, None, None]
    valid = (back >= 0) & (back <= steps) & ((blk > 0) | (ki >= BLOCK))
    s = jnp.where(valid[None, :, None, None], s, NEG_INF)
    m = jnp.max(s, axis=-1, keepdims=True)
    p = jnp.exp(s - m)
    den = jnp.sum(p, axis=-1, keepdims=True)
    o = jnp.einsum("bnrhqk,bnkrhe->bnqrhe", (p / den).astype(v.dtype), vw)
    lse = (m + jnp.log(den))[..., 0]
    o = o.reshape(B, Lp, dilation, H, Dh)[:, :L].reshape(B, S, H, Dh)
    lse = lse.transpose(0, 1, 4, 2, 3).reshape(B, Lp, dilation, H)[:, :L].reshape(B, S, H)
    return o, lse


def short_conv_mixer(xn, w_in, conv_w, w_out, mem_k, mem_v):
    h = xn @ w_in
    b_gate = h[..., :CONV_WIDTH]
    c_gate = h[..., CONV_WIDTH:2 * CONV_WIDTH]
    u = h[..., 2 * CONV_WIDTH:3 * CONV_WIDTH]
    q_mem = h[..., 3 * CONV_WIDTH:]
    y = b_gate * causal_dwconv3(c_gate * u, conv_w)
    y_mem = memory_attend(q_mem, mem_k, mem_v)
    return jnp.concatenate([y, y_mem], axis=-1) @ w_out


def dilated_attention_mixer(xn, w_in, w_out, cos, sin, mem_k, mem_v):
    B, S, _ = xn.shape
    n_self = len(DILATED_GROUPS) * 3 * GROUP_WIDTH
    h = xn @ w_in
    qkv = h[..., :n_self].reshape(B, S, len(DILATED_GROUPS), 3, N_GROUP_HEADS, HEAD_DIM)
    q = apply_rope(qkv[:, :, :, 0], cos, sin)
    k = apply_rope(qkv[:, :, :, 1], cos, sin)
    v = qkv[:, :, :, 2]
    outs, lses = [], []
    for g, (window, dilation) in enumerate(DILATED_GROUPS):
        o, lse = dilated_window_attention(q[:, :, g], k[:, :, g], v[:, :, g], dilation, window // dilation)
        outs.append(o)
        lses.append(lse)
    alpha = jax.nn.softmax(jnp.stack(lses, axis=0), axis=0)
    merged = jnp.sum(alpha[..., None] * jnp.stack(outs, axis=0).astype(jnp.float32), axis=0)
    merged = merged.astype(xn.dtype).reshape(B, S, GROUP_WIDTH)
    y_mem = memory_attend(h[..., n_self:], mem_k, mem_v)
    return jnp.concatenate([merged, y_mem], axis=-1) @ w_out


def conv_ffn(xn, w_gate, w_up, conv_w, w_down):
    g = causal_dwconv3(xn @ w_gate, conv_w)
    return (jax.nn.silu(g) * (xn @ w_up)) @ w_down


def setup_inputs(seed: int = 0) -> dict:
    key = jax.random.key(seed)
    ks = jax.random.split(key, 20)
    f32 = jnp.float32
    n_a = (DEPTH + 1) // 2
    n_b = DEPTH // 2

    def w(k, shape, fan_in):
        return jax.random.normal(k, shape, f32) * (fan_in ** -0.5)

    def gain(k, shape):
        return 1.0 + 0.02 * jax.random.normal(k, shape, f32)

    x = jax.random.normal(ks[0], (BATCH, SEQ, D_MODEL), f32)
    mem = jax.random.normal(ks[1], (BATCH, MEM_TOKENS, D_MODEL), f32)
    offset = jax.random.randint(ks[2], (BATCH, 1), 0, 4096, dtype=jnp.int32)
    positions = offset + jnp.arange(SEQ, dtype=jnp.int32)[None, :]
    n_attn_in = len(DILATED_GROUPS) * 3 * GROUP_WIDTH + MEM_WIDTH
    return {
        "x": x,
        "mem": mem,
        "positions": positions,
        "g_mem": gain(ks[3], (D_MODEL,)),
        "w_mem_kv": w(ks[4], (D_MODEL, 2 * MEM_WIDTH), D_MODEL),
        "g_mix_pre": gain(ks[5], (DEPTH, D_MODEL)),
        "g_mix_post": gain(ks[6], (DEPTH, D_MODEL)),
        "g_ffn_pre": gain(ks[7], (DEPTH, D_MODEL)),
        "g_ffn_post": gain(ks[8], (DEPTH, D_MODEL)),
        "w_conv_in": w(ks[9], (n_a, D_MODEL, 3 * CONV_WIDTH + MEM_WIDTH), D_MODEL),
        "conv_mix_w": w(ks[10], (n_a, SHORT_CONV, CONV_WIDTH), SHORT_CONV),
        "w_conv_out": w(ks[11], (n_a, CONV_WIDTH + MEM_WIDTH, D_MODEL), CONV_WIDTH + MEM_WIDTH),
        "w_attn_in": w(ks[12], (n_b, D_MODEL, n_attn_in), D_MODEL),
        "w_attn_out": w(ks[13], (n_b, GROUP_WIDTH + MEM_WIDTH, D_MODEL), GROUP_WIDTH + MEM_WIDTH),
        "w_ffn_gate": w(ks[14], (DEPTH, D_MODEL, D_FF), D_MODEL),
        "w_ffn_up": w(ks[15], (DEPTH, D_MODEL, D_FF), D_MODEL),
        "conv_ffn_w": w(ks[16], (DEPTH, SHORT_CONV, D_FF), SHORT_CONV),
        "w_ffn_down": w(ks[17], (DEPTH, D_FF, D_MODEL), D_FF),
    }


def reference(x, mem, positions, g_mem, w_mem_kv, g_mix_pre, g_mix_post, g_ffn_pre, g_ffn_post,
              w_conv_in, conv_mix_w, w_conv_out, w_attn_in, w_attn_out,
              w_ffn_gate, w_ffn_up, conv_ffn_w, w_ffn_down):
    B = x.shape[0]
    kv = (rmsnorm(mem, g_mem) @ w_mem_kv).reshape(B, mem.shape[1], 2, MEM_HEADS, MEM_HEAD_DIM)
    mem_k, mem_v = kv[:, :, 0], kv[:, :, 1]
    cos, sin = rope_tables(positions)
    for i in range(DEPTH):
        j = i // N_MIXERS
        xn = rmsnorm(x, g_mix_pre[i])
        if i % N_MIXERS == 0:
            y = short_conv_mixer(xn, w_conv_in[j], conv_mix_w[j], w_conv_out[j], mem_k, mem_v)
        else:
            y = dilated_attention_mixer(xn, w_attn_in[j], w_attn_out[j], cos, sin, mem_k, mem_v)
        x = x + rmsnorm(y, g_mix_post[i])
        xn = rmsnorm(x, g_ffn_pre[i])
        y = conv_ffn(xn, w_ffn_gate[i], w_ffn_up[i], conv_ffn_w[i], w_ffn_down[i])
        x = x + rmsnorm(y, g_ffn_post[i])
    return x
```

```python
import functools

import jax
import jax.numpy as jnp
from jax import lax
from jax.experimental import pallas as pl
from jax.experimental.pallas import tpu as pltpu

D_MODEL = 4096
SEQ = 8192
DEPTH = 2
HEAD_DIM = 128
N_GROUP_HEADS = 16
GROUP_WIDTH = N_GROUP_HEADS * HEAD_DIM
DILATED_GROUPS = ((128, 1), (512, 4), (2048, 16))
N_GROUPS = len(DILATED_GROUPS)
N_SELF = N_GROUPS * 3 * GROUP_WIDTH
BLOCK = 128
CONV_WIDTH = 3 * D_MODEL // 4
MEM_TOKENS = 256
MEM_HEADS = 4
MEM_HEAD_DIM = 256
MEM_WIDTH = MEM_HEADS * MEM_HEAD_DIM
D_FF = 11008
ROPE_THETA = 10000.0
EPS = 1e-6
NEG_INF = -1e30

V7X_SUBLANES = 8
V7X_LANES = 128
V7X_MXU_COLS = 256
V7X_VMEM_LIMIT_BYTES = 60 * 1024 * 1024

BF16 = jnp.bfloat16
F32 = jnp.float32

ROW_BLOCK = 1024
NORM_ROWS = 256
HALO = V7X_SUBLANES


def _params(n_axes):
    return pltpu.CompilerParams(
        dimension_semantics=("arbitrary",) * n_axes,
        vmem_limit_bytes=V7X_VMEM_LIMIT_BYTES,
    )


def _rms_scale(v):
    return lax.rsqrt(jnp.mean(v * v, axis=-1, keepdims=True) + EPS)


def _bdot(a, b):
    return jnp.dot(a, b, preferred_element_type=F32)


def _norm_kernel(x_ref, g_ref, o_ref):
    x = x_ref[...]
    o_ref[...] = (x * _rms_scale(x) * g_ref[...]).astype(o_ref.dtype)


def _rmsnorm_bf16(x, g_all, layer):
    s, d = x.shape
    return pl.pallas_call(
        _norm_kernel,
        grid=(s // NORM_ROWS,),
        in_specs=[
            pl.BlockSpec((NORM_ROWS, d), lambda i: (i, 0)),
            pl.BlockSpec((None, 1, d), lambda i: (layer, 0, 0)),
        ],
        out_specs=pl.BlockSpec((NORM_ROWS, d), lambda i: (i, 0)),
        out_shape=jax.ShapeDtypeStruct((s, d), BF16),
        compiler_params=_params(1),
        name="rmsnorm_bf16",
    )(x, g_all)


def _residual_kernel(y_ref, x_ref, gpost_ref, gnext_ref, xo_ref, xn_ref):
    y = y_ref[...]
    xnew = x_ref[...] + y * _rms_scale(y) * gpost_ref[...]
    xo_ref[...] = xnew
    xn_ref[...] = (xnew * _rms_scale(xnew) * gnext_ref[...]).astype(xn_ref.dtype)


def _residual_last_kernel(y_ref, x_ref, gpost_ref, xo_ref):
    y = y_ref[...]
    xo_ref[...] = x_ref[...] + y * _rms_scale(y) * gpost_ref[...]


def _residual_norm(y, x, g_post, post_layer, g_next=None, next_layer=None):
    s, d = x.shape
    row = pl.BlockSpec((NORM_ROWS, d), lambda i: (i, 0))
    gspec = lambda layer: pl.BlockSpec((None, 1, d), lambda i: (layer, 0, 0))
    if g_next is None:
        return pl.pallas_call(
            _residual_last_kernel,
            grid=(s // NORM_ROWS,),
            in_specs=[row, row, gspec(post_layer)],
            out_specs=row,
            out_shape=jax.ShapeDtypeStruct((s, d), F32),
            compiler_params=_params(1),
            name="residual_last",
        )(y, x, g_post)
    return pl.pallas_call(
        _residual_kernel,
        grid=(s // NORM_ROWS,),
        in_specs=[row, row, gspec(post_layer), gspec(next_layer)],
        out_specs=[row, row],
        out_shape=[jax.ShapeDtypeStruct((s, d), F32), jax.ShapeDtypeStruct((s, d), BF16)],
        compiler_params=_params(1),
        name="residual_norm",
    )(y, x, g_post, g_next)


def _rope_table_kernel(pos_ref, freq_ref, sign_ref, cos_ref, sin_ref):
    ang = pos_ref[...].astype(F32) * freq_ref[...]
    cos_ref[...] = jnp.cos(ang)
    sin_ref[...] = jnp.sin(ang) * sign_ref[...]


def _rope_tables(positions):
    s = positions.shape[-1]
    half = HEAD_DIM // 2
    inv_freq = ROPE_THETA ** (-jnp.arange(half, dtype=F32) * 2.0 / HEAD_DIM)
    freq = jnp.concatenate([inv_freq, inv_freq]).reshape(1, HEAD_DIM)
    sign = jnp.concatenate([-jnp.ones((half,), F32), jnp.ones((half,), F32)]).reshape(1, HEAD_DIM)
    pos = positions.reshape(s, 1)
    rows = 1024
    tab = pl.BlockSpec((rows, HEAD_DIM), lambda i: (i, 0))
    const = pl.BlockSpec((1, HEAD_DIM), lambda i: (0, 0))
    return pl.pallas_call(
        _rope_table_kernel,
        grid=(s // rows,),
        in_specs=[pl.BlockSpec((rows, 1), lambda i: (i, 0)), const, const],
        out_specs=[tab, tab],
        out_shape=[jax.ShapeDtypeStruct((s, HEAD_DIM), F32)] * 2,
        compiler_params=_params(1),
        name="rope_tables",
    )(pos, freq, sign)


def _mem_kv_kernel(mem_ref, g_ref, w_ref, o_ref):
    m = mem_ref[...]
    mn = (m * _rms_scale(m) * g_ref[...]).astype(BF16)
    o_ref[...] = _bdot(mn, w_ref[...].astype(BF16)).astype(o_ref.dtype)


def _mem_kv(mem, g_mem, w_mem_kv):
    m, d = mem.shape
    n = w_mem_kv.shape[1]
    bn = 512
    return pl.pallas_call(
        _mem_kv_kernel,
        grid=(n // bn,),
        in_specs=[
            pl.BlockSpec((m, d), lambda j: (0, 0)),
            pl.BlockSpec((1, d), lambda j: (0, 0)),
            pl.BlockSpec((d, bn), lambda j: (0, j)),
        ],
        out_specs=pl.BlockSpec((m, bn), lambda j: (0, j)),
        out_shape=jax.ShapeDtypeStruct((m, n), BF16),
        compiler_params=_params(1),
        name="mem_kv",
    )(mem, g_mem.reshape(1, d), w_mem_kv)


def _causal_conv3(v, cw_ref, ext_ref, tails_ref, i, j):
    bm = v.shape[0]
    prev = tails_ref[j]
    ext_ref[0:HALO, :] = jnp.where(i == 0, jnp.zeros_like(prev), prev)
    ext_ref[HALO:HALO + bm, :] = v
    tails_ref[j] = v[bm - HALO:bm, :]
    v1 = ext_ref[HALO - 1:HALO - 1 + bm, :]
    v2 = ext_ref[HALO - 2:HALO - 2 + bm, :]
    return cw_ref[0:1, :] * v2 + cw_ref[1:2, :] * v1 + cw_ref[2:3, :] * v


def _conv_in_kernel(x_ref, wb_ref, wc_ref, wu_ref, cw_ref, o_ref, ext_ref, tails_ref):
    i, j = pl.program_id(0), pl.program_id(1)
    x = x_ref[...]
    b = _bdot(x, wb_ref[...].astype(BF16))
    c = _bdot(x, wc_ref[...].astype(BF16))
    u = _bdot(x, wu_ref[...].astype(BF16))
    conv = _causal_conv3(c * u, cw_ref, ext_ref, tails_ref, i, j)
    o_ref[...] = (b * conv).astype(o_ref.dtype)


def _conv_in(xn, w_in, conv_w, layer):
    s, d = xn.shape
    bm, bn = ROW_BLOCK, V7X_MXU_COLS
    nj = CONV_WIDTH // bn
    wspec = lambda off: pl.BlockSpec((None, d, bn), lambda i, j: (layer, 0, j + off))
    return pl.pallas_call(
        _conv_in_kernel,
        grid=(s // bm, nj),
        in_specs=[
            pl.BlockSpec((bm, d), lambda i, j: (i, 0)),
            wspec(0), wspec(nj), wspec(2 * nj),
            pl.BlockSpec((None, 3, bn), lambda i, j: (layer, 0, j)),
        ],
        out_specs=pl.BlockSpec((bm, bn), lambda i, j: (i, j)),
        out_shape=jax.ShapeDtypeStruct((s, CONV_WIDTH), BF16),
        scratch_shapes=[
            pltpu.VMEM((bm + HALO, bn), F32),
            pltpu.VMEM((nj, HALO, bn), F32),
        ],
        compiler_params=_params(2),
        name="conv_in",
    )(xn, w_in, w_in, w_in, conv_w)


def _mem_q_kernel(x_ref, wq_ref, k_ref, v_ref, o_ref):
    q = _bdot(x_ref[...], wq_ref[...].astype(BF16)) * (MEM_HEAD_DIM ** -0.5)
    s = lax.dot_general(q.astype(BF16), k_ref[...], (((1,), (1,)), ((), ())),
                        preferred_element_type=F32)
    m = jnp.max(s, axis=-1, keepdims=True)
    p = jnp.exp(s - m)
    den = jnp.sum(p, axis=-1, keepdims=True)
    o = _bdot((p / den).astype(BF16), v_ref[...])
    o_ref[...] = o.astype(o_ref.dtype)


def _mem_attend(xn, w_in, layer, q_col_block, kv):
    s, d = xn.shape
    bm, bn = ROW_BLOCK, MEM_HEAD_DIM
    return pl.pallas_call(
        _mem_q_kernel,
        grid=(s // bm, MEM_HEADS),
        in_specs=[
            pl.BlockSpec((bm, d), lambda i, h: (i, 0)),
            pl.BlockSpec((None, d, bn), lambda i, h: (layer, 0, q_col_block + h)),
            pl.BlockSpec((MEM_TOKENS, bn), lambda i, h: (0, h)),
            pl.BlockSpec((MEM_TOKENS, bn), lambda i, h: (0, MEM_HEADS + h)),
        ],
        out_specs=pl.BlockSpec((bm, bn), lambda i, h: (i, h)),
        out_shape=jax.ShapeDtypeStruct((s, MEM_WIDTH), BF16),
        compiler_params=_params(2),
        name="mem_attend",
    )(xn, w_in, kv, kv)


def _out_proj_kernel(a_ref, b_ref, wa_ref, wb_ref, o_ref):
    acc = _bdot(a_ref[...], wa_ref[...].astype(BF16))
    acc = acc + _bdot(b_ref[...], wb_ref[...].astype(BF16))
    o_ref[...] = acc


def _out_proj(a, b, w, layer):
    s, ka = a.shape
    kb = b.shape[1]
    n = w.shape[2]
    bm, bn = ROW_BLOCK, 512
    assert ka % kb == 0
    return pl.pallas_call(
        _out_proj_kernel,
        grid=(s // bm, n // bn),
        in_specs=[
            pl.BlockSpec((bm, ka), lambda i, j: (i, 0)),
            pl.BlockSpec((bm, kb), lambda i, j: (i, 0)),
            pl.BlockSpec((None, ka, bn), lambda i, j: (layer, 0, j)),
            pl.BlockSpec((None, kb, bn), lambda i, j: (layer, ka // kb, j)),
        ],
        out_specs=pl.BlockSpec((bm, bn), lambda i, j: (i, j)),
        out_shape=jax.ShapeDtypeStruct((s, n), F32),
        compiler_params=_params(2),
        name="out_proj",
    )(a, b, w, w)


def _ffn_in_kernel(x_ref, wg_ref, wu_ref, cw_ref, o_ref, ext_ref, tails_ref):
    i, j = pl.program_id(0), pl.program_id(1)
    x = x_ref[...]
    g = _bdot(x, wg_ref[...].astype(BF16))
    u = _bdot(x, wu_ref[...].astype(BF16))
    gc = _causal_conv3(g, cw_ref, ext_ref, tails_ref, i, j)
    act = gc / (1.0 + jnp.exp(-gc))
    o_ref[...] = (act * u).astype(o_ref.dtype)


def _ffn_in(xn, w_gate, w_up, conv_w, layer):
    s, d = xn.shape
    bm, bn = ROW_BLOCK, V7X_MXU_COLS
    nj = D_FF // bn
    wspec = pl.BlockSpec((None, d, bn), lambda i, j: (layer, 0, j))
    return pl.pallas_call(
        _ffn_in_kernel,
        grid=(s // bm, nj),
        in_specs=[
            pl.BlockSpec((bm, d), lambda i, j: (i, 0)),
            wspec, wspec,
            pl.BlockSpec((None, 3, bn), lambda i, j: (layer, 0, j)),
        ],
        out_specs=pl.BlockSpec((bm, bn), lambda i, j: (i, j)),
        out_shape=jax.ShapeDtypeStruct((s, D_FF), BF16),
        scratch_shapes=[
            pltpu.VMEM((bm + HALO, bn), F32),
            pltpu.VMEM((nj, HALO, bn), F32),
        ],
        compiler_params=_params(2),
        name="ffn_in",
    )(xn, w_gate, w_up, conv_w)


def _ffn_out_kernel(a_ref, w_ref, o_ref):
    k = pl.program_id(1)

    @pl.when(k == 0)
    def _():
        o_ref[...] = jnp.zeros_like(o_ref)

    a = a_ref[...]
    for c in range(0, o_ref.shape[1], FFN_OUT_COL_CHUNK):
        cols = slice(c, c + FFN_OUT_COL_CHUNK)
        o_ref[:, cols] += _bdot(a, w_ref[:, cols].astype(BF16))


FFN_OUT_COL_CHUNK = 512


def _ffn_out(a, w_down, layer):
    s, kdim = a.shape
    n = w_down.shape[2]
    bm, bk = ROW_BLOCK, V7X_MXU_COLS
    return pl.pallas_call(
        _ffn_out_kernel,
        grid=(s // bm, kdim // bk),
        in_specs=[
            pl.BlockSpec((bm, bk), lambda i, k: (i, k)),
            pl.BlockSpec((None, bk, n), lambda i, k: (layer, k, 0)),
        ],
        out_specs=pl.BlockSpec((bm, n), lambda i, k: (i, 0)),
        out_shape=jax.ShapeDtypeStruct((s, n), F32),
        compiler_params=_params(2),
        name="ffn_out",
    )(a, w_down)


ATTN_IN_COLS = 512
HEADS_PER_ATTN_IN_BLOCK = ATTN_IN_COLS // HEAD_DIM
ATTN_IN_BLOCKS_PER_SECTION = GROUP_WIDTH // ATTN_IN_COLS


def _attn_in_kernel(x_ref, w_ref, cos_ref, sin_ref, o_ref):
    j = pl.program_id(1)
    acc = _bdot(x_ref[...], w_ref[...].astype(BF16))
    section = (j // ATTN_IN_BLOCKS_PER_SECTION) % 3

    @pl.when(section == 2)
    def _():
        o_ref[...] = acc.astype(o_ref.dtype)

    @pl.when(section != 2)
    def _():
        scale = jnp.where(section == 0, HEAD_DIM ** -0.5, 1.0).astype(F32)
        cos = cos_ref[...] * scale
        sin = sin_ref[...] * scale
        for h in range(HEADS_PER_ATTN_IN_BLOCK):
            cols = slice(h * HEAD_DIM, (h + 1) * HEAD_DIM)
            t = acc[:, cols]
            rot = pltpu.roll(t, HEAD_DIM // 2, axis=1)
            o_ref[:, cols] = (t * cos + rot * sin).astype(o_ref.dtype)


def _attn_in(xn, w_in, cos_tab, sin_tab, layer):
    s, d = xn.shape
    bm, bn = ROW_BLOCK, ATTN_IN_COLS
    tab = pl.BlockSpec((bm, HEAD_DIM), lambda i, j: (i, 0))
    return pl.pallas_call(
        _attn_in_kernel,
        grid=(s // bm, N_SELF // bn),
        in_specs=[
            pl.BlockSpec((bm, d), lambda i, j: (i, 0)),
            pl.BlockSpec((None, d, bn), lambda i, j: (layer, 0, j)),
            tab, tab,
        ],
        out_specs=pl.BlockSpec((bm, bn), lambda i, j: (i, j)),
        out_shape=jax.ShapeDtypeStruct((s, N_SELF), BF16),
        compiler_params=_params(2),
        name="attn_in",
    )(xn, w_in, cos_tab, sin_tab)


def _window_attn_kernel(q_ref, k_ref, v_ref, o_ref, lse_ref, kprev_ref, vprev_ref):
    nb = pl.program_id(1)

    @pl.when(nb == 0)
    def _():
        kprev_ref[...] = jnp.zeros_like(kprev_ref)
        vprev_ref[...] = jnp.zeros_like(vprev_ref)

    qi = lax.broadcasted_iota(jnp.int32, (BLOCK, BLOCK), 0)
    ki = lax.broadcasted_iota(jnp.int32, (BLOCK, BLOCK), 1)
    own_valid = ki <= qi
    prev_valid = jnp.logical_and(ki >= qi, nb > 0)
    lane = lax.broadcasted_iota(jnp.int32, (BLOCK, V7X_LANES), 1)
    lse_all = jnp.zeros((BLOCK, V7X_LANES), F32)
    nt = (((1,), (1,)), ((), ()))
    for h in range(N_GROUP_HEADS):
        cols = slice(h * HEAD_DIM, (h + 1) * HEAD_DIM)
        q = q_ref[:, cols]
        s_own = lax.dot_general(q, k_ref[:, cols], nt, preferred_element_type=F32)
        s_prev = lax.dot_general(q, kprev_ref[:, cols], nt, preferred_element_type=F32)
        s_own = jnp.where(own_valid, s_own, NEG_INF)
        s_prev = jnp.where(prev_valid, s_prev, NEG_INF)
        m = jnp.maximum(jnp.max(s_own, axis=-1, keepdims=True),
                        jnp.max(s_prev, axis=-1, keepdims=True))
        p_own = jnp.exp(s_own - m)
        p_prev = jnp.exp(s_prev - m)
        den = jnp.sum(p_own, axis=-1, keepdims=True) + jnp.sum(p_prev, axis=-1, keepdims=True)
        o = _bdot(p_own.astype(BF16), v_ref[:, cols]) + _bdot(p_prev.astype(BF16), vprev_ref[:, cols])
        o_ref[:, cols] = (o / den).astype(o_ref.dtype)
        lse_all = jnp.where(lane == h, m + jnp.log(den), lse_all)
    lse_ref[...] = lse_all
    kprev_ref[...] = k_ref[...]
    vprev_ref[...] = v_ref[...]


def _window_attn(h_attn, group):
    s = h_attn.shape[0]
    _, d = DILATED_GROUPS[group]
    length = s // d
    hv = h_attn.reshape(length, d * N_SELF)
    sections_per_row = N_SELF // GROUP_WIDTH
    spec = lambda sec: pl.BlockSpec(
        (BLOCK, GROUP_WIDTH), lambda r, nb: (nb, r * sections_per_row + 3 * group + sec))
    o, lse = pl.pallas_call(
        _window_attn_kernel,
        grid=(d, length // BLOCK),
        in_specs=[spec(0), spec(1), spec(2)],
        out_specs=[
            pl.BlockSpec((BLOCK, GROUP_WIDTH), lambda r, nb: (nb, r)),
            pl.BlockSpec((BLOCK, V7X_LANES), lambda r, nb: (nb, r)),
        ],
        out_shape=[
            jax.ShapeDtypeStruct((length, d * GROUP_WIDTH), F32),
            jax.ShapeDtypeStruct((length, d * V7X_LANES), F32),
        ],
        scratch_shapes=[pltpu.VMEM((BLOCK, GROUP_WIDTH), BF16)] * 2,
        compiler_params=_params(2),
        name=f"window_attn_g{group}",
    )(hv, hv, hv)
    return o.reshape(s, GROUP_WIDTH), lse.reshape(s, V7X_LANES)


def _merge_kernel(o0_ref, o1_ref, o2_ref, l0_ref, l1_ref, l2_ref, out_ref):
    l0, l1, l2 = l0_ref[...], l1_ref[...], l2_ref[...]
    m = jnp.maximum(jnp.maximum(l0, l1), l2)
    e0, e1, e2 = jnp.exp(l0 - m), jnp.exp(l1 - m), jnp.exp(l2 - m)
    tot = e0 + e1 + e2
    a0, a1, a2 = e0 / tot, e1 / tot, e2 / tot
    for h in range(N_GROUP_HEADS):
        cols = slice(h * HEAD_DIM, (h + 1) * HEAD_DIM)
        merged = (a0[:, h:h + 1] * o0_ref[:, cols] + a1[:, h:h + 1] * o1_ref[:, cols]
                  + a2[:, h:h + 1] * o2_ref[:, cols])
        out_ref[:, cols] = merged.astype(out_ref.dtype)


def _merge_groups(outs, lses):
    s = outs[0].shape[0]
    rows = 512
    ospec = pl.BlockSpec((rows, GROUP_WIDTH), lambda i: (i, 0))
    lspec = pl.BlockSpec((rows, V7X_LANES), lambda i: (i, 0))
    return pl.pallas_call(
        _merge_kernel,
        grid=(s // rows,),
        in_specs=[ospec] * 3 + [lspec] * 3,
        out_specs=ospec,
        out_shape=jax.ShapeDtypeStruct((s, GROUP_WIDTH), BF16),
        compiler_params=_params(1),
        name="merge_groups",
    )(*outs, *lses)


def kernel(x, mem, positions, g_mem, w_mem_kv, g_mix_pre, g_mix_post, g_ffn_pre, g_ffn_post,
           w_conv_in, conv_mix_w, w_conv_out, w_attn_in, w_attn_out,
           w_ffn_gate, w_ffn_up, conv_ffn_w, w_ffn_down):
    b, s, d = x.shape
    assert b == 1 and s == SEQ and d == D_MODEL
    xs = x.reshape(s, d)
    gains = lambda g: g.reshape(DEPTH, 1, d)
    g_mix_pre, g_mix_post = gains(g_mix_pre), gains(g_mix_post)
    g_ffn_pre, g_ffn_post = gains(g_ffn_pre), gains(g_ffn_post)

    kv = _mem_kv(mem.reshape(MEM_TOKENS, d), g_mem, w_mem_kv)
    cos_tab, sin_tab = _rope_tables(positions)

    xn = _rmsnorm_bf16(xs, g_mix_pre, 0)
    for i in range(DEPTH):
        j = i // 2
        if i % 2 == 0:
            y_mix = _conv_in(xn, w_conv_in, conv_mix_w, j)
            y_mem = _mem_attend(xn, w_conv_in, j, 3 * CONV_WIDTH // MEM_HEAD_DIM, kv)
            y = _out_proj(y_mix, y_mem, w_conv_out, j)
        else:
            h_attn = _attn_in(xn, w_attn_in, cos_tab, sin_tab, j)
            parts = [_window_attn(h_attn, g) for g in range(N_GROUPS)]
            y_mix = _merge_groups([p[0] for p in parts], [p[1] for p in parts])
            y_mem = _mem_attend(xn, w_attn_in, j, N_SELF // MEM_HEAD_DIM, kv)
            y = _out_proj(y_mix, y_mem, w_attn_out, j)
        xs, xn = _residual_norm(y, xs, g_mix_post, i, g_ffn_pre, i)
        a = _ffn_in(xn, w_ffn_gate, w_ffn_up, conv_ffn_w, i)
        y = _ffn_out(a, w_ffn_down, i)
        if i + 1 < DEPTH:
            xs, xn = _residual_norm(y, xs, g_ffn_post, i, g_mix_pre, i + 1)
        else:
            xs = _residual_norm(y, xs, g_ffn_post, i)
    return xs.reshape(b, s, d)
```

```python
import functools

import jax
import jax.numpy as jnp
from jax import lax
from jax.experimental import pallas as pl
from jax.experimental.pallas import tpu as pltpu

D_MODEL = 4096
SEQ = 8192
DEPTH = 2
HEAD_DIM = 128
N_GROUP_HEADS = 16
GROUP_WIDTH = N_GROUP_HEADS * HEAD_DIM
DILATED_GROUPS = ((128, 1), (512, 4), (2048, 16))
N_GROUPS = len(DILATED_GROUPS)
N_SELF = N_GROUPS * 3 * GROUP_WIDTH
BLOCK = 128
CONV_WIDTH = 3 * D_MODEL // 4
MEM_TOKENS = 256
MEM_HEADS = 4
MEM_HEAD_DIM = 256
MEM_WIDTH = MEM_HEADS * MEM_HEAD_DIM
D_FF = 11008
ROPE_THETA = 10000.0
EPS = 1e-6
NEG_INF = -1e30

V7X_SUBLANES = 8
V7X_LANES = 128
V7X_MXU_COLS = 256
V7X_VMEM_LIMIT_BYTES = 60 * 1024 * 1024

BF16 = jnp.bfloat16
F32 = jnp.float32

ROW_BLOCK = 1024
NORM_ROWS = 256
HALO = V7X_SUBLANES


def _params(n_axes):
    return pltpu.CompilerParams(
        dimension_semantics=("arbitrary",) * n_axes,
        vmem_limit_bytes=V7X_VMEM_LIMIT_BYTES,
    )


def _rms_scale(v):
    return lax.rsqrt(jnp.mean(v * v, axis=-1, keepdims=True) + EPS)


def _bdot(a, b):
    return jnp.dot(a, b, preferred_element_type=F32)


def _norm_kernel(x_ref, g_ref, o_ref):
    x = x_ref[...]
    o_ref[...] = (x * _rms_scale(x) * g_ref[...]).astype(o_ref.dtype)


def _rmsnorm_bf16(x, g_all, layer):
    s, d = x.shape
    return pl.pallas_call(
        _norm_kernel,
        grid=(s // NORM_ROWS,),
        in_specs=[
            pl.BlockSpec((NORM_ROWS, d), lambda i: (i, 0)),
            pl.BlockSpec((None, 1, d), lambda i: (layer, 0, 0)),
        ],
        out_specs=pl.BlockSpec((NORM_ROWS, d), lambda i: (i, 0)),
        out_shape=jax.ShapeDtypeStruct((s, d), BF16),
        compiler_params=_params(1),
        name="rmsnorm_bf16",
    )(x, g_all)


def _residual_kernel(y_ref, x_ref, gpost_ref, gnext_ref, xo_ref, xn_ref):
    y = y_ref[...]
    xnew = x_ref[...] + y * _rms_scale(y) * gpost_ref[...]
    xo_ref[...] = xnew
    xn_ref[...] = (xnew * _rms_scale(xnew) * gnext_ref[...]).astype(xn_ref.dtype)


def _row_permutation(rows, d, to_class_major):
    n = rows // d
    out_row = lax.broadcasted_iota(jnp.int32, (rows, rows), 0)
    in_row = lax.broadcasted_iota(jnp.int32, (rows, rows), 1)
    if to_class_major:
        src = (out_row & (n - 1)) * d + (out_row >> (n.bit_length() - 1))
    else:
        src = (out_row & (d - 1)) * n + (out_row >> (d.bit_length() - 1))
    return (in_row == src).astype(BF16)


def _residual_dilated_kernel(y_ref, x_ref, gpost_ref, gnext_ref, xo_ref, xn_ref, *xp_refs):
    y = y_ref[...]
    xnew = x_ref[...] + y * _rms_scale(y) * gpost_ref[...]
    xo_ref[...] = xnew
    xn = (xnew * _rms_scale(xnew) * gnext_ref[...]).astype(BF16)
    xn_ref[...] = xn
    for xp_ref in xp_refs:
        d, n = xp_ref.shape[0], xp_ref.shape[1]
        xp = _bdot(_row_permutation(NORM_ROWS, d, True), xn).astype(BF16)
        for r in range(d):
            xp_ref[r] = xp[r * n:(r + 1) * n, :]


def _residual_last_kernel(y_ref, x_ref, gpost_ref, xo_ref):
    y = y_ref[...]
    xo_ref[...] = x_ref[...] + y * _rms_scale(y) * gpost_ref[...]


def _residual_norm(y, x, g_post, post_layer, g_next=None, next_layer=None, dilations=()):
    s, d = x.shape
    row = pl.BlockSpec((NORM_ROWS, d), lambda i: (i, 0))
    gspec = lambda layer: pl.BlockSpec((None, 1, d), lambda i: (layer, 0, 0))
    if dilations:
        return pl.pallas_call(
            _residual_dilated_kernel,
            grid=(s // NORM_ROWS,),
            in_specs=[row, row, gspec(post_layer), gspec(next_layer)],
            out_specs=[row, row] + [
                pl.BlockSpec((dil, NORM_ROWS // dil, d), lambda i: (0, i, 0)) for dil in dilations],
            out_shape=[jax.ShapeDtypeStruct((s, d), F32), jax.ShapeDtypeStruct((s, d), BF16)] + [
                jax.ShapeDtypeStruct((dil, s // dil, d), BF16) for dil in dilations],
            compiler_params=_params(1),
            name="residual_norm_dilated",
        )(y, x, g_post, g_next)
    if g_next is None:
        return pl.pallas_call(
            _residual_last_kernel,
            grid=(s // NORM_ROWS,),
            in_specs=[row, row, gspec(post_layer)],
            out_specs=row,
            out_shape=jax.ShapeDtypeStruct((s, d), F32),
            compiler_params=_params(1),
            name="residual_last",
        )(y, x, g_post)
    return pl.pallas_call(
        _residual_kernel,
        grid=(s // NORM_ROWS,),
        in_specs=[row, row, gspec(post_layer), gspec(next_layer)],
        out_specs=[row, row],
        out_shape=[jax.ShapeDtypeStruct((s, d), F32), jax.ShapeDtypeStruct((s, d), BF16)],
        compiler_params=_params(1),
        name="residual_norm",
    )(y, x, g_post, g_next)


def _rope_table_kernel(pos_ref, freq_ref, sign_ref, cos_ref, sin_ref):
    ang = pos_ref[...].astype(F32) * freq_ref[...]
    cos_ref[...] = jnp.cos(ang)
    sin_ref[...] = jnp.sin(ang) * sign_ref[...]


def _rope_tables(positions):
    s = positions.shape[-1]
    half = HEAD_DIM // 2
    inv_freq = ROPE_THETA ** (-jnp.arange(half, dtype=F32) * 2.0 / HEAD_DIM)
    freq = jnp.concatenate([inv_freq, inv_freq]).reshape(1, HEAD_DIM)
    sign = jnp.concatenate([-jnp.ones((half,), F32), jnp.ones((half,), F32)]).reshape(1, HEAD_DIM)
    pos = jnp.stack([positions.reshape(s // d, d).T.reshape(s, 1) for _, d in DILATED_GROUPS])
    rows = 1024
    tab = pl.BlockSpec((None, rows, HEAD_DIM), lambda g, i: (g, i, 0))
    const = pl.BlockSpec((1, HEAD_DIM), lambda g, i: (0, 0))
    return pl.pallas_call(
        _rope_table_kernel,
        grid=(N_GROUPS, s // rows),
        in_specs=[pl.BlockSpec((None, rows, 1), lambda g, i: (g, i, 0)), const, const],
        out_specs=[tab, tab],
        out_shape=[jax.ShapeDtypeStruct((N_GROUPS, s, HEAD_DIM), F32)] * 2,
        compiler_params=_params(2),
        name="rope_tables",
    )(pos, freq, sign)


def _mem_kv_kernel(mem_ref, g_ref, w_ref, o_ref):
    m = mem_ref[...]
    mn = (m * _rms_scale(m) * g_ref[...]).astype(BF16)
    o_ref[...] = _bdot(mn, w_ref[...].astype(BF16)).astype(o_ref.dtype)


def _mem_kv(mem, g_mem, w_mem_kv):
    m, d = mem.shape
    n = w_mem_kv.shape[1]
    bn = 512
    return pl.pallas_call(
        _mem_kv_kernel,
        grid=(n // bn,),
        in_specs=[
            pl.BlockSpec((m, d), lambda j: (0, 0)),
            pl.BlockSpec((1, d), lambda j: (0, 0)),
            pl.BlockSpec((d, bn), lambda j: (0, j)),
        ],
        out_specs=pl.BlockSpec((m, bn), lambda j: (0, j)),
        out_shape=jax.ShapeDtypeStruct((m, n), BF16),
        compiler_params=_params(1),
        name="mem_kv",
    )(mem, g_mem.reshape(1, d), w_mem_kv)


def _causal_conv3(v, cw_ref, ext_ref, tails_ref, i, j):
    bm = v.shape[0]
    prev = tails_ref[j]
    ext_ref[0:HALO, :] = jnp.where(i == 0, jnp.zeros_like(prev), prev)
    ext_ref[HALO:HALO + bm, :] = v
    tails_ref[j] = v[bm - HALO:bm, :]
    v1 = ext_ref[HALO - 1:HALO - 1 + bm, :]
    v2 = ext_ref[HALO - 2:HALO - 2 + bm, :]
    return cw_ref[0:1, :] * v2 + cw_ref[1:2, :] * v1 + cw_ref[2:3, :] * v


def _conv_in_kernel(x_ref, wb_ref, wc_ref, wu_ref, cw_ref, o_ref, ext_ref, tails_ref):
    i, j = pl.program_id(0), pl.program_id(1)
    x = x_ref[...]
    b = _bdot(x, wb_ref[...].astype(BF16))
    c = _bdot(x, wc_ref[...].astype(BF16))
    u = _bdot(x, wu_ref[...].astype(BF16))
    conv = _causal_conv3(c * u, cw_ref, ext_ref, tails_ref, i, j)
    o_ref[...] = (b * conv).astype(o_ref.dtype)


def _conv_in(xn, w_in, conv_w, layer):
    s, d = xn.shape
    bm, bn = ROW_BLOCK, V7X_MXU_COLS
    nj = CONV_WIDTH // bn
    wspec = lambda off: pl.BlockSpec((None, d, bn), lambda i, j: (layer, 0, j + off))
    return pl.pallas_call(
        _conv_in_kernel,
        grid=(s // bm, nj),
        in_specs=[
            pl.BlockSpec((bm, d), lambda i, j: (i, 0)),
            wspec(0), wspec(nj), wspec(2 * nj),
            pl.BlockSpec((None, 3, bn), lambda i, j: (layer, 0, j)),
        ],
        out_specs=pl.BlockSpec((bm, bn), lambda i, j: (i, j)),
        out_shape=jax.ShapeDtypeStruct((s, CONV_WIDTH), BF16),
        scratch_shapes=[
            pltpu.VMEM((bm + HALO, bn), F32),
            pltpu.VMEM((nj, HALO, bn), F32),
        ],
        compiler_params=_params(2),
        name="conv_in",
    )(xn, w_in, w_in, w_in, conv_w)


def _mem_q_kernel(x_ref, wq_ref, k_ref, v_ref, o_ref):
    q = _bdot(x_ref[...], wq_ref[...].astype(BF16)) * (MEM_HEAD_DIM ** -0.5)
    s = lax.dot_general(q.astype(BF16), k_ref[...], (((1,), (1,)), ((), ())),
                        preferred_element_type=F32)
    m = jnp.max(s, axis=-1, keepdims=True)
    p = jnp.exp(s - m)
    den = jnp.sum(p, axis=-1, keepdims=True)
    o = _bdot((p / den).astype(BF16), v_ref[...])
    o_ref[...] = o.astype(o_ref.dtype)


def _mem_attend(xn, w_in, layer, q_col_block, kv):
    s, d = xn.shape
    bm, bn = ROW_BLOCK, MEM_HEAD_DIM
    return pl.pallas_call(
        _mem_q_kernel,
        grid=(s // bm, MEM_HEADS),
        in_specs=[
            pl.BlockSpec((bm, d), lambda i, h: (i, 0)),
            pl.BlockSpec((None, d, bn), lambda i, h: (layer, 0, q_col_block + h)),
            pl.BlockSpec((MEM_TOKENS, bn), lambda i, h: (0, h)),
            pl.BlockSpec((MEM_TOKENS, bn), lambda i, h: (0, MEM_HEADS + h)),
        ],
        out_specs=pl.BlockSpec((bm, bn), lambda i, h: (i, h)),
        out_shape=jax.ShapeDtypeStruct((s, MEM_WIDTH), BF16),
        compiler_params=_params(2),
        name="mem_attend",
    )(xn, w_in, kv, kv)


def _out_proj_kernel(a_ref, b_ref, wa_ref, wb_ref, o_ref):
    acc = _bdot(a_ref[...], wa_ref[...].astype(BF16))
    acc = acc + _bdot(b_ref[...], wb_ref[...].astype(BF16))
    o_ref[...] = acc


def _out_proj(a, b, w, layer):
    s, ka = a.shape
    kb = b.shape[1]
    n = w.shape[2]
    bm, bn = ROW_BLOCK, 512
    assert ka % kb == 0
    return pl.pallas_call(
        _out_proj_kernel,
        grid=(s // bm, n // bn),
        in_specs=[
            pl.BlockSpec((bm, ka), lambda i, j: (i, 0)),
            pl.BlockSpec((bm, kb), lambda i, j: (i, 0)),
            pl.BlockSpec((None, ka, bn), lambda i, j: (layer, 0, j)),
            pl.BlockSpec((None, kb, bn), lambda i, j: (layer, ka // kb, j)),
        ],
        out_specs=pl.BlockSpec((bm, bn), lambda i, j: (i, j)),
        out_shape=jax.ShapeDtypeStruct((s, n), F32),
        compiler_params=_params(2),
        name="out_proj",
    )(a, b, w, w)


def _ffn_in_kernel(x_ref, wg_ref, wu_ref, cw_ref, o_ref, ext_ref, tails_ref):
    i, j = pl.program_id(0), pl.program_id(1)
    x = x_ref[...]
    g = _bdot(x, wg_ref[...].astype(BF16))
    u = _bdot(x, wu_ref[...].astype(BF16))
    gc = _causal_conv3(g, cw_ref, ext_ref, tails_ref, i, j)
    act = gc / (1.0 + jnp.exp(-gc))
    o_ref[...] = (act * u).astype(o_ref.dtype)


def _ffn_in(xn, w_gate, w_up, conv_w, layer):
    s, d = xn.shape
    bm, bn = ROW_BLOCK, V7X_MXU_COLS
    nj = D_FF // bn
    wspec = pl.BlockSpec((None, d, bn), lambda i, j: (layer, 0, j))
    return pl.pallas_call(
        _ffn_in_kernel,
        grid=(s // bm, nj),
        in_specs=[
            pl.BlockSpec((bm, d), lambda i, j: (i, 0)),
            wspec, wspec,
            pl.BlockSpec((None, 3, bn), lambda i, j: (layer, 0, j)),
        ],
        out_specs=pl.BlockSpec((bm, bn), lambda i, j: (i, j)),
        out_shape=jax.ShapeDtypeStruct((s, D_FF), BF16),
        scratch_shapes=[
            pltpu.VMEM((bm + HALO, bn), F32),
            pltpu.VMEM((nj, HALO, bn), F32),
        ],
        compiler_params=_params(2),
        name="ffn_in",
    )(xn, w_gate, w_up, conv_w)


def _ffn_out_kernel(a_ref, w_ref, o_ref):
    k = pl.program_id(1)

    @pl.when(k == 0)
    def _():
        o_ref[...] = jnp.zeros_like(o_ref)

    a = a_ref[...]
    for c in range(0, o_ref.shape[1], FFN_OUT_COL_CHUNK):
        cols = slice(c, c + FFN_OUT_COL_CHUNK)
        o_ref[:, cols] += _bdot(a, w_ref[:, cols].astype(BF16))


FFN_OUT_COL_CHUNK = 512


def _ffn_out(a, w_down, layer):
    s, kdim = a.shape
    n = w_down.shape[2]
    bm, bk = ROW_BLOCK, V7X_MXU_COLS
    return pl.pallas_call(
        _ffn_out_kernel,
        grid=(s // bm, kdim // bk),
        in_specs=[
            pl.BlockSpec((bm, bk), lambda i, k: (i, k)),
            pl.BlockSpec((None, bk, n), lambda i, k: (layer, k, 0)),
        ],
        out_specs=pl.BlockSpec((bm, n), lambda i, k: (i, 0)),
        out_shape=jax.ShapeDtypeStruct((s, n), F32),
        compiler_params=_params(2),
        name="ffn_out",
    )(a, w_down)


ATTN_IN_COLS = 512
HEADS_PER_ATTN_IN_BLOCK = ATTN_IN_COLS // HEAD_DIM
ATTN_IN_BLOCKS_PER_SECTION = GROUP_WIDTH // ATTN_IN_COLS
ATTN_IN_BLOCKS_PER_GROUP = 3 * ATTN_IN_BLOCKS_PER_SECTION


def _attn_in_kernel(x_ref, w_ref, cos_ref, sin_ref, o_ref):
    j = pl.program_id(1)
    acc = _bdot(x_ref[...], w_ref[...].astype(BF16))
    section = j // ATTN_IN_BLOCKS_PER_SECTION

    @pl.when(section == 2)
    def _():
        o_ref[...] = acc.astype(o_ref.dtype)

    @pl.when(section != 2)
    def _():
        scale = jnp.where(section == 0, HEAD_DIM ** -0.5, 1.0).astype(F32)
        cos = cos_ref[...] * scale
        sin = sin_ref[...] * scale
        for h in range(HEADS_PER_ATTN_IN_BLOCK):
            cols = slice(h * HEAD_DIM, (h + 1) * HEAD_DIM)
            t = acc[:, cols]
            rot = pltpu.roll(t, HEAD_DIM // 2, axis=1)
            o_ref[:, cols] = (t * cos + rot * sin).astype(o_ref.dtype)


def _attn_in(xn, w_in, cos_tab, sin_tab, layer, group):
    s, d = xn.shape
    bm, bn = ROW_BLOCK, ATTN_IN_COLS
    tab = pl.BlockSpec((None, bm, HEAD_DIM), lambda i, j: (group, i, 0))
    return pl.pallas_call(
        _attn_in_kernel,
        grid=(s // bm, ATTN_IN_BLOCKS_PER_GROUP),
        in_specs=[
            pl.BlockSpec((bm, d), lambda i, j: (i, 0)),
            pl.BlockSpec((None, d, bn), lambda i, j: (layer, 0, group * ATTN_IN_BLOCKS_PER_GROUP + j)),
            tab, tab,
        ],
        out_specs=pl.BlockSpec((bm, bn), lambda i, j: (i, j)),
        out_shape=jax.ShapeDtypeStruct((s, 3 * GROUP_WIDTH), BF16),
        compiler_params=_params(2),
        name=f"attn_in_g{group}",
    )(xn, w_in, cos_tab, sin_tab)


def _window_attn_kernel(q_ref, k_ref, v_ref, o_ref, lse_ref, k2_ref, v2_ref):
    nb = pl.program_id(1)

    @pl.when(nb == 0)
    def _():
        k2_ref[...] = jnp.zeros_like(k2_ref)
        v2_ref[...] = jnp.zeros_like(v2_ref)

    own_half = nb % 2
    own_rows = pl.ds(pl.multiple_of(own_half * BLOCK, BLOCK), BLOCK)
    k2_ref[own_rows, :] = k_ref[...]
    v2_ref[own_rows, :] = v_ref[...]

    qi = lax.broadcasted_iota(jnp.int32, (BLOCK, 2 * BLOCK), 0)
    col = lax.broadcasted_iota(jnp.int32, (BLOCK, 2 * BLOCK), 1)
    ki = col & (BLOCK - 1)
    in_own_half = (col >> (BLOCK.bit_length() - 1)) == own_half
    valid = jnp.logical_or(
        jnp.logical_and(in_own_half, ki <= qi),
        jnp.logical_and(jnp.logical_not(in_own_half), jnp.logical_and(ki >= qi, nb > 0)))
    lane = lax.broadcasted_iota(jnp.int32, (BLOCK, V7X_LANES), 1)
    ones = jnp.ones((2 * BLOCK, HEAD_DIM), BF16)
    lse_all = jnp.zeros((BLOCK, V7X_LANES), F32)
    nt = (((1,), (1,)), ((), ()))
    for h in range(N_GROUP_HEADS):
        cols = slice(h * HEAD_DIM, (h + 1) * HEAD_DIM)
        s = lax.dot_general(q_ref[:, cols], k2_ref[:, cols], nt, preferred_element_type=F32)
        s = jnp.where(valid, s, NEG_INF)
        m = jnp.max(s, axis=-1, keepdims=True)
        p = jnp.exp(s - m).astype(BF16)
        ov = _bdot(p, jnp.concatenate([v2_ref[:, cols], ones], axis=1))
        den = ov[:, HEAD_DIM:]
        o_ref[:, cols] = (ov[:, :HEAD_DIM] / den).astype(o_ref.dtype)
        lse_all = jnp.where(lane == h, m + jnp.log(den), lse_all)
    lse_ref[...] = lse_all


def _window_attn(h_group, group):
    s = h_group.shape[0]
    _, d = DILATED_GROUPS[group]
    blocks_per_class = s // d // BLOCK
    spec = lambda sec: pl.BlockSpec(
        (BLOCK, GROUP_WIDTH), lambda r, nb: (r * blocks_per_class + nb, sec))
    return pl.pallas_call(
        _window_attn_kernel,
        grid=(d, blocks_per_class),
        in_specs=[spec(0), spec(1), spec(2)],
        out_specs=[
            pl.BlockSpec((BLOCK, GROUP_WIDTH), lambda r, nb: (r * blocks_per_class + nb, 0)),
            pl.BlockSpec((BLOCK, V7X_LANES), lambda r, nb: (r * blocks_per_class + nb, 0)),
        ],
        out_shape=[
            jax.ShapeDtypeStruct((s, GROUP_WIDTH), BF16),
            jax.ShapeDtypeStruct((s, V7X_LANES), F32),
        ],
        scratch_shapes=[pltpu.VMEM((2 * BLOCK, GROUP_WIDTH), BF16)] * 2,
        compiler_params=_params(2),
        name=f"window_attn_g{group}",
    )(h_group, h_group, h_group)


def _merge_kernel(o0_ref, o1_ref, o2_ref, l0_ref, l1_ref, l2_ref, out_ref):
    outs = [o0_ref[...].astype(F32)]
    lses = [l0_ref[...]]
    for o_ref, l_ref in ((o1_ref, l1_ref), (o2_ref, l2_ref)):
        d = o_ref.shape[0]
        perm = _row_permutation(NORM_ROWS, d, False)
        o_cm = jnp.concatenate([o_ref[r] for r in range(d)], axis=0)
        l_cm = jnp.concatenate([l_ref[r] for r in range(d)], axis=0)
        outs.append(_bdot(perm, o_cm))
        hi = l_cm.astype(BF16)
        rest = l_cm - hi.astype(F32)
        mid = rest.astype(BF16)
        lo = (rest - mid.astype(F32)).astype(BF16)
        lses.append(_bdot(perm, hi) + _bdot(perm, mid) + _bdot(perm, lo))
    l0, l1, l2 = lses
    m = jnp.maximum(jnp.maximum(l0, l1), l2)
    e0, e1, e2 = jnp.exp(l0 - m), jnp.exp(l1 - m), jnp.exp(l2 - m)
    tot = e0 + e1 + e2
    a0, a1, a2 = e0 / tot, e1 / tot, e2 / tot
    for h in range(N_GROUP_HEADS):
        cols = slice(h * HEAD_DIM, (h + 1) * HEAD_DIM)
        merged = (a0[:, h:h + 1] * outs[0][:, cols] + a1[:, h:h + 1] * outs[1][:, cols]
                  + a2[:, h:h + 1] * outs[2][:, cols])
        out_ref[:, cols] = merged.astype(out_ref.dtype)


def _merge_groups(outs, lses):
    s = outs[0].shape[0]
    rows = NORM_ROWS

    def specs(width):
        out = [pl.BlockSpec((rows, width), lambda i: (i, 0))]
        for _, d in DILATED_GROUPS[1:]:
            out.append(pl.BlockSpec((d, rows // d, width), lambda i: (0, i, 0)))
        return out

    class_major = lambda a, d: a.reshape(d, s // d, a.shape[1])
    outs = [outs[0]] + [class_major(o, d) for o, (_, d) in zip(outs[1:], DILATED_GROUPS[1:])]
    lses = [lses[0]] + [class_major(l, d) for l, (_, d) in zip(lses[1:], DILATED_GROUPS[1:])]
    return pl.pallas_call(
        _merge_kernel,
        grid=(s // rows,),
        in_specs=specs(GROUP_WIDTH) + specs(V7X_LANES),
        out_specs=pl.BlockSpec((rows, GROUP_WIDTH), lambda i: (i, 0)),
        out_shape=jax.ShapeDtypeStruct((s, GROUP_WIDTH), BF16),
        compiler_params=_params(1),
        name="merge_groups",
    )(*outs, *lses)


def kernel(x, mem, positions, g_mem, w_mem_kv, g_mix_pre, g_mix_post, g_ffn_pre, g_ffn_post,
           w_conv_in, conv_mix_w, w_conv_out, w_attn_in, w_attn_out,
           w_ffn_gate, w_ffn_up, conv_ffn_w, w_ffn_down):
    b, s, d = x.shape
    assert b == 1 and s == SEQ and d == D_MODEL
    xs = x.reshape(s, d)
    gains = lambda g: g.reshape(DEPTH, 1, d)
    g_mix_pre, g_mix_post = gains(g_mix_pre), gains(g_mix_post)
    g_ffn_pre, g_ffn_post = gains(g_ffn_pre), gains(g_ffn_post)

    kv = _mem_kv(mem.reshape(MEM_TOKENS, d), g_mem, w_mem_kv)
    cos_tab, sin_tab = _rope_tables(positions)

    dilations = tuple(dil for _, dil in DILATED_GROUPS[1:])
    xn = _rmsnorm_bf16(xs, g_mix_pre, 0)
    xn_class_major = ()
    for i in range(DEPTH):
        j = i // 2
        if i % 2 == 0:
            y_mix = _conv_in(xn, w_conv_in, conv_mix_w, j)
            y_mem = _mem_attend(xn, w_conv_in, j, 3 * CONV_WIDTH // MEM_HEAD_DIM, kv)
            y = _out_proj(y_mix, y_mem, w_conv_out, j)
        else:
            lhs = [xn] + [xp.reshape(s, d) for xp in xn_class_major]
            parts = [
                _window_attn(_attn_in(lhs[g], w_attn_in, cos_tab, sin_tab, j, g), g)
                for g in range(N_GROUPS)]
            y_mix = _merge_groups([p[0] for p in parts], [p[1] for p in parts])
            y_mem = _mem_attend(xn, w_attn_in, j, N_SELF // MEM_HEAD_DIM, kv)
            y = _out_proj(y_mix, y_mem, w_attn_out, j)
        xs, xn = _residual_norm(y, xs, g_mix_post, i, g_ffn_pre, i)
        a = _ffn_in(xn, w_ffn_gate, w_ffn_up, conv_ffn_w, i)
        y = _ffn_out(a, w_ffn_down, i)
        if i + 1 == DEPTH:
            xs = _residual_norm(y, xs, g_ffn_post, i)
        elif (i + 1) % 2 == 1:
            xs, xn, *xn_class_major = _residual_norm(
                y, xs, g_ffn_post, i, g_mix_pre, i + 1, dilations=dilations)
        else:
            xs, xn = _residual_norm(y, xs, g_ffn_post, i, g_mix_pre, i + 1)
    return xs.reshape(b, s, d)
```

```python
import functools

import jax
import jax.numpy as jnp
from jax import lax
from jax.experimental import pallas as pl
from jax.experimental.pallas import tpu as pltpu

D_MODEL = 4096
SEQ = 8192
DEPTH = 2
HEAD_DIM = 128
N_GROUP_HEADS = 16
GROUP_WIDTH = N_GROUP_HEADS * HEAD_DIM
DILATED_GROUPS = ((128, 1), (512, 4), (2048, 16))
N_GROUPS = len(DILATED_GROUPS)
N_SELF = N_GROUPS * 3 * GROUP_WIDTH
BLOCK = 128
CONV_WIDTH = 3 * D_MODEL // 4
MEM_TOKENS = 256
MEM_HEADS = 4
MEM_HEAD_DIM = 256
MEM_WIDTH = MEM_HEADS * MEM_HEAD_DIM
D_FF = 11008
ROPE_THETA = 10000.0
EPS = 1e-6
NEG_INF = -1e30

V7X_SUBLANES = 8
V7X_LANES = 128
V7X_MXU_COLS = 256
V7X_VMEM_LIMIT_BYTES = 60 * 1024 * 1024

BF16 = jnp.bfloat16
F32 = jnp.float32

ROW_BLOCK = 1024
BIG_ROW_BLOCK = 2048
CHUNK_ROWS = 512
NORM_ROWS = 256
HALO = V7X_SUBLANES


def _params(n_axes):
    return pltpu.CompilerParams(
        dimension_semantics=("arbitrary",) * n_axes,
        vmem_limit_bytes=V7X_VMEM_LIMIT_BYTES,
    )


def _rms_scale(v):
    return lax.rsqrt(jnp.mean(v * v, axis=-1, keepdims=True) + EPS)


def _bdot(a, b):
    return jnp.dot(a, b, preferred_element_type=F32)


def _norm_kernel(x_ref, g_ref, o_ref):
    x = x_ref[...]
    o_ref[...] = (x * _rms_scale(x) * g_ref[...]).astype(o_ref.dtype)


def _rmsnorm_bf16(x, g_all, layer):
    s, d = x.shape
    return pl.pallas_call(
        _norm_kernel,
        grid=(s // NORM_ROWS,),
        in_specs=[
            pl.BlockSpec((NORM_ROWS, d), lambda i: (i, 0)),
            pl.BlockSpec((None, 1, d), lambda i: (layer, 0, 0)),
        ],
        out_specs=pl.BlockSpec((NORM_ROWS, d), lambda i: (i, 0)),
        out_shape=jax.ShapeDtypeStruct((s, d), BF16),
        compiler_params=_params(1),
        name="rmsnorm_bf16",
    )(x, g_all)


def _residual_kernel(y_ref, x_ref, gpost_ref, gnext_ref, xo_ref, xn_ref):
    y = y_ref[...].astype(F32)
    xnew = x_ref[...] + y * _rms_scale(y) * gpost_ref[...]
    xo_ref[...] = xnew
    xn_ref[...] = (xnew * _rms_scale(xnew) * gnext_ref[...]).astype(xn_ref.dtype)


def _row_permutation(rows, d, to_class_major):
    n = rows // d
    out_row = lax.broadcasted_iota(jnp.int32, (rows, rows), 0)
    in_row = lax.broadcasted_iota(jnp.int32, (rows, rows), 1)
    if to_class_major:
        src = (out_row & (n - 1)) * d + (out_row >> (n.bit_length() - 1))
    else:
        src = (out_row & (d - 1)) * n + (out_row >> (d.bit_length() - 1))
    return (in_row == src).astype(BF16)


def _residual_dilated_kernel(y_ref, x_ref, gpost_ref, gnext_ref, xo_ref, xn_ref, *xp_refs):
    y = y_ref[...].astype(F32)
    xnew = x_ref[...] + y * _rms_scale(y) * gpost_ref[...]
    xo_ref[...] = xnew
    xn = (xnew * _rms_scale(xnew) * gnext_ref[...]).astype(BF16)
    xn_ref[...] = xn
    for xp_ref in xp_refs:
        d, n = xp_ref.shape[0], xp_ref.shape[1]
        xp = _bdot(_row_permutation(NORM_ROWS, d, True), xn).astype(BF16)
        for r in range(d):
            xp_ref[r] = xp[r * n:(r + 1) * n, :]


def _residual_last_kernel(y_ref, x_ref, gpost_ref, xo_ref):
    y = y_ref[...].astype(F32)
    xo_ref[...] = x_ref[...] + y * _rms_scale(y) * gpost_ref[...]


def _residual_norm(y, x, g_post, post_layer, g_next=None, next_layer=None, dilations=()):
    s, d = x.shape
    row = pl.BlockSpec((NORM_ROWS, d), lambda i: (i, 0))
    gspec = lambda layer: pl.BlockSpec((None, 1, d), lambda i: (layer, 0, 0))
    if dilations:
        return pl.pallas_call(
            _residual_dilated_kernel,
            grid=(s // NORM_ROWS,),
            in_specs=[row, row, gspec(post_layer), gspec(next_layer)],
            out_specs=[row, row] + [
                pl.BlockSpec((dil, NORM_ROWS // dil, d), lambda i: (0, i, 0)) for dil in dilations],
            out_shape=[jax.ShapeDtypeStruct((s, d), F32), jax.ShapeDtypeStruct((s, d), BF16)] + [
                jax.ShapeDtypeStruct((dil, s // dil, d), BF16) for dil in dilations],
            compiler_params=_params(1),
            name="residual_norm_dilated",
        )(y, x, g_post, g_next)
    if g_next is None:
        return pl.pallas_call(
            _residual_last_kernel,
            grid=(s // NORM_ROWS,),
            in_specs=[row, row, gspec(post_layer)],
            out_specs=row,
            out_shape=jax.ShapeDtypeStruct((s, d), F32),
            compiler_params=_params(1),
            name="residual_last",
        )(y, x, g_post)
    return pl.pallas_call(
        _residual_kernel,
        grid=(s // NORM_ROWS,),
        in_specs=[row, row, gspec(post_layer), gspec(next_layer)],
        out_specs=[row, row],
        out_shape=[jax.ShapeDtypeStruct((s, d), F32), jax.ShapeDtypeStruct((s, d), BF16)],
        compiler_params=_params(1),
        name="residual_norm",
    )(y, x, g_post, g_next)


def _rope_table_kernel(pos_ref, freq_ref, sign_ref, cos_ref, sin_ref):
    ang = pos_ref[...].astype(F32) * freq_ref[...]
    cos_ref[...] = jnp.cos(ang)
    sin_ref[...] = jnp.sin(ang) * sign_ref[...]


def _rope_tables(positions):
    s = positions.shape[-1]
    half = HEAD_DIM // 2
    inv_freq = ROPE_THETA ** (-jnp.arange(half, dtype=F32) * 2.0 / HEAD_DIM)
    freq = jnp.concatenate([inv_freq, inv_freq]).reshape(1, HEAD_DIM)
    sign = jnp.concatenate([-jnp.ones((half,), F32), jnp.ones((half,), F32)]).reshape(1, HEAD_DIM)
    pos = jnp.stack([positions.reshape(s // d, d).T.reshape(s, 1) for _, d in DILATED_GROUPS])
    rows = 1024
    tab = pl.BlockSpec((None, rows, HEAD_DIM), lambda g, i: (g, i, 0))
    const = pl.BlockSpec((1, HEAD_DIM), lambda g, i: (0, 0))
    return pl.pallas_call(
        _rope_table_kernel,
        grid=(N_GROUPS, s // rows),
        in_specs=[pl.BlockSpec((None, rows, 1), lambda g, i: (g, i, 0)), const, const],
        out_specs=[tab, tab],
        out_shape=[jax.ShapeDtypeStruct((N_GROUPS, s, HEAD_DIM), F32)] * 2,
        compiler_params=_params(2),
        name="rope_tables",
    )(pos, freq, sign)


def _mem_kv_kernel(mem_ref, g_ref, w_ref, o_ref):
    m = mem_ref[...]
    mn = (m * _rms_scale(m) * g_ref[...]).astype(BF16)
    o_ref[...] = _bdot(mn, w_ref[...].astype(BF16)).astype(o_ref.dtype)


def _mem_kv(mem, g_mem, w_mem_kv):
    m, d = mem.shape
    n = w_mem_kv.shape[1]
    bn = 512
    return pl.pallas_call(
        _mem_kv_kernel,
        grid=(n // bn,),
        in_specs=[
            pl.BlockSpec((m, d), lambda j: (0, 0)),
            pl.BlockSpec((1, d), lambda j: (0, 0)),
            pl.BlockSpec((d, bn), lambda j: (0, j)),
        ],
        out_specs=pl.BlockSpec((m, bn), lambda j: (0, j)),
        out_shape=jax.ShapeDtypeStruct((m, n), BF16),
        compiler_params=_params(1),
        name="mem_kv",
    )(mem, g_mem.reshape(1, d), w_mem_kv)


def _causal_conv3(v, prev_tail, cw_ref, ext_ref):
    bm = v.shape[0]
    ext_ref[0:HALO, :] = prev_tail
    ext_ref[HALO:HALO + bm, :] = v
    v1 = ext_ref[HALO - 1:HALO - 1 + bm, :]
    v2 = ext_ref[HALO - 2:HALO - 2 + bm, :]
    conv = cw_ref[0:1, :] * v2 + cw_ref[1:2, :] * v1 + cw_ref[2:3, :] * v
    return conv, v[bm - HALO:bm, :]


def _resident_rows(bm, k):
    return pl.BlockSpec((bm, k), lambda i, j: (i, 0), pipeline_mode=pl.Buffered(1))


def _conv_in_kernel(x_ref, wb_ref, wc_ref, wu_ref, cw_ref, o_ref, ext0_ref, ext1_ref, tails_ref):
    i, j = pl.program_id(0), pl.program_id(1)
    tail = jnp.where(i == 0, 0.0, tails_ref[j])
    for chunk in range(x_ref.shape[0] // CHUNK_ROWS):
        ext_ref = (ext0_ref, ext1_ref)[chunk % 2]
        rows = slice(chunk * CHUNK_ROWS, (chunk + 1) * CHUNK_ROWS)
        x = x_ref[rows, :]
        b = _bdot(x, wb_ref[...].astype(BF16))
        c = _bdot(x, wc_ref[...].astype(BF16))
        u = _bdot(x, wu_ref[...].astype(BF16))
        conv, tail = _causal_conv3(c * u, tail, cw_ref, ext_ref)
        o_ref[rows, :] = (b * conv).astype(o_ref.dtype)
    tails_ref[j] = tail


def _conv_in(xn, w_in, conv_w, layer):
    s, d = xn.shape
    bm, bn = BIG_ROW_BLOCK, V7X_MXU_COLS
    nj = CONV_WIDTH // bn
    wspec = lambda off: pl.BlockSpec((None, d, bn), lambda i, j: (layer, 0, j + off))
    ext = pltpu.VMEM((CHUNK_ROWS + HALO, bn), F32)
    return pl.pallas_call(
        _conv_in_kernel,
        grid=(s // bm, nj),
        in_specs=[
            _resident_rows(bm, d),
            wspec(0), wspec(nj), wspec(2 * nj),
            pl.BlockSpec((None, 3, bn), lambda i, j: (layer, 0, j)),
        ],
        out_specs=pl.BlockSpec((bm, bn), lambda i, j: (i, j)),
        out_shape=jax.ShapeDtypeStruct((s, CONV_WIDTH), BF16),
        scratch_shapes=[ext, ext, pltpu.VMEM((nj, HALO, bn), F32)],
        compiler_params=_params(2),
        name="conv_in",
    )(xn, w_in, w_in, w_in, conv_w)


def _mem_q_kernel(x_ref, wq_ref, k_ref, v_ref, o_ref):
    q = _bdot(x_ref[...], wq_ref[...].astype(BF16)) * (MEM_HEAD_DIM ** -0.5)
    s = lax.dot_general(q.astype(BF16), k_ref[...], (((1,), (1,)), ((), ())),
                        preferred_element_type=F32)
    m = jnp.max(s, axis=-1, keepdims=True)
    p = jnp.exp(s - m)
    den = jnp.sum(p, axis=-1, keepdims=True)
    o = _bdot((p / den).astype(BF16), v_ref[...])
    o_ref[...] = o.astype(o_ref.dtype)


def _mem_attend(xn, w_in, layer, q_col_block, kv):
    s, d = xn.shape
    bm, bn = ROW_BLOCK, MEM_HEAD_DIM
    return pl.pallas_call(
        _mem_q_kernel,
        grid=(s // bm, MEM_HEADS),
        in_specs=[
            pl.BlockSpec((bm, d), lambda i, h: (i, 0)),
            pl.BlockSpec((None, d, bn), lambda i, h: (layer, 0, q_col_block + h)),
            pl.BlockSpec((MEM_TOKENS, bn), lambda i, h: (0, h)),
            pl.BlockSpec((MEM_TOKENS, bn), lambda i, h: (0, MEM_HEADS + h)),
        ],
        out_specs=pl.BlockSpec((bm, bn), lambda i, h: (i, h)),
        out_shape=jax.ShapeDtypeStruct((s, MEM_WIDTH), BF16),
        compiler_params=_params(2),
        name="mem_attend",
    )(xn, w_in, kv, kv)


def _out_proj_kernel(a_ref, b_ref, wa_ref, wb_ref, o_ref):
    for chunk in range(a_ref.shape[0] // CHUNK_ROWS):
        rows = slice(chunk * CHUNK_ROWS, (chunk + 1) * CHUNK_ROWS)
        acc = _bdot(a_ref[rows, :], wa_ref[...].astype(BF16))
        acc = acc + _bdot(b_ref[rows, :], wb_ref[...].astype(BF16))
        o_ref[rows, :] = acc.astype(o_ref.dtype)


def _out_proj(a, b, w, layer):
    s, ka = a.shape
    kb = b.shape[1]
    n = w.shape[2]
    bm, bn = BIG_ROW_BLOCK, 512
    assert ka % kb == 0
    return pl.pallas_call(
        _out_proj_kernel,
        grid=(s // bm, n // bn),
        in_specs=[
            _resident_rows(bm, ka), _resident_rows(bm, kb),
            pl.BlockSpec((None, ka, bn), lambda i, j: (layer, 0, j)),
            pl.BlockSpec((None, kb, bn), lambda i, j: (layer, ka // kb, j)),
        ],
        out_specs=pl.BlockSpec((bm, bn), lambda i, j: (i, j)),
        out_shape=jax.ShapeDtypeStruct((s, n), BF16),
        compiler_params=_params(2),
        name="out_proj",
    )(a, b, w, w)


def _ffn_in_kernel(x_ref, wg_ref, wu_ref, cw_ref, o_ref, ext0_ref, ext1_ref, tails_ref):
    i, j = pl.program_id(0), pl.program_id(1)
    tail = jnp.where(i == 0, 0.0, tails_ref[j])
    for chunk in range(x_ref.shape[0] // CHUNK_ROWS):
        ext_ref = (ext0_ref, ext1_ref)[chunk % 2]
        rows = slice(chunk * CHUNK_ROWS, (chunk + 1) * CHUNK_ROWS)
        x = x_ref[rows, :]
        g = _bdot(x, wg_ref[...].astype(BF16))
        u = _bdot(x, wu_ref[...].astype(BF16))
        gc, tail = _causal_conv3(g, tail, cw_ref, ext_ref)
        act = gc / (1.0 + jnp.exp(-gc))
        o_ref[rows, :] = (act * u).astype(o_ref.dtype)
    tails_ref[j] = tail


def _ffn_in(xn, w_gate, w_up, conv_w, layer):
    s, d = xn.shape
    bm, bn = BIG_ROW_BLOCK, V7X_MXU_COLS
    nj = D_FF // bn
    wspec = pl.BlockSpec((None, d, bn), lambda i, j: (layer, 0, j))
    ext = pltpu.VMEM((CHUNK_ROWS + HALO, bn), F32)
    return pl.pallas_call(
        _ffn_in_kernel,
        grid=(s // bm, nj),
        in_specs=[
            _resident_rows(bm, d),
            wspec, wspec,
            pl.BlockSpec((None, 3, bn), lambda i, j: (layer, 0, j)),
        ],
        out_specs=pl.BlockSpec((bm, bn), lambda i, j: (i, j)),
        out_shape=jax.ShapeDtypeStruct((s, D_FF), BF16),
        scratch_shapes=[ext, ext, pltpu.VMEM((nj, HALO, bn), F32)],
        compiler_params=_params(2),
        name="ffn_in",
    )(xn, w_gate, w_up, conv_w)


def _ffn_out_kernel(a_ref, w_ref, o_ref, acc_ref):
    k = pl.program_id(1)

    @pl.when(k == 0)
    def _():
        acc_ref[...] = jnp.zeros_like(acc_ref)

    a = a_ref[...]
    for c in range(0, o_ref.shape[1], FFN_OUT_COL_CHUNK):
        cols = slice(c, c + FFN_OUT_COL_CHUNK)
        acc_ref[:, cols] += _bdot(a, w_ref[:, cols].astype(BF16))

    @pl.when(k == pl.num_programs(1) - 1)
    def _():
        o_ref[...] = acc_ref[...].astype(o_ref.dtype)


FFN_OUT_COL_CHUNK = 512


def _ffn_out(a, w_down, layer):
    s, kdim = a.shape
    n = w_down.shape[2]
    bm, bk = ROW_BLOCK, V7X_MXU_COLS
    return pl.pallas_call(
        _ffn_out_kernel,
        grid=(s // bm, kdim // bk),
        in_specs=[
            pl.BlockSpec((bm, bk), lambda i, k: (i, k)),
            pl.BlockSpec((None, bk, n), lambda i, k: (layer, k, 0)),
        ],
        out_specs=pl.BlockSpec((bm, n), lambda i, k: (i, 0)),
        out_shape=jax.ShapeDtypeStruct((s, n), BF16),
        scratch_shapes=[pltpu.VMEM((bm, n), F32)],
        compiler_params=_params(2),
        name="ffn_out",
    )(a, w_down)


ATTN_IN_COLS = 512
HEADS_PER_ATTN_IN_BLOCK = ATTN_IN_COLS // HEAD_DIM
ATTN_IN_BLOCKS_PER_SECTION = GROUP_WIDTH // ATTN_IN_COLS
ATTN_IN_BLOCKS_PER_GROUP = 3 * ATTN_IN_BLOCKS_PER_SECTION
ATTN_IN_CHUNK_ROWS = 512


def _attn_in_kernel(x_ref, w_ref, cos_ref, sin_ref, o_ref):
    section = pl.program_id(1) // ATTN_IN_BLOCKS_PER_SECTION
    scale = jnp.where(section == 0, HEAD_DIM ** -0.5, 1.0).astype(F32)
    is_v = section == 2
    for chunk in range(x_ref.shape[0] // ATTN_IN_CHUNK_ROWS):
        rows = slice(chunk * ATTN_IN_CHUNK_ROWS, (chunk + 1) * ATTN_IN_CHUNK_ROWS)
        acc = _bdot(x_ref[rows, :], w_ref[...].astype(BF16))
        cos = cos_ref[rows, :] * scale
        sin = sin_ref[rows, :] * scale
        for h in range(HEADS_PER_ATTN_IN_BLOCK):
            cols = slice(h * HEAD_DIM, (h + 1) * HEAD_DIM)
            v = acc[:, cols]
            rot = pltpu.roll(v, HEAD_DIM // 2, axis=1)
            o_ref[rows, cols] = jnp.where(is_v, v, v * cos + rot * sin).astype(o_ref.dtype)


def _attn_in(xn, w_in, cos_tab, sin_tab, layer, group):
    s, d = xn.shape
    bm, bn = BIG_ROW_BLOCK, ATTN_IN_COLS
    nj = ATTN_IN_BLOCKS_PER_GROUP
    tab = pl.BlockSpec((None, bm, HEAD_DIM), lambda i, j: (group, i, 0))
    return pl.pallas_call(
        _attn_in_kernel,
        grid=(s // bm, nj),
        in_specs=[
            _resident_rows(bm, d),
            pl.BlockSpec((None, d, bn), lambda i, j: (layer, 0, group * nj + j)),
            tab, tab,
        ],
        out_specs=pl.BlockSpec((bm, bn), lambda i, j: (i, j)),
        out_shape=jax.ShapeDtypeStruct((s, 3 * GROUP_WIDTH), BF16),
        compiler_params=_params(2),
        name=f"attn_in_g{group}",
    )(xn, w_in, cos_tab, sin_tab)


def _window_attn_kernel(q_ref, k_ref, v_ref, o_ref, lse_ref, k2_ref, v2_ref):
    nb = pl.program_id(1)

    @pl.when(nb == 0)
    def _():
        k2_ref[...] = jnp.zeros_like(k2_ref)
        v2_ref[...] = jnp.zeros_like(v2_ref)

    own_half = nb % 2
    own_rows = pl.ds(pl.multiple_of(own_half * BLOCK, BLOCK), BLOCK)
    k2_ref[own_rows, :] = k_ref[...]
    v2_ref[own_rows, :] = v_ref[...]

    qi = lax.broadcasted_iota(jnp.int32, (BLOCK, 2 * BLOCK), 0)
    col = lax.broadcasted_iota(jnp.int32, (BLOCK, 2 * BLOCK), 1)
    ki = col & (BLOCK - 1)
    in_own_half = (col >> (BLOCK.bit_length() - 1)) == own_half
    valid = jnp.logical_or(
        jnp.logical_and(in_own_half, ki <= qi),
        jnp.logical_and(jnp.logical_not(in_own_half), jnp.logical_and(ki >= qi, nb > 0)))
    lane = lax.broadcasted_iota(jnp.int32, (BLOCK, V7X_LANES), 1)
    ones = jnp.ones((2 * BLOCK, HEAD_DIM), BF16)
    lse_all = jnp.zeros((BLOCK, V7X_LANES), F32)
    nt = (((1,), (1,)), ((), ()))
    for h in range(N_GROUP_HEADS):
        cols = slice(h * HEAD_DIM, (h + 1) * HEAD_DIM)
        s = lax.dot_general(q_ref[:, cols], k2_ref[:, cols], nt, preferred_element_type=F32)
        s = jnp.where(valid, s, NEG_INF)
        m = jnp.max(s, axis=-1, keepdims=True)
        p = jnp.exp(s - m).astype(BF16)
        ov = _bdot(p, jnp.concatenate([v2_ref[:, cols], ones], axis=1))
        den = ov[:, HEAD_DIM:]
        o_ref[:, cols] = (ov[:, :HEAD_DIM] / den).astype(o_ref.dtype)
        lse_all = jnp.where(lane == h, m + jnp.log(den), lse_all)
    lse_ref[...] = lse_all


def _window_attn(h_group, group):
    s = h_group.shape[0]
    _, d = DILATED_GROUPS[group]
    blocks_per_class = s // d // BLOCK
    spec = lambda sec: pl.BlockSpec(
        (BLOCK, GROUP_WIDTH), lambda r, nb: (r * blocks_per_class + nb, sec))
    return pl.pallas_call(
        _window_attn_kernel,
        grid=(d, blocks_per_class),
        in_specs=[spec(0), spec(1), spec(2)],
        out_specs=[
            pl.BlockSpec((BLOCK, GROUP_WIDTH), lambda r, nb: (r * blocks_per_class + nb, 0)),
            pl.BlockSpec((BLOCK, V7X_LANES), lambda r, nb: (r * blocks_per_class + nb, 0)),
        ],
        out_shape=[
            jax.ShapeDtypeStruct((s, GROUP_WIDTH), BF16),
            jax.ShapeDtypeStruct((s, V7X_LANES), F32),
        ],
        scratch_shapes=[pltpu.VMEM((2 * BLOCK, GROUP_WIDTH), BF16)] * 2,
        compiler_params=_params(2),
        name=f"window_attn_g{group}",
    )(h_group, h_group, h_group)


def _merge_kernel(o0_ref, o1_ref, o2_ref, l0_ref, l1_ref, l2_ref, out_ref):
    outs = [o0_ref[...].astype(F32)]
    lses = [l0_ref[...]]
    for o_ref, l_ref in ((o1_ref, l1_ref), (o2_ref, l2_ref)):
        d = o_ref.shape[0]
        perm = _row_permutation(NORM_ROWS, d, False)
        o_cm = jnp.concatenate([o_ref[r] for r in range(d)], axis=0)
        l_cm = jnp.concatenate([l_ref[r] for r in range(d)], axis=0)
        outs.append(_bdot(perm, o_cm))
        hi = l_cm.astype(BF16)
        rest = l_cm - hi.astype(F32)
        mid = rest.astype(BF16)
        lo = (rest - mid.astype(F32)).astype(BF16)
        lses.append(_bdot(perm, hi) + _bdot(perm, mid) + _bdot(perm, lo))
    l0, l1, l2 = lses
    m = jnp.maximum(jnp.maximum(l0, l1), l2)
    e0, e1, e2 = jnp.exp(l0 - m), jnp.exp(l1 - m), jnp.exp(l2 - m)
    tot = e0 + e1 + e2
    a0, a1, a2 = e0 / tot, e1 / tot, e2 / tot
    for h in range(N_GROUP_HEADS):
        cols = slice(h * HEAD_DIM, (h + 1) * HEAD_DIM)
        merged = (a0[:, h:h + 1] * outs[0][:, cols] + a1[:, h:h + 1] * outs[1][:, cols]
                  + a2[:, h:h + 1] * outs[2][:, cols])
        out_ref[:, cols] = merged.astype(out_ref.dtype)


def _merge_groups(outs, lses):
    s = outs[0].shape[0]
    rows = NORM_ROWS

    def specs(width):
        out = [pl.BlockSpec((rows, width), lambda i: (i, 0))]
        for _, d in DILATED_GROUPS[1:]:
            out.append(pl.BlockSpec((d, rows // d, width), lambda i: (0, i, 0)))
        return out

    class_major = lambda a, d: a.reshape(d, s // d, a.shape[1])
    outs = [outs[0]] + [class_major(o, d) for o, (_, d) in zip(outs[1:], DILATED_GROUPS[1:])]
    lses = [lses[0]] + [class_major(l, d) for l, (_, d) in zip(lses[1:], DILATED_GROUPS[1:])]
    return pl.pallas_call(
        _merge_kernel,
        grid=(s // rows,),
        in_specs=specs(GROUP_WIDTH) + specs(V7X_LANES),
        out_specs=pl.BlockSpec((rows, GROUP_WIDTH), lambda i: (i, 0)),
        out_shape=jax.ShapeDtypeStruct((s, GROUP_WIDTH), BF16),
        compiler_params=_params(1),
        name="merge_groups",
    )(*outs, *lses)


def kernel(x, mem, positions, g_mem, w_mem_kv, g_mix_pre, g_mix_post, g_ffn_pre, g_ffn_post,
           w_conv_in, conv_mix_w, w_conv_out, w_attn_in, w_attn_out,
           w_ffn_gate, w_ffn_up, conv_ffn_w, w_ffn_down):
    b, s, d = x.shape
    assert b == 1 and s == SEQ and d == D_MODEL
    xs = x.reshape(s, d)
    gains = lambda g: g.reshape(DEPTH, 1, d)
    g_mix_pre, g_mix_post = gains(g_mix_pre), gains(g_mix_post)
    g_ffn_pre, g_ffn_post = gains(g_ffn_pre), gains(g_ffn_post)

    kv = _mem_kv(mem.reshape(MEM_TOKENS, d), g_mem, w_mem_kv)
    cos_tab, sin_tab = _rope_tables(positions)

    dilations = tuple(dil for _, dil in DILATED_GROUPS[1:])
    xn = _rmsnorm_bf16(xs, g_mix_pre, 0)
    xn_class_major = ()
    for i in range(DEPTH):
        j = i // 2
        if i % 2 == 0:
            y_mix = _conv_in(xn, w_conv_in, conv_mix_w, j)
            y_mem = _mem_attend(xn, w_conv_in, j, 3 * CONV_WIDTH // MEM_HEAD_DIM, kv)
            y = _out_proj(y_mix, y_mem, w_conv_out, j)
        else:
            lhs = [xn] + [xp.reshape(s, d) for xp in xn_class_major]
            parts = [
                _window_attn(_attn_in(lhs[g], w_attn_in, cos_tab, sin_tab, j, g), g)
                for g in range(N_GROUPS)]
            y_mix = _merge_groups([p[0] for p in parts], [p[1] for p in parts])
            y_mem = _mem_attend(xn, w_attn_in, j, N_SELF // MEM_HEAD_DIM, kv)
            y = _out_proj(y_mix, y_mem, w_attn_out, j)
        xs, xn = _residual_norm(y, xs, g_mix_post, i, g_ffn_pre, i)
        a = _ffn_in(xn, w_ffn_gate, w_ffn_up, conv_ffn_w, i)
        y = _ffn_out(a, w_ffn_down, i)
        if i + 1 == DEPTH:
            xs = _residual_norm(y, xs, g_ffn_post, i)
        elif (i + 1) % 2 == 1:
            xs, xn, *xn_class_major = _residual_norm(
                y, xs, g_ffn_post, i, g_mix_pre, i + 1, dilations=dilations)
        else:
            xs, xn = _residual_norm(y, xs, g_ffn_post, i, g_mix_pre, i + 1)
    return xs.reshape(b, s, d)
```

```python
import functools

import jax
import jax.numpy as jnp
from jax import lax
from jax.experimental import pallas as pl
from jax.experimental.pallas import tpu as pltpu

D_MODEL = 4096
SEQ = 8192
DEPTH = 2
HEAD_DIM = 128
N_GROUP_HEADS = 16
GROUP_WIDTH = N_GROUP_HEADS * HEAD_DIM
DILATED_GROUPS = ((128, 1), (512, 4), (2048, 16))
N_GROUPS = len(DILATED_GROUPS)
N_SELF = N_GROUPS * 3 * GROUP_WIDTH
BLOCK = 128
CONV_WIDTH = 3 * D_MODEL // 4
MEM_TOKENS = 256
MEM_HEADS = 4
MEM_HEAD_DIM = 256
MEM_WIDTH = MEM_HEADS * MEM_HEAD_DIM
D_FF = 11008
ROPE_THETA = 10000.0
EPS = 1e-6
NEG_INF = -1e30

V7X_SUBLANES = 8
V7X_LANES = 128
V7X_MXU_COLS = 256
V7X_VMEM_LIMIT_BYTES = 60 * 1024 * 1024

BF16 = jnp.bfloat16
F32 = jnp.float32

ROW_BLOCK = 1024
BIG_ROW_BLOCK = 2048
CHUNK_ROWS = 512
NORM_ROWS = 256
HALO = V7X_SUBLANES


def _params(n_axes):
    return pltpu.CompilerParams(
        dimension_semantics=("arbitrary",) * n_axes,
        vmem_limit_bytes=V7X_VMEM_LIMIT_BYTES,
    )


def _rms_scale(v):
    return lax.rsqrt(jnp.mean(v * v, axis=-1, keepdims=True) + EPS)


def _bdot(a, b):
    return jnp.dot(a, b, preferred_element_type=F32)


def _norm_kernel(x_ref, g_ref, o_ref):
    x = x_ref[...]
    o_ref[...] = (x * _rms_scale(x) * g_ref[...]).astype(o_ref.dtype)


def _rmsnorm_bf16(x, g_all, layer):
    s, d = x.shape
    return pl.pallas_call(
        _norm_kernel,
        grid=(s // NORM_ROWS,),
        in_specs=[
            pl.BlockSpec((NORM_ROWS, d), lambda i: (i, 0)),
            pl.BlockSpec((None, 1, d), lambda i: (layer, 0, 0)),
        ],
        out_specs=pl.BlockSpec((NORM_ROWS, d), lambda i: (i, 0)),
        out_shape=jax.ShapeDtypeStruct((s, d), BF16),
        compiler_params=_params(1),
        name="rmsnorm_bf16",
    )(x, g_all)


def _residual_kernel(y_ref, x_ref, gpost_ref, gnext_ref, xo_ref, xn_ref):
    y = y_ref[...].astype(F32)
    xnew = x_ref[...] + y * _rms_scale(y) * gpost_ref[...]
    xo_ref[...] = xnew
    xn_ref[...] = (xnew * _rms_scale(xnew) * gnext_ref[...]).astype(xn_ref.dtype)


def _row_permutation(rows, d, to_class_major):
    n = rows // d
    out_row = lax.broadcasted_iota(jnp.int32, (rows, rows), 0)
    in_row = lax.broadcasted_iota(jnp.int32, (rows, rows), 1)
    if to_class_major:
        src = (out_row & (n - 1)) * d + (out_row >> (n.bit_length() - 1))
    else:
        src = (out_row & (d - 1)) * n + (out_row >> (d.bit_length() - 1))
    return (in_row == src).astype(BF16)


def _residual_dilated_kernel(y_ref, x_ref, gpost_ref, gnext_ref, xo_ref, xn_ref, *xp_refs):
    y = y_ref[...].astype(F32)
    xnew = x_ref[...] + y * _rms_scale(y) * gpost_ref[...]
    xo_ref[...] = xnew
    xn = (xnew * _rms_scale(xnew) * gnext_ref[...]).astype(BF16)
    xn_ref[...] = xn
    for xp_ref in xp_refs:
        d, n = xp_ref.shape[0], xp_ref.shape[1]
        xp = _bdot(_row_permutation(NORM_ROWS, d, True), xn).astype(BF16)
        for r in range(d):
            xp_ref[r] = xp[r * n:(r + 1) * n, :]


def _residual_last_kernel(y_ref, x_ref, gpost_ref, xo_ref):
    y = y_ref[...].astype(F32)
    xo_ref[...] = x_ref[...] + y * _rms_scale(y) * gpost_ref[...]


def _residual_norm(y, x, g_post, post_layer, g_next=None, next_layer=None, dilations=()):
    s, d = x.shape
    row = pl.BlockSpec((NORM_ROWS, d), lambda i: (i, 0))
    gspec = lambda layer: pl.BlockSpec((None, 1, d), lambda i: (layer, 0, 0))
    if dilations:
        return pl.pallas_call(
            _residual_dilated_kernel,
            grid=(s // NORM_ROWS,),
            in_specs=[row, row, gspec(post_layer), gspec(next_layer)],
            out_specs=[row, row] + [
                pl.BlockSpec((dil, NORM_ROWS // dil, d), lambda i: (0, i, 0)) for dil in dilations],
            out_shape=[jax.ShapeDtypeStruct((s, d), F32), jax.ShapeDtypeStruct((s, d), BF16)] + [
                jax.ShapeDtypeStruct((dil, s // dil, d), BF16) for dil in dilations],
            compiler_params=_params(1),
            name="residual_norm_dilated",
        )(y, x, g_post, g_next)
    if g_next is None:
        return pl.pallas_call(
            _residual_last_kernel,
            grid=(s // NORM_ROWS,),
            in_specs=[row, row, gspec(post_layer)],
            out_specs=row,
            out_shape=jax.ShapeDtypeStruct((s, d), F32),
            compiler_params=_params(1),
            name="residual_last",
        )(y, x, g_post)
    return pl.pallas_call(
        _residual_kernel,
        grid=(s // NORM_ROWS,),
        in_specs=[row, row, gspec(post_layer), gspec(next_layer)],
        out_specs=[row, row],
        out_shape=[jax.ShapeDtypeStruct((s, d), F32), jax.ShapeDtypeStruct((s, d), BF16)],
        compiler_params=_params(1),
        name="residual_norm",
    )(y, x, g_post, g_next)


def _rope_table_kernel(pos_ref, freq_ref, sign_ref, cos_ref, sin_ref):
    ang = pos_ref[...].astype(F32) * freq_ref[...]
    cos_ref[...] = jnp.cos(ang)
    sin_ref[...] = jnp.sin(ang) * sign_ref[...]


def _rope_tables(positions):
    s = positions.shape[-1]
    half = HEAD_DIM // 2
    inv_freq = ROPE_THETA ** (-jnp.arange(half, dtype=F32) * 2.0 / HEAD_DIM)
    freq = jnp.concatenate([inv_freq, inv_freq]).reshape(1, HEAD_DIM)
    sign = jnp.concatenate([-jnp.ones((half,), F32), jnp.ones((half,), F32)]).reshape(1, HEAD_DIM)
    pos = jnp.stack([positions.reshape(s // d, d).T.reshape(s, 1) for _, d in DILATED_GROUPS])
    rows = 1024
    tab = pl.BlockSpec((None, rows, HEAD_DIM), lambda g, i: (g, i, 0))
    const = pl.BlockSpec((1, HEAD_DIM), lambda g, i: (0, 0))
    return pl.pallas_call(
        _rope_table_kernel,
        grid=(N_GROUPS, s // rows),
        in_specs=[pl.BlockSpec((None, rows, 1), lambda g, i: (g, i, 0)), const, const],
        out_specs=[tab, tab],
        out_shape=[jax.ShapeDtypeStruct((N_GROUPS, s, HEAD_DIM), F32)] * 2,
        compiler_params=_params(2),
        name="rope_tables",
    )(pos, freq, sign)


def _mem_kv_kernel(mem_ref, g_ref, w_ref, o_ref):
    m = mem_ref[...]
    mn = (m * _rms_scale(m) * g_ref[...]).astype(BF16)
    o_ref[...] = _bdot(mn, w_ref[...].astype(BF16)).astype(o_ref.dtype)


def _mem_kv(mem, g_mem, w_mem_kv):
    m, d = mem.shape
    n = w_mem_kv.shape[1]
    bn = 512
    return pl.pallas_call(
        _mem_kv_kernel,
        grid=(n // bn,),
        in_specs=[
            pl.BlockSpec((m, d), lambda j: (0, 0)),
            pl.BlockSpec((1, d), lambda j: (0, 0)),
            pl.BlockSpec((d, bn), lambda j: (0, j)),
        ],
        out_specs=pl.BlockSpec((m, bn), lambda j: (0, j)),
        out_shape=jax.ShapeDtypeStruct((m, n), BF16),
        compiler_params=_params(1),
        name="mem_kv",
    )(mem, g_mem.reshape(1, d), w_mem_kv)


def _causal_conv3(v, prev_tail, cw_ref, ext_ref):
    bm = v.shape[0]
    ext_ref[0:HALO, :] = prev_tail
    ext_ref[HALO:HALO + bm, :] = v
    v1 = ext_ref[HALO - 1:HALO - 1 + bm, :]
    v2 = ext_ref[HALO - 2:HALO - 2 + bm, :]
    conv = cw_ref[0:1, :] * v2 + cw_ref[1:2, :] * v1 + cw_ref[2:3, :] * v
    return conv, v[bm - HALO:bm, :]


def _resident_rows(bm, k):
    return pl.BlockSpec((bm, k), lambda i, j: (i, 0), pipeline_mode=pl.Buffered(1))


def _conv_in_kernel(x_ref, wb_ref, wc_ref, wu_ref, cw_ref, o_ref, ext0_ref, ext1_ref, tails_ref):
    i, j = pl.program_id(0), pl.program_id(1)
    tail = jnp.where(i == 0, 0.0, tails_ref[j])
    for chunk in range(x_ref.shape[0] // CHUNK_ROWS):
        ext_ref = (ext0_ref, ext1_ref)[chunk % 2]
        rows = slice(chunk * CHUNK_ROWS, (chunk + 1) * CHUNK_ROWS)
        x = x_ref[rows, :]
        b = _bdot(x, wb_ref[...].astype(BF16))
        c = _bdot(x, wc_ref[...].astype(BF16))
        u = _bdot(x, wu_ref[...].astype(BF16))
        conv, tail = _causal_conv3(c * u, tail, cw_ref, ext_ref)
        o_ref[rows, :] = (b * conv).astype(o_ref.dtype)
    tails_ref[j] = tail


def _conv_in(xn, w_in, conv_w, layer):
    s, d = xn.shape
    bm, bn = BIG_ROW_BLOCK, V7X_MXU_COLS
    nj = CONV_WIDTH // bn
    wspec = lambda off: pl.BlockSpec((None, d, bn), lambda i, j: (layer, 0, j + off))
    ext = pltpu.VMEM((CHUNK_ROWS + HALO, bn), F32)
    return pl.pallas_call(
        _conv_in_kernel,
        grid=(s // bm, nj),
        in_specs=[
            _resident_rows(bm, d),
            wspec(0), wspec(nj), wspec(2 * nj),
            pl.BlockSpec((None, 3, bn), lambda i, j: (layer, 0, j)),
        ],
        out_specs=pl.BlockSpec((bm, bn), lambda i, j: (i, j)),
        out_shape=jax.ShapeDtypeStruct((s, CONV_WIDTH), BF16),
        scratch_shapes=[ext, ext, pltpu.VMEM((nj, HALO, bn), F32)],
        compiler_params=_params(2),
        name="conv_in",
    )(xn, w_in, w_in, w_in, conv_w)


def _mem_q_kernel(x_ref, wq_ref, k_ref, v_ref, o_ref):
    q = _bdot(x_ref[...], wq_ref[...].astype(BF16)) * (MEM_HEAD_DIM ** -0.5)
    s = lax.dot_general(q.astype(BF16), k_ref[...], (((1,), (1,)), ((), ())),
                        preferred_element_type=F32)
    m = jnp.max(s, axis=-1, keepdims=True)
    p = jnp.exp(s - m)
    den = jnp.sum(p, axis=-1, keepdims=True)
    o = _bdot((p / den).astype(BF16), v_ref[...])
    o_ref[...] = o.astype(o_ref.dtype)


def _mem_attend(xn, w_in, layer, q_col_block, kv):
    s, d = xn.shape
    bm, bn = ROW_BLOCK, MEM_HEAD_DIM
    return pl.pallas_call(
        _mem_q_kernel,
        grid=(s // bm, MEM_HEADS),
        in_specs=[
            pl.BlockSpec((bm, d), lambda i, h: (i, 0)),
            pl.BlockSpec((None, d, bn), lambda i, h: (layer, 0, q_col_block + h)),
            pl.BlockSpec((MEM_TOKENS, bn), lambda i, h: (0, h)),
            pl.BlockSpec((MEM_TOKENS, bn), lambda i, h: (0, MEM_HEADS + h)),
        ],
        out_specs=pl.BlockSpec((bm, bn), lambda i, h: (i, h)),
        out_shape=jax.ShapeDtypeStruct((s, MEM_WIDTH), BF16),
        compiler_params=_params(2),
        name="mem_attend",
    )(xn, w_in, kv, kv)


def _out_proj_kernel(a_ref, b_ref, wa_ref, wb_ref, o_ref):
    for chunk in range(a_ref.shape[0] // CHUNK_ROWS):
        rows = slice(chunk * CHUNK_ROWS, (chunk + 1) * CHUNK_ROWS)
        acc = _bdot(a_ref[rows, :], wa_ref[...].astype(BF16))
        acc = acc + _bdot(b_ref[rows, :], wb_ref[...].astype(BF16))
        o_ref[rows, :] = acc.astype(o_ref.dtype)


def _out_proj(a, b, w, layer):
    s, ka = a.shape
    kb = b.shape[1]
    n = w.shape[2]
    bm, bn = BIG_ROW_BLOCK, 512
    assert ka % kb == 0
    return pl.pallas_call(
        _out_proj_kernel,
        grid=(s // bm, n // bn),
        in_specs=[
            _resident_rows(bm, ka), _resident_rows(bm, kb),
            pl.BlockSpec((None, ka, bn), lambda i, j: (layer, 0, j)),
            pl.BlockSpec((None, kb, bn), lambda i, j: (layer, ka // kb, j)),
        ],
        out_specs=pl.BlockSpec((bm, bn), lambda i, j: (i, j)),
        out_shape=jax.ShapeDtypeStruct((s, n), BF16),
        compiler_params=_params(2),
        name="out_proj",
    )(a, b, w, w)


def _ffn_in_kernel(x_ref, wg_ref, wu_ref, cw_ref, o_ref, ext0_ref, ext1_ref, tails_ref):
    i, j = pl.program_id(0), pl.program_id(1)
    tail = jnp.where(i == 0, 0.0, tails_ref[j])
    for chunk in range(x_ref.shape[0] // CHUNK_ROWS):
        ext_ref = (ext0_ref, ext1_ref)[chunk % 2]
        rows = slice(chunk * CHUNK_ROWS, (chunk + 1) * CHUNK_ROWS)
        x = x_ref[rows, :]
        g = _bdot(x, wg_ref[...].astype(BF16))
        u = _bdot(x, wu_ref[...].astype(BF16))
        gc, tail = _causal_conv3(g, tail, cw_ref, ext_ref)
        act = gc / (1.0 + jnp.exp(-gc))
        o_ref[rows, :] = (act * u).astype(o_ref.dtype)
    tails_ref[j] = tail


def _ffn_in(xn, w_gate, w_up, conv_w, layer):
    s, d = xn.shape
    bm, bn = BIG_ROW_BLOCK, V7X_MXU_COLS
    nj = D_FF // bn
    wspec = pl.BlockSpec((None, d, bn), lambda i, j: (layer, 0, j))
    ext = pltpu.VMEM((CHUNK_ROWS + HALO, bn), F32)
    return pl.pallas_call(
        _ffn_in_kernel,
        grid=(s // bm, nj),
        in_specs=[
            _resident_rows(bm, d),
            wspec, wspec,
            pl.BlockSpec((None, 3, bn), lambda i, j: (layer, 0, j)),
        ],
        out_specs=pl.BlockSpec((bm, bn), lambda i, j: (i, j)),
        out_shape=jax.ShapeDtypeStruct((s, D_FF), BF16),
        scratch_shapes=[ext, ext, pltpu.VMEM((nj, HALO, bn), F32)],
        compiler_params=_params(2),
        name="ffn_in",
    )(xn, w_gate, w_up, conv_w)


def _ffn_out_kernel(a_ref, w_ref, o_ref, acc_ref, *, kdim):
    k = pl.program_id(1)
    bk = a_ref.shape[1]

    @pl.when(k == 0)
    def _():
        acc_ref[...] = jnp.zeros_like(acc_ref)

    valid = kdim - k * bk
    a_cols = lax.broadcasted_iota(jnp.int32, a_ref.shape, 1)
    a = jnp.where(a_cols < valid, a_ref[...], jnp.zeros_like(a_ref))
    w_rows = lax.broadcasted_iota(jnp.int32, (bk, FFN_OUT_COL_CHUNK), 0)
    for c in range(0, o_ref.shape[1], FFN_OUT_COL_CHUNK):
        cols = slice(c, c + FFN_OUT_COL_CHUNK)
        w = jnp.where(w_rows < valid, w_ref[:, cols], 0.0).astype(BF16)
        acc_ref[:, cols] += _bdot(a, w)

    @pl.when(k == pl.num_programs(1) - 1)
    def _():
        o_ref[...] = acc_ref[...].astype(o_ref.dtype)


FFN_OUT_COL_CHUNK = 512
FFN_OUT_K_BLOCK = 512


def _ffn_out(a, w_down, layer):
    s, kdim = a.shape
    n = w_down.shape[2]
    bm, bk = ROW_BLOCK, FFN_OUT_K_BLOCK
    return pl.pallas_call(
        functools.partial(_ffn_out_kernel, kdim=kdim),
        grid=(s // bm, pl.cdiv(kdim, bk)),
        in_specs=[
            pl.BlockSpec((bm, bk), lambda i, k: (i, k)),
            pl.BlockSpec((None, bk, n), lambda i, k: (layer, k, 0)),
        ],
        out_specs=pl.BlockSpec((bm, n), lambda i, k: (i, 0)),
        out_shape=jax.ShapeDtypeStruct((s, n), BF16),
        scratch_shapes=[pltpu.VMEM((bm, n), F32)],
        compiler_params=_params(2),
        name="ffn_out",
    )(a, w_down)


ATTN_IN_COLS = 512
HEADS_PER_ATTN_IN_BLOCK = ATTN_IN_COLS // HEAD_DIM
ATTN_IN_BLOCKS_PER_SECTION = GROUP_WIDTH // ATTN_IN_COLS
ATTN_IN_BLOCKS_PER_GROUP = 3 * ATTN_IN_BLOCKS_PER_SECTION
ATTN_IN_CHUNK_ROWS = 512


def _attn_in_kernel(x_ref, w_ref, cos_ref, sin_ref, o_ref):
    section = pl.program_id(1) // ATTN_IN_BLOCKS_PER_SECTION
    scale = jnp.where(section == 0, HEAD_DIM ** -0.5, 1.0).astype(F32)
    is_v = section == 2
    for chunk in range(x_ref.shape[0] // ATTN_IN_CHUNK_ROWS):
        rows = slice(chunk * ATTN_IN_CHUNK_ROWS, (chunk + 1) * ATTN_IN_CHUNK_ROWS)
        acc = _bdot(x_ref[rows, :], w_ref[...].astype(BF16))
        cos = cos_ref[rows, :] * scale
        sin = sin_ref[rows, :] * scale
        for h in range(HEADS_PER_ATTN_IN_BLOCK):
            cols = slice(h * HEAD_DIM, (h + 1) * HEAD_DIM)
            v = acc[:, cols]
            rot = pltpu.roll(v, HEAD_DIM // 2, axis=1)
            o_ref[rows, cols] = jnp.where(is_v, v, v * cos + rot * sin).astype(o_ref.dtype)


def _attn_in(xn, w_in, cos_tab, sin_tab, layer, group):
    s, d = xn.shape
    bm, bn = BIG_ROW_BLOCK, ATTN_IN_COLS
    nj = ATTN_IN_BLOCKS_PER_GROUP
    tab = pl.BlockSpec((None, bm, HEAD_DIM), lambda i, j: (group, i, 0))
    return pl.pallas_call(
        _attn_in_kernel,
        grid=(s // bm, nj),
        in_specs=[
            _resident_rows(bm, d),
            pl.BlockSpec((None, d, bn), lambda i, j: (layer, 0, group * nj + j)),
            tab, tab,
        ],
        out_specs=pl.BlockSpec((bm, bn), lambda i, j: (i, j)),
        out_shape=jax.ShapeDtypeStruct((s, 3 * GROUP_WIDTH), BF16),
        compiler_params=_params(2),
        name=f"attn_in_g{group}",
    )(xn, w_in, cos_tab, sin_tab)


def _window_attn_kernel(q_ref, k_ref, v_ref, o_ref, lse_ref, k2_ref, v2_ref):
    nb = pl.program_id(1)

    @pl.when(nb == 0)
    def _():
        k2_ref[...] = jnp.zeros_like(k2_ref)
        v2_ref[...] = jnp.zeros_like(v2_ref)

    own_half = nb % 2
    own_rows = pl.ds(pl.multiple_of(own_half * BLOCK, BLOCK), BLOCK)
    k2_ref[own_rows, :] = k_ref[...]
    v2_ref[own_rows, :] = v_ref[...]

    qi = lax.broadcasted_iota(jnp.int32, (BLOCK, 2 * BLOCK), 0)
    col = lax.broadcasted_iota(jnp.int32, (BLOCK, 2 * BLOCK), 1)
    ki = col & (BLOCK - 1)
    in_own_half = (col >> (BLOCK.bit_length() - 1)) == own_half
    valid = jnp.logical_or(
        jnp.logical_and(in_own_half, ki <= qi),
        jnp.logical_and(jnp.logical_not(in_own_half), jnp.logical_and(ki >= qi, nb > 0)))
    lane = lax.broadcasted_iota(jnp.int32, (BLOCK, V7X_LANES), 1)
    ones = jnp.ones((2 * BLOCK, HEAD_DIM), BF16)
    lse_all = jnp.zeros((BLOCK, V7X_LANES), F32)
    nt = (((1,), (1,)), ((), ()))
    for h in range(N_GROUP_HEADS):
        cols = slice(h * HEAD_DIM, (h + 1) * HEAD_DIM)
        s = lax.dot_general(q_ref[:, cols], k2_ref[:, cols], nt, preferred_element_type=F32)
        s = jnp.where(valid, s, NEG_INF)
        m = jnp.max(s, axis=-1, keepdims=True)
        p = jnp.exp(s - m).astype(BF16)
        ov = _bdot(p, jnp.concatenate([v2_ref[:, cols], ones], axis=1))
        den = ov[:, HEAD_DIM:]
        o_ref[:, cols] = (ov[:, :HEAD_DIM] / den).astype(o_ref.dtype)
        lse_all = jnp.where(lane == h, m + jnp.log(den), lse_all)
    lse_ref[...] = lse_all


def _window_attn(h_group, group):
    s = h_group.shape[0]
    _, d = DILATED_GROUPS[group]
    blocks_per_class = s // d // BLOCK
    spec = lambda sec: pl.BlockSpec(
        (BLOCK, GROUP_WIDTH), lambda r, nb: (r * blocks_per_class + nb, sec))
    return pl.pallas_call(
        _window_attn_kernel,
        grid=(d, blocks_per_class),
        in_specs=[spec(0), spec(1), spec(2)],
        out_specs=[
            pl.BlockSpec((BLOCK, GROUP_WIDTH), lambda r, nb: (r * blocks_per_class + nb, 0)),
            pl.BlockSpec((BLOCK, V7X_LANES), lambda r, nb: (r * blocks_per_class + nb, 0)),
        ],
        out_shape=[
            jax.ShapeDtypeStruct((s, GROUP_WIDTH), BF16),
            jax.ShapeDtypeStruct((s, V7X_LANES), F32),
        ],
        scratch_shapes=[pltpu.VMEM((2 * BLOCK, GROUP_WIDTH), BF16)] * 2,
        compiler_params=_params(2),
        name=f"window_attn_g{group}",
    )(h_group, h_group, h_group)


def _merge_kernel(o0_ref, o1_ref, o2_ref, l0_ref, l1_ref, l2_ref, out_ref):
    outs = [o0_ref[...].astype(F32)]
    lses = [l0_ref[...]]
    for o_ref, l_ref in ((o1_ref, l1_ref), (o2_ref, l2_ref)):
        d = o_ref.shape[0]
        perm = _row_permutation(NORM_ROWS, d, False)
        o_cm = jnp.concatenate([o_ref[r] for r in range(d)], axis=0)
        l_cm = jnp.concatenate([l_ref[r] for r in range(d)], axis=0)
        outs.append(_bdot(perm, o_cm))
        hi = l_cm.astype(BF16)
        rest = l_cm - hi.astype(F32)
        mid = rest.astype(BF16)
        lo = (rest - mid.astype(F32)).astype(BF16)
        lses.append(_bdot(perm, hi) + _bdot(perm, mid) + _bdot(perm, lo))
    l0, l1, l2 = lses
    m = jnp.maximum(jnp.maximum(l0, l1), l2)
    e0, e1, e2 = jnp.exp(l0 - m), jnp.exp(l1 - m), jnp.exp(l2 - m)
    tot = e0 + e1 + e2
    a0, a1, a2 = e0 / tot, e1 / tot, e2 / tot
    for h in range(N_GROUP_HEADS):
        cols = slice(h * HEAD_DIM, (h + 1) * HEAD_DIM)
        merged = (a0[:, h:h + 1] * outs[0][:, cols] + a1[:, h:h + 1] * outs[1][:, cols]
                  + a2[:, h:h + 1] * outs[2][:, cols])
        out_ref[:, cols] = merged.astype(out_ref.dtype)


def _merge_groups(outs, lses):
    s = outs[0].shape[0]
    rows = NORM_ROWS

    def specs(width):
        out = [pl.BlockSpec((rows, width), lambda i: (i, 0))]
        for _, d in DILATED_GROUPS[1:]:
            out.append(pl.BlockSpec((d, rows // d, width), lambda i: (0, i, 0)))
        return out

    class_major = lambda a, d: a.reshape(d, s // d, a.shape[1])
    outs = [outs[0]] + [class_major(o, d) for o, (_, d) in zip(outs[1:], DILATED_GROUPS[1:])]
    lses = [lses[0]] + [class_major(l, d) for l, (_, d) in zip(lses[1:], DILATED_GROUPS[1:])]
    return pl.pallas_call(
        _merge_kernel,
        grid=(s // rows,),
        in_specs=specs(GROUP_WIDTH) + specs(V7X_LANES),
        out_specs=pl.BlockSpec((rows, GROUP_WIDTH), lambda i: (i, 0)),
        out_shape=jax.ShapeDtypeStruct((s, GROUP_WIDTH), BF16),
        compiler_params=_params(1),
        name="merge_groups",
    )(*outs, *lses)


def kernel(x, mem, positions, g_mem, w_mem_kv, g_mix_pre, g_mix_post, g_ffn_pre, g_ffn_post,
           w_conv_in, conv_mix_w, w_conv_out, w_attn_in, w_attn_out,
           w_ffn_gate, w_ffn_up, conv_ffn_w, w_ffn_down):
    b, s, d = x.shape
    assert b == 1 and s == SEQ and d == D_MODEL
    xs = x.reshape(s, d)
    gains = lambda g: g.reshape(DEPTH, 1, d)
    g_mix_pre, g_mix_post = gains(g_mix_pre), gains(g_mix_post)
    g_ffn_pre, g_ffn_post = gains(g_ffn_pre), gains(g_ffn_post)

    kv = _mem_kv(mem.reshape(MEM_TOKENS, d), g_mem, w_mem_kv)
    cos_tab, sin_tab = _rope_tables(positions)

    dilations = tuple(dil for _, dil in DILATED_GROUPS[1:])
    xn = _rmsnorm_bf16(xs, g_mix_pre, 0)
    xn_class_major = ()
    for i in range(DEPTH):
        j = i // 2
        if i % 2 == 0:
            y_mix = _conv_in(xn, w_conv_in, conv_mix_w, j)
            y_mem = _mem_attend(xn, w_conv_in, j, 3 * CONV_WIDTH // MEM_HEAD_DIM, kv)
            y = _out_proj(y_mix, y_mem, w_conv_out, j)
        else:
            lhs = [xn] + [xp.reshape(s, d) for xp in xn_class_major]
            parts = [
                _window_attn(_attn_in(lhs[g], w_attn_in, cos_tab, sin_tab, j, g), g)
                for g in range(N_GROUPS)]
            y_mix = _merge_groups([p[0] for p in parts], [p[1] for p in parts])
            y_mem = _mem_attend(xn, w_attn_in, j, N_SELF // MEM_HEAD_DIM, kv)
            y = _out_proj(y_mix, y_mem, w_attn_out, j)
        xs, xn = _residual_norm(y, xs, g_mix_post, i, g_ffn_pre, i)
        a = _ffn_in(xn, w_ffn_gate, w_ffn_up, conv_ffn_w, i)
        y = _ffn_out(a, w_ffn_down, i)
        if i + 1 == DEPTH:
            xs = _residual_norm(y, xs, g_ffn_post, i)
        elif (i + 1) % 2 == 1:
            xs, xn, *xn_class_major = _residual_norm(
                y, xs, g_ffn_post, i, g_mix_pre, i + 1, dilations=dilations)
        else:
            xs, xn = _residual_norm(y, xs, g_ffn_post, i, g_mix_pre, i + 1)
    return xs.reshape(b, s, d)
```

```python
import functools

import jax
import jax.numpy as jnp
from jax import lax
from jax.experimental import pallas as pl
from jax.experimental.pallas import tpu as pltpu

D_MODEL = 4096
SEQ = 8192
DEPTH = 2
HEAD_DIM = 128
N_GROUP_HEADS = 16
GROUP_WIDTH = N_GROUP_HEADS * HEAD_DIM
DILATED_GROUPS = ((128, 1), (512, 4), (2048, 16))
N_GROUPS = len(DILATED_GROUPS)
N_SELF = N_GROUPS * 3 * GROUP_WIDTH
BLOCK = 128
CONV_WIDTH = 3 * D_MODEL // 4
MEM_TOKENS = 256
MEM_HEADS = 4
MEM_HEAD_DIM = 256
MEM_WIDTH = MEM_HEADS * MEM_HEAD_DIM
D_FF = 11008
ROPE_THETA = 10000.0
EPS = 1e-6
NEG_INF = -1e30

V7X_SUBLANES = 8
V7X_LANES = 128
V7X_MXU_COLS = 256
V7X_VMEM_LIMIT_BYTES = 60 * 1024 * 1024

BF16 = jnp.bfloat16
F32 = jnp.float32

ROW_BLOCK = 1024
BIG_ROW_BLOCK = 2048
CHUNK_ROWS = 512
NORM_ROWS = 256
HALO = V7X_SUBLANES


def _params(n_axes):
    return pltpu.CompilerParams(
        dimension_semantics=("arbitrary",) * n_axes,
        vmem_limit_bytes=V7X_VMEM_LIMIT_BYTES,
    )


def _rms_scale(v):
    return lax.rsqrt(jnp.mean(v * v, axis=-1, keepdims=True) + EPS)


def _bdot(a, b):
    return jnp.dot(a, b, preferred_element_type=F32)


def _norm_kernel(x_ref, g_ref, o_ref):
    x = x_ref[...]
    o_ref[...] = (x * _rms_scale(x) * g_ref[...]).astype(o_ref.dtype)


def _rmsnorm_bf16(x, g_all, layer):
    s, d = x.shape
    return pl.pallas_call(
        _norm_kernel,
        grid=(s // NORM_ROWS,),
        in_specs=[
            pl.BlockSpec((NORM_ROWS, d), lambda i: (i, 0)),
            pl.BlockSpec((None, 1, d), lambda i: (layer, 0, 0)),
        ],
        out_specs=pl.BlockSpec((NORM_ROWS, d), lambda i: (i, 0)),
        out_shape=jax.ShapeDtypeStruct((s, d), BF16),
        compiler_params=_params(1),
        name="rmsnorm_bf16",
    )(x, g_all)


def _residual_kernel(y_ref, x_ref, gpost_ref, gnext_ref, xo_ref, xn_ref):
    y = y_ref[...].astype(F32)
    xnew = x_ref[...] + y * _rms_scale(y) * gpost_ref[...]
    xo_ref[...] = xnew
    xn_ref[...] = (xnew * _rms_scale(xnew) * gnext_ref[...]).astype(xn_ref.dtype)


def _row_permutation(rows, d, to_class_major):
    n = rows // d
    out_row = lax.broadcasted_iota(jnp.int32, (rows, rows), 0)
    in_row = lax.broadcasted_iota(jnp.int32, (rows, rows), 1)
    if to_class_major:
        src = (out_row & (n - 1)) * d + (out_row >> (n.bit_length() - 1))
    else:
        src = (out_row & (d - 1)) * n + (out_row >> (d.bit_length() - 1))
    return (in_row == src).astype(BF16)


def _residual_dilated_kernel(y_ref, x_ref, gpost_ref, gnext_ref, xo_ref, xn_ref, *xp_refs):
    y = y_ref[...].astype(F32)
    xnew = x_ref[...] + y * _rms_scale(y) * gpost_ref[...]
    xo_ref[...] = xnew
    xn = (xnew * _rms_scale(xnew) * gnext_ref[...]).astype(BF16)
    xn_ref[...] = xn
    for xp_ref in xp_refs:
        d, n = xp_ref.shape[0], xp_ref.shape[1]
        xp = _bdot(_row_permutation(NORM_ROWS, d, True), xn).astype(BF16)
        for r in range(d):
            xp_ref[r] = xp[r * n:(r + 1) * n, :]


def _residual_last_kernel(y_ref, x_ref, gpost_ref, xo_ref):
    y = y_ref[...].astype(F32)
    xo_ref[...] = x_ref[...] + y * _rms_scale(y) * gpost_ref[...]


def _residual_norm(y, x, g_post, post_layer, g_next=None, next_layer=None, dilations=()):
    s, d = x.shape
    row = pl.BlockSpec((NORM_ROWS, d), lambda i: (i, 0))
    gspec = lambda layer: pl.BlockSpec((None, 1, d), lambda i: (layer, 0, 0))
    if dilations:
        return pl.pallas_call(
            _residual_dilated_kernel,
            grid=(s // NORM_ROWS,),
            in_specs=[row, row, gspec(post_layer), gspec(next_layer)],
            out_specs=[row, row] + [
                pl.BlockSpec((dil, NORM_ROWS // dil, d), lambda i: (0, i, 0)) for dil in dilations],
            out_shape=[jax.ShapeDtypeStruct((s, d), F32), jax.ShapeDtypeStruct((s, d), BF16)] + [
                jax.ShapeDtypeStruct((dil, s // dil, d), BF16) for dil in dilations],
            compiler_params=_params(1),
            name="residual_norm_dilated",
        )(y, x, g_post, g_next)
    if g_next is None:
        return pl.pallas_call(
            _residual_last_kernel,
            grid=(s // NORM_ROWS,),
            in_specs=[row, row, gspec(post_layer)],
            out_specs=row,
            out_shape=jax.ShapeDtypeStruct((s, d), F32),
            compiler_params=_params(1),
            name="residual_last",
        )(y, x, g_post)
    return pl.pallas_call(
        _residual_kernel,
        grid=(s // NORM_ROWS,),
        in_specs=[row, row, gspec(post_layer), gspec(next_layer)],
        out_specs=[row, row],
        out_shape=[jax.ShapeDtypeStruct((s, d), F32), jax.ShapeDtypeStruct((s, d), BF16)],
        compiler_params=_params(1),
        name="residual_norm",
    )(y, x, g_post, g_next)


def _rope_table_kernel(pos_ref, freq_ref, sign_ref, cos_ref, sin_ref):
    ang = pos_ref[...].astype(F32) * freq_ref[...]
    cos_ref[...] = jnp.cos(ang)
    sin_ref[...] = jnp.sin(ang) * sign_ref[...]


def _rope_tables(positions):
    s = positions.shape[-1]
    half = HEAD_DIM // 2
    inv_freq = ROPE_THETA ** (-jnp.arange(half, dtype=F32) * 2.0 / HEAD_DIM)
    freq = jnp.concatenate([inv_freq, inv_freq]).reshape(1, HEAD_DIM)
    sign = jnp.concatenate([-jnp.ones((half,), F32), jnp.ones((half,), F32)]).reshape(1, HEAD_DIM)
    pos = jnp.stack([positions.reshape(s // d, d).T.reshape(s, 1) for _, d in DILATED_GROUPS])
    rows = 1024
    tab = pl.BlockSpec((None, rows, HEAD_DIM), lambda g, i: (g, i, 0))
    const = pl.BlockSpec((1, HEAD_DIM), lambda g, i: (0, 0))
    return pl.pallas_call(
        _rope_table_kernel,
        grid=(N_GROUPS, s // rows),
        in_specs=[pl.BlockSpec((None, rows, 1), lambda g, i: (g, i, 0)), const, const],
        out_specs=[tab, tab],
        out_shape=[jax.ShapeDtypeStruct((N_GROUPS, s, HEAD_DIM), F32)] * 2,
        compiler_params=_params(2),
        name="rope_tables",
    )(pos, freq, sign)


def _mem_kv_kernel(mem_ref, g_ref, w_ref, o_ref):
    m = mem_ref[...]
    mn = (m * _rms_scale(m) * g_ref[...]).astype(BF16)
    o_ref[...] = _bdot(mn, w_ref[...].astype(BF16)).astype(o_ref.dtype)


def _mem_kv(mem, g_mem, w_mem_kv):
    m, d = mem.shape
    n = w_mem_kv.shape[1]
    bn = 512
    return pl.pallas_call(
        _mem_kv_kernel,
        grid=(n // bn,),
        in_specs=[
            pl.BlockSpec((m, d), lambda j: (0, 0)),
            pl.BlockSpec((1, d), lambda j: (0, 0)),
            pl.BlockSpec((d, bn), lambda j: (0, j)),
        ],
        out_specs=pl.BlockSpec((m, bn), lambda j: (0, j)),
        out_shape=jax.ShapeDtypeStruct((m, n), BF16),
        compiler_params=_params(1),
        name="mem_kv",
    )(mem, g_mem.reshape(1, d), w_mem_kv)


def _causal_conv3(v, prev_tail, cw_ref, ext_ref):
    bm = v.shape[0]
    ext_ref[0:HALO, :] = prev_tail
    ext_ref[HALO:HALO + bm, :] = v
    v1 = ext_ref[HALO - 1:HALO - 1 + bm, :]
    v2 = ext_ref[HALO - 2:HALO - 2 + bm, :]
    conv = cw_ref[0:1, :] * v2 + cw_ref[1:2, :] * v1 + cw_ref[2:3, :] * v
    return conv, v[bm - HALO:bm, :]


def _row_chunks(total):
    return [slice(a, a + CHUNK_ROWS) for a in range(0, total, CHUNK_ROWS)]


def _resident_rows(bm, k, buffers=2):
    return pl.BlockSpec((bm, k), lambda i, j: (i, 0), pipeline_mode=pl.Buffered(buffers))


def _conv_in_kernel(x_ref, wb_ref, wc_ref, wu_ref, cw_ref, o_ref, ext0_ref, ext1_ref, tails_ref):
    i, j = pl.program_id(0), pl.program_id(1)
    tail = jnp.where(i == 0, 0.0, tails_ref[j])
    for chunk in range(x_ref.shape[0] // CHUNK_ROWS):
        ext_ref = (ext0_ref, ext1_ref)[chunk % 2]
        rows = slice(chunk * CHUNK_ROWS, (chunk + 1) * CHUNK_ROWS)
        x = x_ref[rows, :]
        b = _bdot(x, wb_ref[...].astype(BF16))
        c = _bdot(x, wc_ref[...].astype(BF16))
        u = _bdot(x, wu_ref[...].astype(BF16))
        conv, tail = _causal_conv3(c * u, tail, cw_ref, ext_ref)
        o_ref[rows, :] = (b * conv).astype(o_ref.dtype)
    tails_ref[j] = tail


def _conv_in(xn, w_in, conv_w, layer):
    s, d = xn.shape
    bm, bn = BIG_ROW_BLOCK, V7X_MXU_COLS
    nj = CONV_WIDTH // bn
    wspec = lambda off: pl.BlockSpec((None, d, bn), lambda i, j: (layer, 0, j + off))
    ext = pltpu.VMEM((CHUNK_ROWS + HALO, bn), F32)
    return pl.pallas_call(
        _conv_in_kernel,
        grid=(s // bm, nj),
        in_specs=[
            _resident_rows(bm, d, buffers=1),
            wspec(0), wspec(nj), wspec(2 * nj),
            pl.BlockSpec((None, 3, bn), lambda i, j: (layer, 0, j)),
        ],
        out_specs=pl.BlockSpec((bm, bn), lambda i, j: (i, j)),
        out_shape=jax.ShapeDtypeStruct((s, CONV_WIDTH), BF16),
        scratch_shapes=[ext, ext, pltpu.VMEM((nj, HALO, bn), F32)],
        compiler_params=_params(2),
        name="conv_in",
    )(xn, w_in, w_in, w_in, conv_w)


def _mem_q_kernel(x_ref, wq_ref, k_ref, v_ref, o_ref):
    q = _bdot(x_ref[...], wq_ref[...].astype(BF16)) * (MEM_HEAD_DIM ** -0.5)
    s = lax.dot_general(q.astype(BF16), k_ref[...], (((1,), (1,)), ((), ())),
                        preferred_element_type=F32)
    m = jnp.max(s, axis=-1, keepdims=True)
    p = jnp.exp(s - m)
    den = jnp.sum(p, axis=-1, keepdims=True)
    o = _bdot((p / den).astype(BF16), v_ref[...])
    o_ref[...] = o.astype(o_ref.dtype)


def _mem_attend(xn, w_in, layer, q_col_block, kv):
    s, d = xn.shape
    bm, bn = ROW_BLOCK, MEM_HEAD_DIM
    return pl.pallas_call(
        _mem_q_kernel,
        grid=(s // bm, MEM_HEADS),
        in_specs=[
            pl.BlockSpec((bm, d), lambda i, h: (i, 0)),
            pl.BlockSpec((None, d, bn), lambda i, h: (layer, 0, q_col_block + h)),
            pl.BlockSpec((MEM_TOKENS, bn), lambda i, h: (0, h)),
            pl.BlockSpec((MEM_TOKENS, bn), lambda i, h: (0, MEM_HEADS + h)),
        ],
        out_specs=pl.BlockSpec((bm, bn), lambda i, h: (i, h)),
        out_shape=jax.ShapeDtypeStruct((s, MEM_WIDTH), BF16),
        compiler_params=_params(2),
        name="mem_attend",
    )(xn, w_in, kv, kv)


def _out_proj_kernel(a_ref, b_ref, wa_ref, wb_ref, o_ref):
    for chunk in range(a_ref.shape[0] // CHUNK_ROWS):
        rows = slice(chunk * CHUNK_ROWS, (chunk + 1) * CHUNK_ROWS)
        acc = _bdot(a_ref[rows, :], wa_ref[...].astype(BF16))
        acc = acc + _bdot(b_ref[rows, :], wb_ref[...].astype(BF16))
        o_ref[rows, :] = acc.astype(o_ref.dtype)


def _out_proj(a, b, w, layer):
    s, ka = a.shape
    kb = b.shape[1]
    n = w.shape[2]
    bm, bn = BIG_ROW_BLOCK, 512
    assert ka % kb == 0
    return pl.pallas_call(
        _out_proj_kernel,
        grid=(s // bm, n // bn),
        in_specs=[
            _resident_rows(bm, ka), _resident_rows(bm, kb),
            pl.BlockSpec((None, ka, bn), lambda i, j: (layer, 0, j)),
            pl.BlockSpec((None, kb, bn), lambda i, j: (layer, ka // kb, j)),
        ],
        out_specs=pl.BlockSpec((bm, bn), lambda i, j: (i, j)),
        out_shape=jax.ShapeDtypeStruct((s, n), BF16),
        compiler_params=_params(2),
        name="out_proj",
    )(a, b, w, w)


def _ffn_in_kernel(x_ref, wg_ref, wu_ref, cw_ref, o_ref, ext0_ref, ext1_ref, tails_ref):
    i, j = pl.program_id(0), pl.program_id(1)
    tail = jnp.where(i == 0, 0.0, tails_ref[j])
    for chunk, rows in enumerate(_row_chunks(x_ref.shape[0])):
        ext_ref = (ext0_ref, ext1_ref)[chunk % 2]
        x = x_ref[rows, :]
        g = _bdot(x, wg_ref[...].astype(BF16))
        u = _bdot(x, wu_ref[...].astype(BF16))
        gc, tail = _causal_conv3(g, tail, cw_ref, ext_ref)
        act = gc / (1.0 + jnp.exp(-gc))
        o_ref[rows, :] = (act * u).astype(o_ref.dtype)
    tails_ref[j] = tail


def _ffn_in(xn, w_gate, w_up, conv_w, layer):
    s, d = xn.shape
    bm, bn = BIG_ROW_BLOCK, V7X_MXU_COLS
    nj = D_FF // bn
    wspec = pl.BlockSpec((None, d, bn), lambda i, j: (layer, 0, j))
    ext = pltpu.VMEM((CHUNK_ROWS + HALO, bn), F32)
    return pl.pallas_call(
        _ffn_in_kernel,
        grid=(s // bm, nj),
        in_specs=[
            _resident_rows(bm, d),
            wspec, wspec,
            pl.BlockSpec((None, 3, bn), lambda i, j: (layer, 0, j)),
        ],
        out_specs=pl.BlockSpec((bm, bn), lambda i, j: (i, j)),
        out_shape=jax.ShapeDtypeStruct((s, D_FF), BF16),
        scratch_shapes=[ext, ext, pltpu.VMEM((nj, HALO, bn), F32)],
        compiler_params=_params(2),
        name="ffn_in",
    )(xn, w_gate, w_up, conv_w)


def _ffn_out_kernel(a_ref, w_ref, o_ref, acc_ref, *, kdim):
    k = pl.program_id(2)
    bk = a_ref.shape[1]

    @pl.when(k == 0)
    def _():
        acc_ref[...] = jnp.zeros_like(acc_ref)

    valid = kdim - k * bk
    a_cols = lax.broadcasted_iota(jnp.int32, a_ref.shape, 1)
    a = jnp.where(a_cols < valid, a_ref[...], jnp.zeros_like(a_ref))
    w_rows = lax.broadcasted_iota(jnp.int32, (bk, FFN_OUT_COL_CHUNK), 0)
    for c in range(0, o_ref.shape[1], FFN_OUT_COL_CHUNK):
        cols = slice(c, c + FFN_OUT_COL_CHUNK)
        w = jnp.where(w_rows < valid, w_ref[:, cols], 0.0).astype(BF16)
        acc_ref[:, cols] += _bdot(a, w)

    @pl.when(k == pl.num_programs(2) - 1)
    def _():
        o_ref[...] = acc_ref[...].astype(o_ref.dtype)


FFN_OUT_COL_CHUNK = 512
FFN_OUT_K_BLOCK = 1024
FFN_OUT_COLS = 2048


def _ffn_out(a, w_down, layer):
    s, kdim = a.shape
    n = w_down.shape[2]
    bm, bk, bn = ROW_BLOCK, FFN_OUT_K_BLOCK, FFN_OUT_COLS
    return pl.pallas_call(
        functools.partial(_ffn_out_kernel, kdim=kdim),
        grid=(s // bm, n // bn, pl.cdiv(kdim, bk)),
        in_specs=[
            pl.BlockSpec((bm, bk), lambda i, j, k: (i, k)),
            pl.BlockSpec((None, bk, bn), lambda i, j, k: (layer, k, j)),
        ],
        out_specs=pl.BlockSpec((bm, bn), lambda i, j, k: (i, j)),
        out_shape=jax.ShapeDtypeStruct((s, n), BF16),
        scratch_shapes=[pltpu.VMEM((bm, bn), F32)],
        compiler_params=_params(3),
        name="ffn_out",
    )(a, w_down)


ATTN_IN_COLS = 512
HEADS_PER_ATTN_IN_BLOCK = ATTN_IN_COLS // HEAD_DIM
ATTN_IN_BLOCKS_PER_SECTION = GROUP_WIDTH // ATTN_IN_COLS
ATTN_IN_BLOCKS_PER_GROUP = 3 * ATTN_IN_BLOCKS_PER_SECTION
ATTN_IN_CHUNK_ROWS = 512


def _attn_in_kernel(x_ref, w_ref, cos_ref, sin_ref, o_ref):
    section = pl.program_id(1) // ATTN_IN_BLOCKS_PER_SECTION
    scale = jnp.where(section == 0, HEAD_DIM ** -0.5, 1.0).astype(F32)
    is_v = section == 2
    for chunk in range(x_ref.shape[0] // ATTN_IN_CHUNK_ROWS):
        rows = slice(chunk * ATTN_IN_CHUNK_ROWS, (chunk + 1) * ATTN_IN_CHUNK_ROWS)
        acc = _bdot(x_ref[rows, :], w_ref[...].astype(BF16))
        cos = cos_ref[rows, :] * scale
        sin = sin_ref[rows, :] * scale
        for h in range(HEADS_PER_ATTN_IN_BLOCK):
            cols = slice(h * HEAD_DIM, (h + 1) * HEAD_DIM)
            v = acc[:, cols]
            rot = pltpu.roll(v, HEAD_DIM // 2, axis=1)
            o_ref[rows, cols] = jnp.where(is_v, v, v * cos + rot * sin).astype(o_ref.dtype)


def _attn_in(xn, w_in, cos_tab, sin_tab, layer, group):
    s, d = xn.shape
    bm, bn = BIG_ROW_BLOCK, ATTN_IN_COLS
    nj = ATTN_IN_BLOCKS_PER_GROUP
    tab = pl.BlockSpec((None, bm, HEAD_DIM), lambda i, j: (group, i, 0))
    return pl.pallas_call(
        _attn_in_kernel,
        grid=(s // bm, nj),
        in_specs=[
            _resident_rows(bm, d),
            pl.BlockSpec((None, d, bn), lambda i, j: (layer, 0, group * nj + j)),
            tab, tab,
        ],
        out_specs=pl.BlockSpec((bm, bn), lambda i, j: (i, j)),
        out_shape=jax.ShapeDtypeStruct((s, 3 * GROUP_WIDTH), BF16),
        compiler_params=_params(2),
        name=f"attn_in_g{group}",
    )(xn, w_in, cos_tab, sin_tab)


def _window_attn_kernel(q_ref, k_ref, v_ref, o_ref, lse_ref, k2_ref, v2_ref):
    nb = pl.program_id(1)

    @pl.when(nb == 0)
    def _():
        k2_ref[...] = jnp.zeros_like(k2_ref)
        v2_ref[...] = jnp.zeros_like(v2_ref)

    own_half = nb % 2
    own_rows = pl.ds(pl.multiple_of(own_half * BLOCK, BLOCK), BLOCK)
    k2_ref[own_rows, :] = k_ref[...]
    v2_ref[own_rows, :] = v_ref[...]

    qi = lax.broadcasted_iota(jnp.int32, (BLOCK, 2 * BLOCK), 0)
    col = lax.broadcasted_iota(jnp.int32, (BLOCK, 2 * BLOCK), 1)
    ki = col & (BLOCK - 1)
    in_own_half = (col >> (BLOCK.bit_length() - 1)) == own_half
    valid = jnp.logical_or(
        jnp.logical_and(in_own_half, ki <= qi),
        jnp.logical_and(jnp.logical_not(in_own_half), jnp.logical_and(ki >= qi, nb > 0)))
    lane = lax.broadcasted_iota(jnp.int32, (BLOCK, V7X_LANES), 1)
    ones = jnp.ones((2 * BLOCK, HEAD_DIM), BF16)
    lse_all = jnp.zeros((BLOCK, V7X_LANES), F32)
    nt = (((1,), (1,)), ((), ()))
    for h in range(N_GROUP_HEADS):
        cols = slice(h * HEAD_DIM, (h + 1) * HEAD_DIM)
        s = lax.dot_general(q_ref[:, cols], k2_ref[:, cols], nt, preferred_element_type=F32)
        s = jnp.where(valid, s, NEG_INF)
        m = jnp.max(s, axis=-1, keepdims=True)
        p = jnp.exp(s - m).astype(BF16)
        ov = _bdot(p, jnp.concatenate([v2_ref[:, cols], ones], axis=1))
        den = ov[:, HEAD_DIM:]
        o_ref[:, cols] = (ov[:, :HEAD_DIM] / den).astype(o_ref.dtype)
        lse_all = jnp.where(lane == h, m + jnp.log(den), lse_all)
    lse_ref[...] = lse_all


def _window_attn(h_group, group):
    s = h_group.shape[0]
    _, d = DILATED_GROUPS[group]
    blocks_per_class = s // d // BLOCK
    spec = lambda sec: pl.BlockSpec(
        (BLOCK, GROUP_WIDTH), lambda r, nb: (r * blocks_per_class + nb, sec))
    return pl.pallas_call(
        _window_attn_kernel,
        grid=(d, blocks_per_class),
        in_specs=[spec(0), spec(1), spec(2)],
        out_specs=[
            pl.BlockSpec((BLOCK, GROUP_WIDTH), lambda r, nb: (r * blocks_per_class + nb, 0)),
            pl.BlockSpec((BLOCK, V7X_LANES), lambda r, nb: (r * blocks_per_class + nb, 0)),
        ],
        out_shape=[
            jax.ShapeDtypeStruct((s, GROUP_WIDTH), BF16),
            jax.ShapeDtypeStruct((s, V7X_LANES), F32),
        ],
        scratch_shapes=[pltpu.VMEM((2 * BLOCK, GROUP_WIDTH), BF16)] * 2,
        compiler_params=_params(2),
        name=f"window_attn_g{group}",
    )(h_group, h_group, h_group)


def _merge_kernel(o0_ref, o1_ref, o2_ref, l0_ref, l1_ref, l2_ref, out_ref):
    outs = [o0_ref[...].astype(F32)]
    lses = [l0_ref[...]]
    for o_ref, l_ref in ((o1_ref, l1_ref), (o2_ref, l2_ref)):
        d = o_ref.shape[0]
        perm = _row_permutation(NORM_ROWS, d, False)
        o_cm = jnp.concatenate([o_ref[r] for r in range(d)], axis=0)
        l_cm = jnp.concatenate([l_ref[r] for r in range(d)], axis=0)
        outs.append(_bdot(perm, o_cm))
        hi = l_cm.astype(BF16)
        rest = l_cm - hi.astype(F32)
        mid = rest.astype(BF16)
        lo = (rest - mid.astype(F32)).astype(BF16)
        lses.append(_bdot(perm, hi) + _bdot(perm, mid) + _bdot(perm, lo))
    l0, l1, l2 = lses
    m = jnp.maximum(jnp.maximum(l0, l1), l2)
    e0, e1, e2 = jnp.exp(l0 - m), jnp.exp(l1 - m), jnp.exp(l2 - m)
    tot = e0 + e1 + e2
    a0, a1, a2 = e0 / tot, e1 / tot, e2 / tot
    for h in range(N_GROUP_HEADS):
        cols = slice(h * HEAD_DIM, (h + 1) * HEAD_DIM)
        merged = (a0[:, h:h + 1] * outs[0][:, cols] + a1[:, h:h + 1] * outs[1][:, cols]
                  + a2[:, h:h + 1] * outs[2][:, cols])
        out_ref[:, cols] = merged.astype(out_ref.dtype)


def _merge_groups(outs, lses):
    s = outs[0].shape[0]
    rows = NORM_ROWS

    def specs(width):
        out = [pl.BlockSpec((rows, width), lambda i: (i, 0))]
        for _, d in DILATED_GROUPS[1:]:
            out.append(pl.BlockSpec((d, rows // d, width), lambda i: (0, i, 0)))
        return out

    class_major = lambda a, d: a.reshape(d, s // d, a.shape[1])
    outs = [outs[0]] + [class_major(o, d) for o, (_, d) in zip(outs[1:], DILATED_GROUPS[1:])]
    lses = [lses[0]] + [class_major(l, d) for l, (_, d) in zip(lses[1:], DILATED_GROUPS[1:])]
    return pl.pallas_call(
        _merge_kernel,
        grid=(s // rows,),
        in_specs=specs(GROUP_WIDTH) + specs(V7X_LANES),
        out_specs=pl.BlockSpec((rows, GROUP_WIDTH), lambda i: (i, 0)),
        out_shape=jax.ShapeDtypeStruct((s, GROUP_WIDTH), BF16),
        compiler_params=_params(1),
        name="merge_groups",
    )(*outs, *lses)


def kernel(x, mem, positions, g_mem, w_mem_kv, g_mix_pre, g_mix_post, g_ffn_pre, g_ffn_post,
           w_conv_in, conv_mix_w, w_conv_out, w_attn_in, w_attn_out,
           w_ffn_gate, w_ffn_up, conv_ffn_w, w_ffn_down):
    b, s, d = x.shape
    assert b == 1 and s == SEQ and d == D_MODEL
    xs = x.reshape(s, d)
    gains = lambda g: g.reshape(DEPTH, 1, d)
    g_mix_pre, g_mix_post = gains(g_mix_pre), gains(g_mix_post)
    g_ffn_pre, g_ffn_post = gains(g_ffn_pre), gains(g_ffn_post)

    kv = _mem_kv(mem.reshape(MEM_TOKENS, d), g_mem, w_mem_kv)
    cos_tab, sin_tab = _rope_tables(positions)

    dilations = tuple(dil for _, dil in DILATED_GROUPS[1:])
    xn = _rmsnorm_bf16(xs, g_mix_pre, 0)
    xn_class_major = ()
    for i in range(DEPTH):
        j = i // 2
        if i % 2 == 0:
            y_mix = _conv_in(xn, w_conv_in, conv_mix_w, j)
            y_mem = _mem_attend(xn, w_conv_in, j, 3 * CONV_WIDTH // MEM_HEAD_DIM, kv)
            y = _out_proj(y_mix, y_mem, w_conv_out, j)
        else:
            lhs = [xn] + [xp.reshape(s, d) for xp in xn_class_major]
            parts = [
                _window_attn(_attn_in(lhs[g], w_attn_in, cos_tab, sin_tab, j, g), g)
                for g in range(N_GROUPS)]
            y_mix = _merge_groups([p[0] for p in parts], [p[1] for p in parts])
            y_mem = _mem_attend(xn, w_attn_in, j, N_SELF // MEM_HEAD_DIM, kv)
            y = _out_proj(y_mix, y_mem, w_attn_out, j)
        xs, xn = _residual_norm(y, xs, g_mix_post, i, g_ffn_pre, i)
        a = _ffn_in(xn, w_ffn_gate, w_ffn_up, conv_ffn_w, i)
        y = _ffn_out(a, w_ffn_down, i)
        if i + 1 == DEPTH:
            xs = _residual_norm(y, xs, g_ffn_post, i)
        elif (i + 1) % 2 == 1:
            xs, xn, *xn_class_major = _residual_norm(
                y, xs, g_ffn_post, i, g_mix_pre, i + 1, dilations=dilations)
        else:
            xs, xn = _residual_norm(y, xs, g_ffn_post, i, g_mix_pre, i + 1)
    return xs.reshape(b, s, d)
```

```python
import functools

import jax
import jax.numpy as jnp
from jax import lax
from jax.experimental import pallas as pl
from jax.experimental.pallas import tpu as pltpu

D_MODEL = 4096
SEQ = 8192
DEPTH = 2
HEAD_DIM = 128
N_GROUP_HEADS = 16
GROUP_WIDTH = N_GROUP_HEADS * HEAD_DIM
DILATED_GROUPS = ((128, 1), (512, 4), (2048, 16))
N_GROUPS = len(DILATED_GROUPS)
N_SELF = N_GROUPS * 3 * GROUP_WIDTH
BLOCK = 128
CONV_WIDTH = 3 * D_MODEL // 4
MEM_TOKENS = 256
MEM_HEADS = 4
MEM_HEAD_DIM = 256
MEM_WIDTH = MEM_HEADS * MEM_HEAD_DIM
D_FF = 11008
ROPE_THETA = 10000.0
EPS = 1e-6
NEG_INF = -1e30

V7X_SUBLANES = 8
V7X_LANES = 128
V7X_MXU_COLS = 256
V7X_VMEM_LIMIT_BYTES = 60 * 1024 * 1024

BF16 = jnp.bfloat16
F32 = jnp.float32

ROW_BLOCK = 1024
BIG_ROW_BLOCK = 2048
CHUNK_ROWS = 512
NORM_ROWS = 256
HALO = V7X_SUBLANES


def _params(n_axes):
    return pltpu.CompilerParams(
        dimension_semantics=("arbitrary",) * n_axes,
        vmem_limit_bytes=V7X_VMEM_LIMIT_BYTES,
    )


def _rms_scale(v):
    return lax.rsqrt(jnp.mean(v * v, axis=-1, keepdims=True) + EPS)


def _bdot(a, b):
    return jnp.dot(a, b, preferred_element_type=F32)


def _norm_kernel(x_ref, g_ref, o_ref):
    x = x_ref[...]
    o_ref[...] = (x * _rms_scale(x) * g_ref[...]).astype(o_ref.dtype)


def _rmsnorm_bf16(x, g_all, layer):
    s, d = x.shape
    return pl.pallas_call(
        _norm_kernel,
        grid=(s // NORM_ROWS,),
        in_specs=[
            pl.BlockSpec((NORM_ROWS, d), lambda i: (i, 0)),
            pl.BlockSpec((None, 1, d), lambda i: (layer, 0, 0)),
        ],
        out_specs=pl.BlockSpec((NORM_ROWS, d), lambda i: (i, 0)),
        out_shape=jax.ShapeDtypeStruct((s, d), BF16),
        compiler_params=_params(1),
        name="rmsnorm_bf16",
    )(x, g_all)


def _residual_kernel(y_ref, x_ref, gpost_ref, gnext_ref, xo_ref, xn_ref):
    y = y_ref[...].astype(F32)
    xnew = x_ref[...] + y * _rms_scale(y) * gpost_ref[...]
    xo_ref[...] = xnew
    xn_ref[...] = (xnew * _rms_scale(xnew) * gnext_ref[...]).astype(xn_ref.dtype)


def _row_permutation(rows, d, to_class_major):
    n = rows // d
    out_row = lax.broadcasted_iota(jnp.int32, (rows, rows), 0)
    in_row = lax.broadcasted_iota(jnp.int32, (rows, rows), 1)
    if to_class_major:
        src = (out_row & (n - 1)) * d + (out_row >> (n.bit_length() - 1))
    else:
        src = (out_row & (d - 1)) * n + (out_row >> (d.bit_length() - 1))
    return (in_row == src).astype(BF16)


def _residual_dilated_kernel(y_ref, x_ref, gpost_ref, gnext_ref, xo_ref, xn_ref, *xp_refs):
    y = y_ref[...].astype(F32)
    xnew = x_ref[...] + y * _rms_scale(y) * gpost_ref[...]
    xo_ref[...] = xnew
    xn = (xnew * _rms_scale(xnew) * gnext_ref[...]).astype(BF16)
    xn_ref[...] = xn
    for xp_ref in xp_refs:
        d, n = xp_ref.shape[0], xp_ref.shape[1]
        xp = _bdot(_row_permutation(NORM_ROWS, d, True), xn).astype(BF16)
        for r in range(d):
            xp_ref[r] = xp[r * n:(r + 1) * n, :]


def _residual_last_kernel(y_ref, x_ref, gpost_ref, xo_ref):
    y = y_ref[...].astype(F32)
    xo_ref[...] = x_ref[...] + y * _rms_scale(y) * gpost_ref[...]


def _residual_norm(y, x, g_post, post_layer, g_next=None, next_layer=None, dilations=()):
    s, d = x.shape
    row = pl.BlockSpec((NORM_ROWS, d), lambda i: (i, 0))
    gspec = lambda layer: pl.BlockSpec((None, 1, d), lambda i: (layer, 0, 0))
    if dilations:
        return pl.pallas_call(
            _residual_dilated_kernel,
            grid=(s // NORM_ROWS,),
            in_specs=[row, row, gspec(post_layer), gspec(next_layer)],
            out_specs=[row, row] + [
                pl.BlockSpec((dil, NORM_ROWS // dil, d), lambda i: (0, i, 0)) for dil in dilations],
            out_shape=[jax.ShapeDtypeStruct((s, d), F32), jax.ShapeDtypeStruct((s, d), BF16)] + [
                jax.ShapeDtypeStruct((dil, s // dil, d), BF16) for dil in dilations],
            compiler_params=_params(1),
            name="residual_norm_dilated",
        )(y, x, g_post, g_next)
    if g_next is None:
        return pl.pallas_call(
            _residual_last_kernel,
            grid=(s // NORM_ROWS,),
            in_specs=[row, row, gspec(post_layer)],
            out_specs=row,
            out_shape=jax.ShapeDtypeStruct((s, d), F32),
            compiler_params=_params(1),
            name="residual_last",
        )(y, x, g_post)
    return pl.pallas_call(
        _residual_kernel,
        grid=(s // NORM_ROWS,),
        in_specs=[row, row, gspec(post_layer), gspec(next_layer)],
        out_specs=[row, row],
        out_shape=[jax.ShapeDtypeStruct((s, d), F32), jax.ShapeDtypeStruct((s, d), BF16)],
        compiler_params=_params(1),
        name="residual_norm",
    )(y, x, g_post, g_next)


def _rope_table_kernel(pos_ref, freq_ref, sign_ref, cos_ref, sin_ref):
    ang = pos_ref[...].astype(F32) * freq_ref[...]
    cos_ref[...] = jnp.cos(ang)
    sin_ref[...] = jnp.sin(ang) * sign_ref[...]


def _rope_tables(positions):
    s = positions.shape[-1]
    half = HEAD_DIM // 2
    inv_freq = ROPE_THETA ** (-jnp.arange(half, dtype=F32) * 2.0 / HEAD_DIM)
    freq = jnp.concatenate([inv_freq, inv_freq]).reshape(1, HEAD_DIM)
    sign = jnp.concatenate([-jnp.ones((half,), F32), jnp.ones((half,), F32)]).reshape(1, HEAD_DIM)
    pos = jnp.stack([positions.reshape(s // d, d).T.reshape(s, 1) for _, d in DILATED_GROUPS])
    rows = 1024
    tab = pl.BlockSpec((None, rows, HEAD_DIM), lambda g, i: (g, i, 0))
    const = pl.BlockSpec((1, HEAD_DIM), lambda g, i: (0, 0))
    return pl.pallas_call(
        _rope_table_kernel,
        grid=(N_GROUPS, s // rows),
        in_specs=[pl.BlockSpec((None, rows, 1), lambda g, i: (g, i, 0)), const, const],
        out_specs=[tab, tab],
        out_shape=[jax.ShapeDtypeStruct((N_GROUPS, s, HEAD_DIM), F32)] * 2,
        compiler_params=_params(2),
        name="rope_tables",
    )(pos, freq, sign)


def _mem_kv_kernel(mem_ref, g_ref, w_ref, o_ref):
    m = mem_ref[...]
    mn = (m * _rms_scale(m) * g_ref[...]).astype(BF16)
    o_ref[...] = _bdot(mn, w_ref[...].astype(BF16)).astype(o_ref.dtype)


def _mem_kv(mem, g_mem, w_mem_kv):
    m, d = mem.shape
    n = w_mem_kv.shape[1]
    bn = 512
    return pl.pallas_call(
        _mem_kv_kernel,
        grid=(n // bn,),
        in_specs=[
            pl.BlockSpec((m, d), lambda j: (0, 0)),
            pl.BlockSpec((1, d), lambda j: (0, 0)),
            pl.BlockSpec((d, bn), lambda j: (0, j)),
        ],
        out_specs=pl.BlockSpec((m, bn), lambda j: (0, j)),
        out_shape=jax.ShapeDtypeStruct((m, n), BF16),
        compiler_params=_params(1),
        name="mem_kv",
    )(mem, g_mem.reshape(1, d), w_mem_kv)


def _causal_conv3(v, prev_tail, cw_ref, ext_ref):
    bm = v.shape[0]
    ext_ref[0:HALO, :] = prev_tail
    ext_ref[HALO:HALO + bm, :] = v
    v1 = ext_ref[HALO - 1:HALO - 1 + bm, :]
    v2 = ext_ref[HALO - 2:HALO - 2 + bm, :]
    conv = cw_ref[0:1, :] * v2 + cw_ref[1:2, :] * v1 + cw_ref[2:3, :] * v
    return conv, v[bm - HALO:bm, :]


def _row_chunks(total):
    return [slice(a, a + CHUNK_ROWS) for a in range(0, total, CHUNK_ROWS)]


def _resident_rows(bm, k, buffers=2):
    return pl.BlockSpec((bm, k), lambda i, j: (i, 0), pipeline_mode=pl.Buffered(buffers))


def _conv_in_kernel(x_ref, wb_ref, wc_ref, wu_ref, cw_ref, o_ref, ext0_ref, ext1_ref, tails_ref):
    i, j = pl.program_id(0), pl.program_id(1)
    tail = jnp.where(i == 0, 0.0, tails_ref[j])
    for chunk in range(x_ref.shape[0] // CHUNK_ROWS):
        ext_ref = (ext0_ref, ext1_ref)[chunk % 2]
        rows = slice(chunk * CHUNK_ROWS, (chunk + 1) * CHUNK_ROWS)
        x = x_ref[rows, :]
        b = _bdot(x, wb_ref[...].astype(BF16))
        c = _bdot(x, wc_ref[...].astype(BF16))
        u = _bdot(x, wu_ref[...].astype(BF16))
        conv, tail = _causal_conv3(c * u, tail, cw_ref, ext_ref)
        o_ref[rows, :] = (b * conv).astype(o_ref.dtype)
    tails_ref[j] = tail


def _conv_in(xn, w_in, conv_w, layer):
    s, d = xn.shape
    bm, bn = BIG_ROW_BLOCK, V7X_MXU_COLS
    nj = CONV_WIDTH // bn
    wspec = lambda off: pl.BlockSpec((None, d, bn), lambda i, j: (layer, 0, j + off))
    ext = pltpu.VMEM((CHUNK_ROWS + HALO, bn), F32)
    return pl.pallas_call(
        _conv_in_kernel,
        grid=(s // bm, nj),
        in_specs=[
            _resident_rows(bm, d, buffers=1),
            wspec(0), wspec(nj), wspec(2 * nj),
            pl.BlockSpec((None, 3, bn), lambda i, j: (layer, 0, j)),
        ],
        out_specs=pl.BlockSpec((bm, bn), lambda i, j: (i, j)),
        out_shape=jax.ShapeDtypeStruct((s, CONV_WIDTH), BF16),
        scratch_shapes=[ext, ext, pltpu.VMEM((nj, HALO, bn), F32)],
        compiler_params=_params(2),
        name="conv_in",
    )(xn, w_in, w_in, w_in, conv_w)


def _mem_q_kernel(x_ref, wq_ref, k_ref, v_ref, o_ref):
    q = _bdot(x_ref[...], wq_ref[...].astype(BF16)) * (MEM_HEAD_DIM ** -0.5)
    s = lax.dot_general(q.astype(BF16), k_ref[...], (((1,), (1,)), ((), ())),
                        preferred_element_type=F32)
    m = jnp.max(s, axis=-1, keepdims=True)
    p = jnp.exp(s - m)
    den = jnp.sum(p, axis=-1, keepdims=True)
    o = _bdot((p / den).astype(BF16), v_ref[...])
    o_ref[...] = o.astype(o_ref.dtype)


def _mem_attend(xn, w_in, layer, q_col_block, kv):
    s, d = xn.shape
    bm, bn = BIG_ROW_BLOCK, MEM_HEAD_DIM
    return pl.pallas_call(
        _mem_q_kernel,
        grid=(s // bm, MEM_HEADS),
        in_specs=[
            _resident_rows(bm, d),
            pl.BlockSpec((None, d, bn), lambda i, h: (layer, 0, q_col_block + h)),
            pl.BlockSpec((MEM_TOKENS, bn), lambda i, h: (0, h)),
            pl.BlockSpec((MEM_TOKENS, bn), lambda i, h: (0, MEM_HEADS + h)),
        ],
        out_specs=pl.BlockSpec((bm, bn), lambda i, h: (i, h)),
        out_shape=jax.ShapeDtypeStruct((s, MEM_WIDTH), BF16),
        compiler_params=_params(2),
        name="mem_attend",
    )(xn, w_in, kv, kv)


def _out_proj_kernel(a_ref, b_ref, wa_ref, wb_ref, o_ref):
    for chunk in range(a_ref.shape[0] // CHUNK_ROWS):
        rows = slice(chunk * CHUNK_ROWS, (chunk + 1) * CHUNK_ROWS)
        acc = _bdot(a_ref[rows, :], wa_ref[...].astype(BF16))
        acc = acc + _bdot(b_ref[rows, :], wb_ref[...].astype(BF16))
        o_ref[rows, :] = acc.astype(o_ref.dtype)


def _out_proj(a, b, w, layer):
    s, ka = a.shape
    kb = b.shape[1]
    n = w.shape[2]
    bm, bn = BIG_ROW_BLOCK, 512
    assert ka % kb == 0
    return pl.pallas_call(
        _out_proj_kernel,
        grid=(s // bm, n // bn),
        in_specs=[
            _resident_rows(bm, ka), _resident_rows(bm, kb),
            pl.BlockSpec((None, ka, bn), lambda i, j: (layer, 0, j)),
            pl.BlockSpec((None, kb, bn), lambda i, j: (layer, ka // kb, j)),
        ],
        out_specs=pl.BlockSpec((bm, bn), lambda i, j: (i, j)),
        out_shape=jax.ShapeDtypeStruct((s, n), BF16),
        compiler_params=_params(2),
        name="out_proj",
    )(a, b, w, w)


def _ffn_in_kernel(x_ref, wg_ref, wu_ref, cw_ref, o_ref, ext0_ref, ext1_ref, tails_ref):
    i, j = pl.program_id(0), pl.program_id(1)
    tail = jnp.where(i == 0, 0.0, tails_ref[j])
    for chunk, rows in enumerate(_row_chunks(x_ref.shape[0])):
        ext_ref = (ext0_ref, ext1_ref)[chunk % 2]
        x = x_ref[rows, :]
        g = _bdot(x, wg_ref[...].astype(BF16))
        u = _bdot(x, wu_ref[...].astype(BF16))
        gc, tail = _causal_conv3(g, tail, cw_ref, ext_ref)
        act = gc / (1.0 + jnp.exp(-gc))
        o_ref[rows, :] = (act * u).astype(o_ref.dtype)
    tails_ref[j] = tail


def _ffn_in(xn, w_gate, w_up, conv_w, layer):
    s, d = xn.shape
    bm, bn = BIG_ROW_BLOCK, V7X_MXU_COLS
    nj = D_FF // bn
    wspec = pl.BlockSpec((None, d, bn), lambda i, j: (layer, 0, j))
    ext = pltpu.VMEM((CHUNK_ROWS + HALO, bn), F32)
    return pl.pallas_call(
        _ffn_in_kernel,
        grid=(s // bm, nj),
        in_specs=[
            _resident_rows(bm, d),
            wspec, wspec,
            pl.BlockSpec((None, 3, bn), lambda i, j: (layer, 0, j)),
        ],
        out_specs=pl.BlockSpec((bm, bn), lambda i, j: (i, j)),
        out_shape=jax.ShapeDtypeStruct((s, D_FF), BF16),
        scratch_shapes=[ext, ext, pltpu.VMEM((nj, HALO, bn), F32)],
        compiler_params=_params(2),
        name="ffn_in",
    )(xn, w_gate, w_up, conv_w)


def _ffn_out_kernel(a_ref, w_ref, o_ref, acc_ref, *, kdim):
    k = pl.program_id(2)
    bk = a_ref.shape[1]

    @pl.when(k == 0)
    def _():
        acc_ref[...] = jnp.zeros_like(acc_ref)

    valid = kdim - k * bk
    a_cols = lax.broadcasted_iota(jnp.int32, a_ref.shape, 1)
    a = jnp.where(a_cols < valid, a_ref[...], jnp.zeros_like(a_ref))
    w_rows = lax.broadcasted_iota(jnp.int32, (bk, FFN_OUT_COL_CHUNK), 0)
    for c in range(0, o_ref.shape[1], FFN_OUT_COL_CHUNK):
        cols = slice(c, c + FFN_OUT_COL_CHUNK)
        w = jnp.where(w_rows < valid, w_ref[:, cols], 0.0).astype(BF16)
        for rows in _row_chunks(a_ref.shape[0]):
            acc_ref[rows, cols] += _bdot(a[rows, :], w)

    @pl.when(k == pl.num_programs(2) - 1)
    def _():
        o_ref[...] = acc_ref[...].astype(o_ref.dtype)


FFN_OUT_COL_CHUNK = 512
FFN_OUT_K_BLOCK = 512
FFN_OUT_COLS = 2048


def _ffn_out(a, w_down, layer):
    s, kdim = a.shape
    n = w_down.shape[2]
    bm, bk, bn = BIG_ROW_BLOCK, FFN_OUT_K_BLOCK, FFN_OUT_COLS
    return pl.pallas_call(
        functools.partial(_ffn_out_kernel, kdim=kdim),
        grid=(s // bm, n // bn, pl.cdiv(kdim, bk)),
        in_specs=[
            pl.BlockSpec((bm, bk), lambda i, j, k: (i, k)),
            pl.BlockSpec((None, bk, bn), lambda i, j, k: (layer, k, j)),
        ],
        out_specs=pl.BlockSpec((bm, bn), lambda i, j, k: (i, j)),
        out_shape=jax.ShapeDtypeStruct((s, n), BF16),
        scratch_shapes=[pltpu.VMEM((bm, bn), F32)],
        compiler_params=_params(3),
        name="ffn_out",
    )(a, w_down)


ATTN_IN_COLS = 512
HEADS_PER_ATTN_IN_BLOCK = ATTN_IN_COLS // HEAD_DIM
ATTN_IN_BLOCKS_PER_SECTION = GROUP_WIDTH // ATTN_IN_COLS
ATTN_IN_BLOCKS_PER_GROUP = 3 * ATTN_IN_BLOCKS_PER_SECTION
ATTN_IN_CHUNK_ROWS = 512


def _attn_in_kernel(x_ref, w_ref, cos_ref, sin_ref, o_ref):
    section = pl.program_id(1) // ATTN_IN_BLOCKS_PER_SECTION
    scale = jnp.where(section == 0, HEAD_DIM ** -0.5, 1.0).astype(F32)
    is_v = section == 2
    for chunk in range(x_ref.shape[0] // ATTN_IN_CHUNK_ROWS):
        rows = slice(chunk * ATTN_IN_CHUNK_ROWS, (chunk + 1) * ATTN_IN_CHUNK_ROWS)
        acc = _bdot(x_ref[rows, :], w_ref[...].astype(BF16))
        cos = cos_ref[rows, :] * scale
        sin = sin_ref[rows, :] * scale
        for h in range(HEADS_PER_ATTN_IN_BLOCK):
            cols = slice(h * HEAD_DIM, (h + 1) * HEAD_DIM)
            v = acc[:, cols]
            rot = pltpu.roll(v, HEAD_DIM // 2, axis=1)
            o_ref[rows, cols] = jnp.where(is_v, v, v * cos + rot * sin).astype(o_ref.dtype)


def _attn_in(xn, w_in, cos_tab, sin_tab, layer, group):
    s, d = xn.shape
    bm, bn = BIG_ROW_BLOCK, ATTN_IN_COLS
    nj = ATTN_IN_BLOCKS_PER_GROUP
    tab = pl.BlockSpec((None, bm, HEAD_DIM), lambda i, j: (group, i, 0))
    return pl.pallas_call(
        _attn_in_kernel,
        grid=(s // bm, nj),
        in_specs=[
            _resident_rows(bm, d),
            pl.BlockSpec((None, d, bn), lambda i, j: (layer, 0, group * nj + j)),
            tab, tab,
        ],
        out_specs=pl.BlockSpec((bm, bn), lambda i, j: (i, j)),
        out_shape=jax.ShapeDtypeStruct((s, 3 * GROUP_WIDTH), BF16),
        compiler_params=_params(2),
        name=f"attn_in_g{group}",
    )(xn, w_in, cos_tab, sin_tab)


def _window_attn_kernel(q_ref, k_ref, v_ref, o_ref, lse_ref, k2_ref, v2_ref):
    nb = pl.program_id(1)

    @pl.when(nb == 0)
    def _():
        k2_ref[...] = jnp.zeros_like(k2_ref)
        v2_ref[...] = jnp.zeros_like(v2_ref)

    own_half = nb % 2
    own_rows = pl.ds(pl.multiple_of(own_half * BLOCK, BLOCK), BLOCK)
    k2_ref[own_rows, :] = k_ref[...]
    v2_ref[own_rows, :] = v_ref[...]

    qi = lax.broadcasted_iota(jnp.int32, (BLOCK, 2 * BLOCK), 0)
    col = lax.broadcasted_iota(jnp.int32, (BLOCK, 2 * BLOCK), 1)
    ki = col & (BLOCK - 1)
    in_own_half = (col >> (BLOCK.bit_length() - 1)) == own_half
    valid = jnp.logical_or(
        jnp.logical_and(in_own_half, ki <= qi),
        jnp.logical_and(jnp.logical_not(in_own_half), jnp.logical_and(ki >= qi, nb > 0)))
    lane = lax.broadcasted_iota(jnp.int32, (BLOCK, V7X_LANES), 1)
    ones = jnp.ones((2 * BLOCK, HEAD_DIM), BF16)
    lse_all = jnp.zeros((BLOCK, V7X_LANES), F32)
    nt = (((1,), (1,)), ((), ()))
    for h in range(N_GROUP_HEADS):
        cols = slice(h * HEAD_DIM, (h + 1) * HEAD_DIM)
        s = lax.dot_general(q_ref[:, cols], k2_ref[:, cols], nt, preferred_element_type=F32)
        s = jnp.where(valid, s, NEG_INF)
        m = jnp.max(s, axis=-1, keepdims=True)
        p = jnp.exp(s - m).astype(BF16)
        ov = _bdot(p, jnp.concatenate([v2_ref[:, cols], ones], axis=1))
        den = ov[:, HEAD_DIM:]
        o_ref[:, cols] = (ov[:, :HEAD_DIM] / den).astype(o_ref.dtype)
        lse_all = jnp.where(lane == h, m + jnp.log(den), lse_all)
    lse_ref[...] = lse_all


def _window_attn(h_group, group):
    s = h_group.shape[0]
    _, d = DILATED_GROUPS[group]
    blocks_per_class = s // d // BLOCK
    spec = lambda sec: pl.BlockSpec(
        (BLOCK, GROUP_WIDTH), lambda r, nb: (r * blocks_per_class + nb, sec))
    return pl.pallas_call(
        _window_attn_kernel,
        grid=(d, blocks_per_class),
        in_specs=[spec(0), spec(1), spec(2)],
        out_specs=[
            pl.BlockSpec((BLOCK, GROUP_WIDTH), lambda r, nb: (r * blocks_per_class + nb, 0)),
            pl.BlockSpec((BLOCK, V7X_LANES), lambda r, nb: (r * blocks_per_class + nb, 0)),
        ],
        out_shape=[
            jax.ShapeDtypeStruct((s, GROUP_WIDTH), BF16),
            jax.ShapeDtypeStruct((s, V7X_LANES), F32),
        ],
        scratch_shapes=[pltpu.VMEM((2 * BLOCK, GROUP_WIDTH), BF16)] * 2,
        compiler_params=_params(2),
        name=f"window_attn_g{group}",
    )(h_group, h_group, h_group)


def _merge_kernel(o0_ref, o1_ref, o2_ref, l0_ref, l1_ref, l2_ref, out_ref):
    outs = [o0_ref[...].astype(F32)]
    lses = [l0_ref[...]]
    for o_ref, l_ref in ((o1_ref, l1_ref), (o2_ref, l2_ref)):
        d = o_ref.shape[0]
        perm = _row_permutation(NORM_ROWS, d, False)
        o_cm = jnp.concatenate([o_ref[r] for r in range(d)], axis=0)
        l_cm = jnp.concatenate([l_ref[r] for r in range(d)], axis=0)
        outs.append(_bdot(perm, o_cm))
        hi = l_cm.astype(BF16)
        rest = l_cm - hi.astype(F32)
        mid = rest.astype(BF16)
        lo = (rest - mid.astype(F32)).astype(BF16)
        lses.append(_bdot(perm, hi) + _bdot(perm, mid) + _bdot(perm, lo))
    l0, l1, l2 = lses
    m = jnp.maximum(jnp.maximum(l0, l1), l2)
    e0, e1, e2 = jnp.exp(l0 - m), jnp.exp(l1 - m), jnp.exp(l2 - m)
    tot = e0 + e1 + e2
    a0, a1, a2 = e0 / tot, e1 / tot, e2 / tot
    for h in range(N_GROUP_HEADS):
        cols = slice(h * HEAD_DIM, (h + 1) * HEAD_DIM)
        merged = (a0[:, h:h + 1] * outs[0][:, cols] + a1[:, h:h + 1] * outs[1][:, cols]
                  + a2[:, h:h + 1] * outs[2][:, cols])
        out_ref[:, cols] = merged.astype(out_ref.dtype)


def _merge_groups(outs, lses):
    s = outs[0].shape[0]
    rows = NORM_ROWS

    def specs(width):
        out = [pl.BlockSpec((rows, width), lambda i: (i, 0))]
        for _, d in DILATED_GROUPS[1:]:
            out.append(pl.BlockSpec((d, rows // d, width), lambda i: (0, i, 0)))
        return out

    class_major = lambda a, d: a.reshape(d, s // d, a.shape[1])
    outs = [outs[0]] + [class_major(o, d) for o, (_, d) in zip(outs[1:], DILATED_GROUPS[1:])]
    lses = [lses[0]] + [class_major(l, d) for l, (_, d) in zip(lses[1:], DILATED_GROUPS[1:])]
    return pl.pallas_call(
        _merge_kernel,
        grid=(s // rows,),
        in_specs=specs(GROUP_WIDTH) + specs(V7X_LANES),
        out_specs=pl.BlockSpec((rows, GROUP_WIDTH), lambda i: (i, 0)),
        out_shape=jax.ShapeDtypeStruct((s, GROUP_WIDTH), BF16),
        compiler_params=_params(1),
        name="merge_groups",
    )(*outs, *lses)


def kernel(x, mem, positions, g_mem, w_mem_kv, g_mix_pre, g_mix_post, g_ffn_pre, g_ffn_post,
           w_conv_in, conv_mix_w, w_conv_out, w_attn_in, w_attn_out,
           w_ffn_gate, w_ffn_up, conv_ffn_w, w_ffn_down):
    b, s, d = x.shape
    assert b == 1 and s == SEQ and d == D_MODEL
    xs = x.reshape(s, d)
    gains = lambda g: g.reshape(DEPTH, 1, d)
    g_mix_pre, g_mix_post = gains(g_mix_pre), gains(g_mix_post)
    g_ffn_pre, g_ffn_post = gains(g_ffn_pre), gains(g_ffn_post)

    kv = _mem_kv(mem.reshape(MEM_TOKENS, d), g_mem, w_mem_kv)
    cos_tab, sin_tab = _rope_tables(positions)

    dilations = tuple(dil for _, dil in DILATED_GROUPS[1:])
    xn = _rmsnorm_bf16(xs, g_mix_pre, 0)
    xn_class_major = ()
    for i in range(DEPTH):
        j = i // 2
        if i % 2 == 0:
            y_mix = _conv_in(xn, w_conv_in, conv_mix_w, j)
            y_mem = _mem_attend(xn, w_conv_in, j, 3 * CONV_WIDTH // MEM_HEAD_DIM, kv)
            y = _out_proj(y_mix, y_mem, w_conv_out, j)
        else:
            lhs = [xn] + [xp.reshape(s, d) for xp in xn_class_major]
            parts = [
                _window_attn(_attn_in(lhs[g], w_attn_in, cos_tab, sin_tab, j, g), g)
                for g in range(N_GROUPS)]
            y_mix = _merge_groups([p[0] for p in parts], [p[1] for p in parts])
            y_mem = _mem_attend(xn, w_attn_in, j, N_SELF // MEM_HEAD_DIM, kv)
            y = _out_proj(y_mix, y_mem, w_attn_out, j)
        xs, xn = _residual_norm(y, xs, g_mix_post, i, g_ffn_pre, i)
        a = _ffn_in(xn, w_ffn_gate, w_ffn_up, conv_ffn_w, i)
        y = _ffn_out(a, w_ffn_down, i)
        if i + 1 == DEPTH:
            xs = _residual_norm(y, xs, g_ffn_post, i)
        elif (i + 1) % 2 == 1:
            xs, xn, *xn_class_major = _residual_norm(
                y, xs, g_ffn_post, i, g_mix_pre, i + 1, dilations=dilations)
        else:
            xs, xn = _residual_norm(y, xs, g_ffn_post, i, g_mix_pre, i + 1)
    return xs.reshape(b, s, d)
```

```python
import functools

import jax
import jax.numpy as jnp
from jax import lax
from jax.experimental import pallas as pl
from jax.experimental.pallas import tpu as pltpu

D_MODEL = 4096
SEQ = 8192
DEPTH = 2
HEAD_DIM = 128
N_GROUP_HEADS = 16
GROUP_WIDTH = N_GROUP_HEADS * HEAD_DIM
DILATED_GROUPS = ((128, 1), (512, 4), (2048, 16))
N_GROUPS = len(DILATED_GROUPS)
N_SELF = N_GROUPS * 3 * GROUP_WIDTH
BLOCK = 128
CONV_WIDTH = 3 * D_MODEL // 4
MEM_TOKENS = 256
MEM_HEADS = 4
MEM_HEAD_DIM = 256
MEM_WIDTH = MEM_HEADS * MEM_HEAD_DIM
D_FF = 11008
ROPE_THETA = 10000.0
EPS = 1e-6
NEG_INF = -1e30

V7X_SUBLANES = 8
V7X_LANES = 128
V7X_MXU_COLS = 256
V7X_VMEM_LIMIT_BYTES = 60 * 1024 * 1024

BF16 = jnp.bfloat16
F32 = jnp.float32

ROW_BLOCK = 1024
BIG_ROW_BLOCK = 2048
CHUNK_ROWS = 512
NORM_ROWS = 512
PERM_ROWS = 256
HALO = V7X_SUBLANES


def _params(n_axes):
    return pltpu.CompilerParams(
        dimension_semantics=("arbitrary",) * n_axes,
        vmem_limit_bytes=V7X_VMEM_LIMIT_BYTES,
    )


def _rms_scale(v):
    return lax.rsqrt(jnp.mean(v * v, axis=-1, keepdims=True) + EPS)


def _bdot(a, b):
    return jnp.dot(a, b, preferred_element_type=F32)


def _norm_kernel(x_ref, g_ref, o_ref):
    x = x_ref[...]
    o_ref[...] = (x * _rms_scale(x) * g_ref[...]).astype(o_ref.dtype)


def _rmsnorm_bf16(x, g_all, layer):
    s, d = x.shape
    return pl.pallas_call(
        _norm_kernel,
        grid=(s // NORM_ROWS,),
        in_specs=[
            pl.BlockSpec((NORM_ROWS, d), lambda i: (i, 0)),
            pl.BlockSpec((None, 1, d), lambda i: (layer, 0, 0)),
        ],
        out_specs=pl.BlockSpec((NORM_ROWS, d), lambda i: (i, 0)),
        out_shape=jax.ShapeDtypeStruct((s, d), BF16),
        compiler_params=_params(1),
        name="rmsnorm_bf16",
    )(x, g_all)


def _residual_kernel(y_ref, x_ref, gpost_ref, gnext_ref, xo_ref, xn_ref):
    y = y_ref[...].astype(F32)
    xnew = x_ref[...] + y * _rms_scale(y) * gpost_ref[...]
    xo_ref[...] = xnew
    xn_ref[...] = (xnew * _rms_scale(xnew) * gnext_ref[...]).astype(xn_ref.dtype)


def _row_permutation(rows, d, to_class_major):
    n = rows // d
    out_row = lax.broadcasted_iota(jnp.int32, (rows, rows), 0)
    in_row = lax.broadcasted_iota(jnp.int32, (rows, rows), 1)
    if to_class_major:
        src = (out_row & (n - 1)) * d + (out_row >> (n.bit_length() - 1))
    else:
        src = (out_row & (d - 1)) * n + (out_row >> (d.bit_length() - 1))
    return (in_row == src).astype(BF16)


def _residual_dilated_kernel(y_ref, x_ref, gpost_ref, gnext_ref, xo_ref, xn_ref, *xp_refs):
    y = y_ref[...].astype(F32)
    xnew = x_ref[...] + y * _rms_scale(y) * gpost_ref[...]
    xo_ref[...] = xnew
    xn = (xnew * _rms_scale(xnew) * gnext_ref[...]).astype(BF16)
    xn_ref[...] = xn
    for xp_ref in xp_refs:
        d, n = xp_ref.shape[0], xp_ref.shape[1]
        xp = _bdot(_row_permutation(PERM_ROWS, d, True), xn).astype(BF16)
        for r in range(d):
            xp_ref[r] = xp[r * n:(r + 1) * n, :]


def _residual_last_kernel(y_ref, x_ref, gpost_ref, xo_ref):
    y = y_ref[...].astype(F32)
    xo_ref[...] = x_ref[...] + y * _rms_scale(y) * gpost_ref[...]


def _residual_norm(y, x, g_post, post_layer, g_next=None, next_layer=None, dilations=()):
    s, d = x.shape
    row = pl.BlockSpec((NORM_ROWS, d), lambda i: (i, 0))
    gspec = lambda layer: pl.BlockSpec((None, 1, d), lambda i: (layer, 0, 0))
    if dilations:
        row = pl.BlockSpec((PERM_ROWS, d), lambda i: (i, 0))
        return pl.pallas_call(
            _residual_dilated_kernel,
            grid=(s // PERM_ROWS,),
            in_specs=[row, row, gspec(post_layer), gspec(next_layer)],
            out_specs=[row, row] + [
                pl.BlockSpec((dil, PERM_ROWS // dil, d), lambda i: (0, i, 0)) for dil in dilations],
            out_shape=[jax.ShapeDtypeStruct((s, d), F32), jax.ShapeDtypeStruct((s, d), BF16)] + [
                jax.ShapeDtypeStruct((dil, s // dil, d), BF16) for dil in dilations],
            compiler_params=_params(1),
            name="residual_norm_dilated",
        )(y, x, g_post, g_next)
    if g_next is None:
        return pl.pallas_call(
            _residual_last_kernel,
            grid=(s // NORM_ROWS,),
            in_specs=[row, row, gspec(post_layer)],
            out_specs=row,
            out_shape=jax.ShapeDtypeStruct((s, d), F32),
            compiler_params=_params(1),
            name="residual_last",
        )(y, x, g_post)
    return pl.pallas_call(
        _residual_kernel,
        grid=(s // NORM_ROWS,),
        in_specs=[row, row, gspec(post_layer), gspec(next_layer)],
        out_specs=[row, row],
        out_shape=[jax.ShapeDtypeStruct((s, d), F32), jax.ShapeDtypeStruct((s, d), BF16)],
        compiler_params=_params(1),
        name="residual_norm",
    )(y, x, g_post, g_next)


def _rope_table_kernel(pos_ref, freq_ref, sign_ref, cos_ref, sin_ref):
    ang = pos_ref[...].astype(F32) * freq_ref[...]
    cos_ref[...] = jnp.cos(ang)
    sin_ref[...] = jnp.sin(ang) * sign_ref[...]


def _rope_tables(positions):
    s = positions.shape[-1]
    half = HEAD_DIM // 2
    inv_freq = ROPE_THETA ** (-jnp.arange(half, dtype=F32) * 2.0 / HEAD_DIM)
    freq = jnp.concatenate([inv_freq, inv_freq]).reshape(1, HEAD_DIM)
    sign = jnp.concatenate([-jnp.ones((half,), F32), jnp.ones((half,), F32)]).reshape(1, HEAD_DIM)
    pos = jnp.stack([positions.reshape(s // d, d).T.reshape(s, 1) for _, d in DILATED_GROUPS])
    rows = 1024
    tab = pl.BlockSpec((None, rows, HEAD_DIM), lambda g, i: (g, i, 0))
    const = pl.BlockSpec((1, HEAD_DIM), lambda g, i: (0, 0))
    return pl.pallas_call(
        _rope_table_kernel,
        grid=(N_GROUPS, s // rows),
        in_specs=[pl.BlockSpec((None, rows, 1), lambda g, i: (g, i, 0)), const, const],
        out_specs=[tab, tab],
        out_shape=[jax.ShapeDtypeStruct((N_GROUPS, s, HEAD_DIM), F32)] * 2,
        compiler_params=_params(2),
        name="rope_tables",
    )(pos, freq, sign)


def _mem_kv_kernel(mem_ref, g_ref, w_ref, o_ref):
    m = mem_ref[...]
    mn = (m * _rms_scale(m) * g_ref[...]).astype(BF16)
    o_ref[...] = _bdot(mn, w_ref[...].astype(BF16)).astype(o_ref.dtype)


def _mem_kv(mem, g_mem, w_mem_kv):
    m, d = mem.shape
    n = w_mem_kv.shape[1]
    bn = 512
    return pl.pallas_call(
        _mem_kv_kernel,
        grid=(n // bn,),
        in_specs=[
            pl.BlockSpec((m, d), lambda j: (0, 0)),
            pl.BlockSpec((1, d), lambda j: (0, 0)),
            pl.BlockSpec((d, bn), lambda j: (0, j)),
        ],
        out_specs=pl.BlockSpec((m, bn), lambda j: (0, j)),
        out_shape=jax.ShapeDtypeStruct((m, n), BF16),
        compiler_params=_params(1),
        name="mem_kv",
    )(mem, g_mem.reshape(1, d), w_mem_kv)


def _causal_conv3(v, prev_tail, cw_ref, ext_ref):
    bm = v.shape[0]
    ext_ref[0:HALO, :] = prev_tail
    ext_ref[HALO:HALO + bm, :] = v
    v1 = ext_ref[HALO - 1:HALO - 1 + bm, :]
    v2 = ext_ref[HALO - 2:HALO - 2 + bm, :]
    conv = cw_ref[0:1, :] * v2 + cw_ref[1:2, :] * v1 + cw_ref[2:3, :] * v
    return conv, v[bm - HALO:bm, :]


def _row_chunks(total):
    return [slice(a, a + CHUNK_ROWS) for a in range(0, total, CHUNK_ROWS)]


def _resident_rows(bm, k, buffers=2):
    return pl.BlockSpec((bm, k), lambda i, j: (i, 0), pipeline_mode=pl.Buffered(buffers))


def _conv_in_kernel(x_ref, wb_ref, wc_ref, wu_ref, cw_ref, o_ref, ext0_ref, ext1_ref, tails_ref):
    i, j = pl.program_id(0), pl.program_id(1)
    tail = jnp.where(i == 0, 0.0, tails_ref[j])
    for chunk in range(x_ref.shape[0] // CHUNK_ROWS):
        ext_ref = (ext0_ref, ext1_ref)[chunk % 2]
        rows = slice(chunk * CHUNK_ROWS, (chunk + 1) * CHUNK_ROWS)
        x = x_ref[rows, :]
        b = _bdot(x, wb_ref[...].astype(BF16))
        c = _bdot(x, wc_ref[...].astype(BF16))
        u = _bdot(x, wu_ref[...].astype(BF16))
        conv, tail = _causal_conv3(c * u, tail, cw_ref, ext_ref)
        o_ref[rows, :] = (b * conv).astype(o_ref.dtype)
    tails_ref[j] = tail


def _conv_in(xn, w_in, conv_w, layer):
    s, d = xn.shape
    bm, bn = BIG_ROW_BLOCK, V7X_MXU_COLS
    nj = CONV_WIDTH // bn
    wspec = lambda off: pl.BlockSpec((None, d, bn), lambda i, j: (layer, 0, j + off))
    ext = pltpu.VMEM((CHUNK_ROWS + HALO, bn), F32)
    return pl.pallas_call(
        _conv_in_kernel,
        grid=(s // bm, nj),
        in_specs=[
            _resident_rows(bm, d, buffers=1),
            wspec(0), wspec(nj), wspec(2 * nj),
            pl.BlockSpec((None, 3, bn), lambda i, j: (layer, 0, j)),
        ],
        out_specs=pl.BlockSpec((bm, bn), lambda i, j: (i, j)),
        out_shape=jax.ShapeDtypeStruct((s, CONV_WIDTH), BF16),
        scratch_shapes=[ext, ext, pltpu.VMEM((nj, HALO, bn), F32)],
        compiler_params=_params(2),
        name="conv_in",
    )(xn, w_in, w_in, w_in, conv_w)


def _mem_q_kernel(x_ref, wq_ref, k_ref, v_ref, o_ref):
    q = _bdot(x_ref[...], wq_ref[...].astype(BF16)) * (MEM_HEAD_DIM ** -0.5)
    s = lax.dot_general(q.astype(BF16), k_ref[...], (((1,), (1,)), ((), ())),
                        preferred_element_type=F32)
    m = jnp.max(s, axis=-1, keepdims=True)
    p = jnp.exp(s - m)
    den = jnp.sum(p, axis=-1, keepdims=True)
    o = _bdot((p / den).astype(BF16), v_ref[...])
    o_ref[...] = o.astype(o_ref.dtype)


def _mem_attend(xn, w_in, layer, q_col_block, kv):
    s, d = xn.shape
    bm, bn = BIG_ROW_BLOCK, MEM_HEAD_DIM
    return pl.pallas_call(
        _mem_q_kernel,
        grid=(s // bm, MEM_HEADS),
        in_specs=[
            _resident_rows(bm, d),
            pl.BlockSpec((None, d, bn), lambda i, h: (layer, 0, q_col_block + h)),
            pl.BlockSpec((MEM_TOKENS, bn), lambda i, h: (0, h)),
            pl.BlockSpec((MEM_TOKENS, bn), lambda i, h: (0, MEM_HEADS + h)),
        ],
        out_specs=pl.BlockSpec((bm, bn), lambda i, h: (i, h)),
        out_shape=jax.ShapeDtypeStruct((s, MEM_WIDTH), BF16),
        compiler_params=_params(2),
        name="mem_attend",
    )(xn, w_in, kv, kv)


def _out_proj_kernel(a_ref, b_ref, wa_ref, wb_ref, o_ref):
    for chunk in range(a_ref.shape[0] // CHUNK_ROWS):
        rows = slice(chunk * CHUNK_ROWS, (chunk + 1) * CHUNK_ROWS)
        acc = _bdot(a_ref[rows, :], wa_ref[...].astype(BF16))
        acc = acc + _bdot(b_ref[rows, :], wb_ref[...].astype(BF16))
        o_ref[rows, :] = acc.astype(o_ref.dtype)


def _out_proj(a, b, w, layer):
    s, ka = a.shape
    kb = b.shape[1]
    n = w.shape[2]
    bm, bn = BIG_ROW_BLOCK, 512
    assert ka % kb == 0
    return pl.pallas_call(
        _out_proj_kernel,
        grid=(s // bm, n // bn),
        in_specs=[
            _resident_rows(bm, ka), _resident_rows(bm, kb),
            pl.BlockSpec((None, ka, bn), lambda i, j: (layer, 0, j)),
            pl.BlockSpec((None, kb, bn), lambda i, j: (layer, ka // kb, j)),
        ],
        out_specs=pl.BlockSpec((bm, bn), lambda i, j: (i, j)),
        out_shape=jax.ShapeDtypeStruct((s, n), BF16),
        compiler_params=_params(2),
        name="out_proj",
    )(a, b, w, w)


def _ffn_in_kernel(x_ref, wg_ref, wu_ref, cw_ref, o_ref, ext0_ref, ext1_ref, tails_ref):
    i, j = pl.program_id(0), pl.program_id(1)
    tail = jnp.where(i == 0, 0.0, tails_ref[j])
    for chunk, rows in enumerate(_row_chunks(x_ref.shape[0])):
        ext_ref = (ext0_ref, ext1_ref)[chunk % 2]
        x = x_ref[rows, :]
        g = _bdot(x, wg_ref[...].astype(BF16))
        u = _bdot(x, wu_ref[...].astype(BF16))
        gc, tail = _causal_conv3(g, tail, cw_ref, ext_ref)
        act = gc / (1.0 + jnp.exp(-gc))
        o_ref[rows, :] = (act * u).astype(o_ref.dtype)
    tails_ref[j] = tail


def _ffn_in(xn, w_gate, w_up, conv_w, layer):
    s, d = xn.shape
    bm, bn = BIG_ROW_BLOCK, V7X_MXU_COLS
    nj = D_FF // bn
    wspec = pl.BlockSpec((None, d, bn), lambda i, j: (layer, 0, j))
    ext = pltpu.VMEM((CHUNK_ROWS + HALO, bn), F32)
    return pl.pallas_call(
        _ffn_in_kernel,
        grid=(s // bm, nj),
        in_specs=[
            _resident_rows(bm, d),
            wspec, wspec,
            pl.BlockSpec((None, 3, bn), lambda i, j: (layer, 0, j)),
        ],
        out_specs=pl.BlockSpec((bm, bn), lambda i, j: (i, j)),
        out_shape=jax.ShapeDtypeStruct((s, D_FF), BF16),
        scratch_shapes=[ext, ext, pltpu.VMEM((nj, HALO, bn), F32)],
        compiler_params=_params(2),
        name="ffn_in",
    )(xn, w_gate, w_up, conv_w)


def _ffn_out_kernel(a_ref, w_ref, o_ref, acc_ref, *, kdim):
    k = pl.program_id(2)
    bk = a_ref.shape[1]

    @pl.when(k == 0)
    def _():
        acc_ref[...] = jnp.zeros_like(acc_ref)

    valid = kdim - k * bk
    a_cols = lax.broadcasted_iota(jnp.int32, a_ref.shape, 1)
    a = jnp.where(a_cols < valid, a_ref[...], jnp.zeros_like(a_ref))
    w_rows = lax.broadcasted_iota(jnp.int32, (bk, FFN_OUT_COL_CHUNK), 0)
    for c in range(0, o_ref.shape[1], FFN_OUT_COL_CHUNK):
        cols = slice(c, c + FFN_OUT_COL_CHUNK)
        w = jnp.where(w_rows < valid, w_ref[:, cols], 0.0).astype(BF16)
        for rows in _row_chunks(a_ref.shape[0]):
            acc_ref[rows, cols] += _bdot(a[rows, :], w)

    @pl.when(k == pl.num_programs(2) - 1)
    def _():
        o_ref[...] = acc_ref[...].astype(o_ref.dtype)


FFN_OUT_COL_CHUNK = 512
FFN_OUT_K_BLOCK = 512
FFN_OUT_COLS = 2048


def _ffn_out(a, w_down, layer):
    s, kdim = a.shape
    n = w_down.shape[2]
    bm, bk, bn = BIG_ROW_BLOCK, FFN_OUT_K_BLOCK, FFN_OUT_COLS
    return pl.pallas_call(
        functools.partial(_ffn_out_kernel, kdim=kdim),
        grid=(s // bm, n // bn, pl.cdiv(kdim, bk)),
        in_specs=[
            pl.BlockSpec((bm, bk), lambda i, j, k: (i, k)),
            pl.BlockSpec((None, bk, bn), lambda i, j, k: (layer, k, j)),
        ],
        out_specs=pl.BlockSpec((bm, bn), lambda i, j, k: (i, j)),
        out_shape=jax.ShapeDtypeStruct((s, n), BF16),
        scratch_shapes=[pltpu.VMEM((bm, bn), F32)],
        compiler_params=_params(3),
        name="ffn_out",
    )(a, w_down)


ATTN_IN_COLS = 512
HEADS_PER_ATTN_IN_BLOCK = ATTN_IN_COLS // HEAD_DIM
ATTN_IN_BLOCKS_PER_SECTION = GROUP_WIDTH // ATTN_IN_COLS
ATTN_IN_BLOCKS_PER_GROUP = 3 * ATTN_IN_BLOCKS_PER_SECTION
ATTN_IN_CHUNK_ROWS = 512


def _attn_in_kernel(x_ref, w_ref, cos_ref, sin_ref, o_ref):
    section = pl.program_id(1) // ATTN_IN_BLOCKS_PER_SECTION
    scale = jnp.where(section == 0, HEAD_DIM ** -0.5, 1.0).astype(F32)
    is_v = section == 2
    for chunk in range(x_ref.shape[0] // ATTN_IN_CHUNK_ROWS):
        rows = slice(chunk * ATTN_IN_CHUNK_ROWS, (chunk + 1) * ATTN_IN_CHUNK_ROWS)
        acc = _bdot(x_ref[rows, :], w_ref[...].astype(BF16))
        cos = cos_ref[rows, :] * scale
        sin = sin_ref[rows, :] * scale
        for h in range(HEADS_PER_ATTN_IN_BLOCK):
            cols = slice(h * HEAD_DIM, (h + 1) * HEAD_DIM)
            v = acc[:, cols]
            rot = pltpu.roll(v, HEAD_DIM // 2, axis=1)
            o_ref[rows, cols] = jnp.where(is_v, v, v * cos + rot * sin).astype(o_ref.dtype)


def _attn_in(xn, w_in, cos_tab, sin_tab, layer, group):
    s, d = xn.shape
    bm, bn = BIG_ROW_BLOCK, ATTN_IN_COLS
    nj = ATTN_IN_BLOCKS_PER_GROUP
    tab = pl.BlockSpec((None, bm, HEAD_DIM), lambda i, j: (group, i, 0))
    return pl.pallas_call(
        _attn_in_kernel,
        grid=(s // bm, nj),
        in_specs=[
            _resident_rows(bm, d),
            pl.BlockSpec((None, d, bn), lambda i, j: (layer, 0, group * nj + j)),
            tab, tab,
        ],
        out_specs=pl.BlockSpec((bm, bn), lambda i, j: (i, j)),
        out_shape=jax.ShapeDtypeStruct((s, 3 * GROUP_WIDTH), BF16),
        compiler_params=_params(2),
        name=f"attn_in_g{group}",
    )(xn, w_in, cos_tab, sin_tab)


def _window_attn_kernel(q_ref, k_ref, v_ref, o_ref, lse_ref, k2_ref, v2_ref):
    nb = pl.program_id(1)

    @pl.when(nb == 0)
    def _():
        k2_ref[...] = jnp.zeros_like(k2_ref)
        v2_ref[...] = jnp.zeros_like(v2_ref)

    own_half = nb % 2
    own_rows = pl.ds(pl.multiple_of(own_half * BLOCK, BLOCK), BLOCK)
    k2_ref[own_rows, :] = k_ref[...]
    v2_ref[own_rows, :] = v_ref[...]

    qi = lax.broadcasted_iota(jnp.int32, (BLOCK, 2 * BLOCK), 0)
    col = lax.broadcasted_iota(jnp.int32, (BLOCK, 2 * BLOCK), 1)
    ki = col & (BLOCK - 1)
    in_own_half = (col >> (BLOCK.bit_length() - 1)) == own_half
    valid = jnp.logical_or(
        jnp.logical_and(in_own_half, ki <= qi),
        jnp.logical_and(jnp.logical_not(in_own_half), jnp.logical_and(ki >= qi, nb > 0)))
    lane = lax.broadcasted_iota(jnp.int32, (BLOCK, V7X_LANES), 1)
    ones = jnp.ones((2 * BLOCK, HEAD_DIM), BF16)
    lse_all = jnp.zeros((BLOCK, V7X_LANES), F32)
    nt = (((1,), (1,)), ((), ()))
    head_cols = [slice(h * HEAD_DIM, (h + 1) * HEAD_DIM) for h in range(N_GROUP_HEADS)]
    scores = [lax.dot_general(q_ref[:, c], k2_ref[:, c], nt, preferred_element_type=F32)
              for c in head_cols]
    scores = [jnp.where(valid, s, NEG_INF) for s in scores]
    maxes = [jnp.max(s, axis=-1, keepdims=True) for s in scores]
    probs = [jnp.exp(s - m).astype(BF16) for s, m in zip(scores, maxes)]
    ovs = [_bdot(p, jnp.concatenate([v2_ref[:, c], ones], axis=1)) for p, c in zip(probs, head_cols)]
    for h, (ov, m, c) in enumerate(zip(ovs, maxes, head_cols)):
        den = ov[:, HEAD_DIM:]
        o_ref[:, c] = (ov[:, :HEAD_DIM] / den).astype(o_ref.dtype)
        lse_all = jnp.where(lane == h, m + jnp.log(den), lse_all)
    lse_ref[...] = lse_all


def _window_attn(h_group, group):
    s = h_group.shape[0]
    _, d = DILATED_GROUPS[group]
    blocks_per_class = s // d // BLOCK
    spec = lambda sec: pl.BlockSpec(
        (BLOCK, GROUP_WIDTH), lambda r, nb: (r * blocks_per_class + nb, sec))
    return pl.pallas_call(
        _window_attn_kernel,
        grid=(d, blocks_per_class),
        in_specs=[spec(0), spec(1), spec(2)],
        out_specs=[
            pl.BlockSpec((BLOCK, GROUP_WIDTH), lambda r, nb: (r * blocks_per_class + nb, 0)),
            pl.BlockSpec((BLOCK, V7X_LANES), lambda r, nb: (r * blocks_per_class + nb, 0)),
        ],
        out_shape=[
            jax.ShapeDtypeStruct((s, GROUP_WIDTH), BF16),
            jax.ShapeDtypeStruct((s, V7X_LANES), F32),
        ],
        scratch_shapes=[pltpu.VMEM((2 * BLOCK, GROUP_WIDTH), BF16)] * 2,
        compiler_params=_params(2),
        name=f"window_attn_g{group}",
    )(h_group, h_group, h_group)


def _merge_kernel(o0_ref, o1_ref, o2_ref, l0_ref, l1_ref, l2_ref, out_ref):
    outs = [o0_ref[...].astype(F32)]
    lses = [l0_ref[...]]
    for o_ref, l_ref in ((o1_ref, l1_ref), (o2_ref, l2_ref)):
        d = o_ref.shape[0]
        perm = _row_permutation(PERM_ROWS, d, False)
        o_cm = jnp.concatenate([o_ref[r] for r in range(d)], axis=0)
        l_cm = jnp.concatenate([l_ref[r] for r in range(d)], axis=0)
        outs.append(_bdot(perm, o_cm))
        hi = l_cm.astype(BF16)
        rest = l_cm - hi.astype(F32)
        mid = rest.astype(BF16)
        lo = (rest - mid.astype(F32)).astype(BF16)
        lses.append(_bdot(perm, hi) + _bdot(perm, mid) + _bdot(perm, lo))
    l0, l1, l2 = lses
    m = jnp.maximum(jnp.maximum(l0, l1), l2)
    e0, e1, e2 = jnp.exp(l0 - m), jnp.exp(l1 - m), jnp.exp(l2 - m)
    tot = e0 + e1 + e2
    a0, a1, a2 = e0 / tot, e1 / tot, e2 / tot
    for h in range(N_GROUP_HEADS):
        cols = slice(h * HEAD_DIM, (h + 1) * HEAD_DIM)
        merged = (a0[:, h:h + 1] * outs[0][:, cols] + a1[:, h:h + 1] * outs[1][:, cols]
                  + a2[:, h:h + 1] * outs[2][:, cols])
        out_ref[:, cols] = merged.astype(out_ref.dtype)


def _merge_groups(outs, lses):
    s = outs[0].shape[0]
    rows = PERM_ROWS

    def specs(width):
        out = [pl.BlockSpec((rows, width), lambda i: (i, 0))]
        for _, d in DILATED_GROUPS[1:]:
            out.append(pl.BlockSpec((d, rows // d, width), lambda i: (0, i, 0)))
        return out

    class_major = lambda a, d: a.reshape(d, s // d, a.shape[1])
    outs = [outs[0]] + [class_major(o, d) for o, (_, d) in zip(outs[1:], DILATED_GROUPS[1:])]
    lses = [lses[0]] + [class_major(l, d) for l, (_, d) in zip(lses[1:], DILATED_GROUPS[1:])]
    return pl.pallas_call(
        _merge_kernel,
        grid=(s // rows,),
        in_specs=specs(GROUP_WIDTH) + specs(V7X_LANES),
        out_specs=pl.BlockSpec((rows, GROUP_WIDTH), lambda i: (i, 0)),
        out_shape=jax.ShapeDtypeStruct((s, GROUP_WIDTH), BF16),
        compiler_params=_params(1),
        name="merge_groups",
    )(*outs, *lses)


def kernel(x, mem, positions, g_mem, w_mem_kv, g_mix_pre, g_mix_post, g_ffn_pre, g_ffn_post,
           w_conv_in, conv_mix_w, w_conv_out, w_attn_in, w_attn_out,
           w_ffn_gate, w_ffn_up, conv_ffn_w, w_ffn_down):
    b, s, d = x.shape
    assert b == 1 and s == SEQ and d == D_MODEL
    xs = x.reshape(s, d)
    gains = lambda g: g.reshape(DEPTH, 1, d)
    g_mix_pre, g_mix_post = gains(g_mix_pre), gains(g_mix_post)
    g_ffn_pre, g_ffn_post = gains(g_ffn_pre), gains(g_ffn_post)

    kv = _mem_kv(mem.reshape(MEM_TOKENS, d), g_mem, w_mem_kv)
    cos_tab, sin_tab = _rope_tables(positions)

    dilations = tuple(dil for _, dil in DILATED_GROUPS[1:])
    xn = _rmsnorm_bf16(xs, g_mix_pre, 0)
    xn_class_major = ()
    for i in range(DEPTH):
        j = i // 2
        if i % 2 == 0:
            y_mix = _conv_in(xn, w_conv_in, conv_mix_w, j)
            y_mem = _mem_attend(xn, w_conv_in, j, 3 * CONV_WIDTH // MEM_HEAD_DIM, kv)
            y = _out_proj(y_mix, y_mem, w_conv_out, j)
        else:
            lhs = [xn] + [xp.reshape(s, d) for xp in xn_class_major]
            parts = [
                _window_attn(_attn_in(lhs[g], w_attn_in, cos_tab, sin_tab, j, g), g)
                for g in range(N_GROUPS)]
            y_mix = _merge_groups([p[0] for p in parts], [p[1] for p in parts])
            y_mem = _mem_attend(xn, w_attn_in, j, N_SELF // MEM_HEAD_DIM, kv)
            y = _out_proj(y_mix, y_mem, w_attn_out, j)
        xs, xn = _residual_norm(y, xs, g_mix_post, i, g_ffn_pre, i)
        a = _ffn_in(xn, w_ffn_gate, w_ffn_up, conv_ffn_w, i)
        y = _ffn_out(a, w_ffn_down, i)
        if i + 1 == DEPTH:
            xs = _residual_norm(y, xs, g_ffn_post, i)
        elif (i + 1) % 2 == 1:
            xs, xn, *xn_class_major = _residual_norm(
                y, xs, g_ffn_post, i, g_mix_pre, i + 1, dilations=dilations)
        else:
            xs, xn = _residual_norm(y, xs, g_ffn_post, i, g_mix_pre, i + 1)
    return xs.reshape(b, s, d)
```

```python
import functools

import jax
import jax.numpy as jnp
from jax import lax
from jax.experimental import pallas as pl
from jax.experimental.pallas import tpu as pltpu

D_MODEL = 4096
SEQ = 8192
DEPTH = 2
HEAD_DIM = 128
N_GROUP_HEADS = 16
GROUP_WIDTH = N_GROUP_HEADS * HEAD_DIM
DILATED_GROUPS = ((128, 1), (512, 4), (2048, 16))
N_GROUPS = len(DILATED_GROUPS)
N_SELF = N_GROUPS * 3 * GROUP_WIDTH
BLOCK = 128
CONV_WIDTH = 3 * D_MODEL // 4
MEM_TOKENS = 256
MEM_HEADS = 4
MEM_HEAD_DIM = 256
MEM_WIDTH = MEM_HEADS * MEM_HEAD_DIM
D_FF = 11008
ROPE_THETA = 10000.0
EPS = 1e-6
NEG_INF = -1e30

V7X_SUBLANES = 8
V7X_LANES = 128
V7X_MXU_COLS = 256
V7X_VMEM_LIMIT_BYTES = 60 * 1024 * 1024

BF16 = jnp.bfloat16
F32 = jnp.float32

ROW_BLOCK = 1024
BIG_ROW_BLOCK = 2048
CHUNK_ROWS = 512
NORM_ROWS = 512
PERM_ROWS = 256
HALO = V7X_SUBLANES


def _params(n_axes):
    return pltpu.CompilerParams(
        dimension_semantics=("arbitrary",) * n_axes,
        vmem_limit_bytes=V7X_VMEM_LIMIT_BYTES,
    )


def _rms_scale(v):
    return lax.rsqrt(jnp.mean(v * v, axis=-1, keepdims=True) + EPS)


def _bdot(a, b):
    return jnp.dot(a, b, preferred_element_type=F32)


def _norm_kernel(x_ref, g_ref, o_ref):
    x = x_ref[...]
    o_ref[...] = (x * _rms_scale(x) * g_ref[...]).astype(o_ref.dtype)


def _rmsnorm_bf16(x, g_all, layer):
    s, d = x.shape
    return pl.pallas_call(
        _norm_kernel,
        grid=(s // NORM_ROWS,),
        in_specs=[
            pl.BlockSpec((NORM_ROWS, d), lambda i: (i, 0)),
            pl.BlockSpec((None, 1, d), lambda i: (layer, 0, 0)),
        ],
        out_specs=pl.BlockSpec((NORM_ROWS, d), lambda i: (i, 0)),
        out_shape=jax.ShapeDtypeStruct((s, d), BF16),
        compiler_params=_params(1),
        name="rmsnorm_bf16",
    )(x, g_all)


def _residual_kernel(y_ref, x_ref, gpost_ref, gnext_ref, xo_ref, xn_ref):
    y = y_ref[...].astype(F32)
    xnew = x_ref[...] + y * _rms_scale(y) * gpost_ref[...]
    xo_ref[...] = xnew
    xn_ref[...] = (xnew * _rms_scale(xnew) * gnext_ref[...]).astype(xn_ref.dtype)


def _row_permutation(rows, d, to_class_major):
    n = rows // d
    out_row = lax.broadcasted_iota(jnp.int32, (rows, rows), 0)
    in_row = lax.broadcasted_iota(jnp.int32, (rows, rows), 1)
    if to_class_major:
        src = (out_row & (n - 1)) * d + (out_row >> (n.bit_length() - 1))
    else:
        src = (out_row & (d - 1)) * n + (out_row >> (d.bit_length() - 1))
    return (in_row == src).astype(BF16)


def _residual_dilated_kernel(y_ref, x_ref, gpost_ref, gnext_ref, xo_ref, xn_ref, *xp_refs):
    y = y_ref[...].astype(F32)
    xnew = x_ref[...] + y * _rms_scale(y) * gpost_ref[...]
    xo_ref[...] = xnew
    xn = (xnew * _rms_scale(xnew) * gnext_ref[...]).astype(BF16)
    xn_ref[...] = xn
    for xp_ref in xp_refs:
        d, n = xp_ref.shape[0], xp_ref.shape[1]
        xp = _bdot(_row_permutation(PERM_ROWS, d, True), xn).astype(BF16)
        for r in range(d):
            xp_ref[r] = xp[r * n:(r + 1) * n, :]


def _residual_last_kernel(y_ref, x_ref, gpost_ref, xo_ref):
    y = y_ref[...].astype(F32)
    xo_ref[...] = x_ref[...] + y * _rms_scale(y) * gpost_ref[...]


def _residual_norm(y, x, g_post, post_layer, g_next=None, next_layer=None, dilations=()):
    s, d = x.shape
    row = pl.BlockSpec((NORM_ROWS, d), lambda i: (i, 0))
    gspec = lambda layer: pl.BlockSpec((None, 1, d), lambda i: (layer, 0, 0))
    if dilations:
        row = pl.BlockSpec((PERM_ROWS, d), lambda i: (i, 0))
        return pl.pallas_call(
            _residual_dilated_kernel,
            grid=(s // PERM_ROWS,),
            in_specs=[row, row, gspec(post_layer), gspec(next_layer)],
            out_specs=[row, row] + [
                pl.BlockSpec((dil, PERM_ROWS // dil, d), lambda i: (0, i, 0)) for dil in dilations],
            out_shape=[jax.ShapeDtypeStruct((s, d), F32), jax.ShapeDtypeStruct((s, d), BF16)] + [
                jax.ShapeDtypeStruct((dil, s // dil, d), BF16) for dil in dilations],
            compiler_params=_params(1),
            name="residual_norm_dilated",
        )(y, x, g_post, g_next)
    if g_next is None:
        return pl.pallas_call(
            _residual_last_kernel,
            grid=(s // NORM_ROWS,),
            in_specs=[row, row, gspec(post_layer)],
            out_specs=row,
            out_shape=jax.ShapeDtypeStruct((s, d), F32),
            compiler_params=_params(1),
            name="residual_last",
        )(y, x, g_post)
    return pl.pallas_call(
        _residual_kernel,
        grid=(s // NORM_ROWS,),
        in_specs=[row, row, gspec(post_layer), gspec(next_layer)],
        out_specs=[row, row],
        out_shape=[jax.ShapeDtypeStruct((s, d), F32), jax.ShapeDtypeStruct((s, d), BF16)],
        compiler_params=_params(1),
        name="residual_norm",
    )(y, x, g_post, g_next)


def _rope_table_kernel(pos_ref, freq_ref, sign_ref, cos_ref, sin_ref):
    ang = pos_ref[...].astype(F32) * freq_ref[...]
    cos_ref[...] = jnp.cos(ang)
    sin_ref[...] = jnp.sin(ang) * sign_ref[...]


def _rope_tables(positions):
    s = positions.shape[-1]
    half = HEAD_DIM // 2
    inv_freq = ROPE_THETA ** (-jnp.arange(half, dtype=F32) * 2.0 / HEAD_DIM)
    freq = jnp.concatenate([inv_freq, inv_freq]).reshape(1, HEAD_DIM)
    sign = jnp.concatenate([-jnp.ones((half,), F32), jnp.ones((half,), F32)]).reshape(1, HEAD_DIM)
    pos = jnp.stack([positions.reshape(s // d, d).T.reshape(s, 1) for _, d in DILATED_GROUPS])
    rows = 1024
    tab = pl.BlockSpec((None, rows, HEAD_DIM), lambda g, i: (g, i, 0))
    const = pl.BlockSpec((1, HEAD_DIM), lambda g, i: (0, 0))
    return pl.pallas_call(
        _rope_table_kernel,
        grid=(N_GROUPS, s // rows),
        in_specs=[pl.BlockSpec((None, rows, 1), lambda g, i: (g, i, 0)), const, const],
        out_specs=[tab, tab],
        out_shape=[jax.ShapeDtypeStruct((N_GROUPS, s, HEAD_DIM), F32)] * 2,
        compiler_params=_params(2),
        name="rope_tables",
    )(pos, freq, sign)


def _mem_kv_kernel(mem_ref, g_ref, w_ref, o_ref):
    m = mem_ref[...]
    mn = (m * _rms_scale(m) * g_ref[...]).astype(BF16)
    o_ref[...] = _bdot(mn, w_ref[...].astype(BF16)).astype(o_ref.dtype)


def _mem_kv(mem, g_mem, w_mem_kv):
    m, d = mem.shape
    n = w_mem_kv.shape[1]
    bn = 512
    return pl.pallas_call(
        _mem_kv_kernel,
        grid=(n // bn,),
        in_specs=[
            pl.BlockSpec((m, d), lambda j: (0, 0)),
            pl.BlockSpec((1, d), lambda j: (0, 0)),
            pl.BlockSpec((d, bn), lambda j: (0, j)),
        ],
        out_specs=pl.BlockSpec((m, bn), lambda j: (0, j)),
        out_shape=jax.ShapeDtypeStruct((m, n), BF16),
        compiler_params=_params(1),
        name="mem_kv",
    )(mem, g_mem.reshape(1, d), w_mem_kv)


def _causal_conv3(v, prev_tail, cw_ref, ext_ref):
    bm = v.shape[0]
    ext_ref[0:HALO, :] = prev_tail
    ext_ref[HALO:HALO + bm, :] = v
    v1 = ext_ref[HALO - 1:HALO - 1 + bm, :]
    v2 = ext_ref[HALO - 2:HALO - 2 + bm, :]
    conv = cw_ref[0:1, :] * v2 + cw_ref[1:2, :] * v1 + cw_ref[2:3, :] * v
    return conv, v[bm - HALO:bm, :]


def _row_chunks(total):
    return [slice(a, a + CHUNK_ROWS) for a in range(0, total, CHUNK_ROWS)]


def _resident_rows(bm, k, buffers=2):
    return pl.BlockSpec((bm, k), lambda i, j: (i, 0), pipeline_mode=pl.Buffered(buffers))


def _conv_in_kernel(x_ref, wb_ref, wc_ref, wu_ref, cw_ref, o_ref, ext0_ref, ext1_ref, tails_ref):
    i, j = pl.program_id(0), pl.program_id(1)
    tail = jnp.where(i == 0, 0.0, tails_ref[j])
    for chunk in range(x_ref.shape[0] // CHUNK_ROWS):
        ext_ref = (ext0_ref, ext1_ref)[chunk % 2]
        rows = slice(chunk * CHUNK_ROWS, (chunk + 1) * CHUNK_ROWS)
        x = x_ref[rows, :]
        b = _bdot(x, wb_ref[...].astype(BF16))
        c = _bdot(x, wc_ref[...].astype(BF16))
        u = _bdot(x, wu_ref[...].astype(BF16))
        conv, tail = _causal_conv3(c * u, tail, cw_ref, ext_ref)
        o_ref[rows, :] = (b * conv).astype(o_ref.dtype)
    tails_ref[j] = tail


def _conv_in(xn, w_in, conv_w, layer):
    s, d = xn.shape
    bm, bn = BIG_ROW_BLOCK, V7X_MXU_COLS
    nj = CONV_WIDTH // bn
    wspec = lambda off: pl.BlockSpec((None, d, bn), lambda i, j: (layer, 0, j + off))
    ext = pltpu.VMEM((CHUNK_ROWS + HALO, bn), F32)
    return pl.pallas_call(
        _conv_in_kernel,
        grid=(s // bm, nj),
        in_specs=[
            _resident_rows(bm, d, buffers=1),
            wspec(0), wspec(nj), wspec(2 * nj),
            pl.BlockSpec((None, 3, bn), lambda i, j: (layer, 0, j)),
        ],
        out_specs=pl.BlockSpec((bm, bn), lambda i, j: (i, j)),
        out_shape=jax.ShapeDtypeStruct((s, CONV_WIDTH), BF16),
        scratch_shapes=[ext, ext, pltpu.VMEM((nj, HALO, bn), F32)],
        compiler_params=_params(2),
        name="conv_in",
    )(xn, w_in, w_in, w_in, conv_w)


def _mem_q_kernel(x_ref, wq_ref, k_ref, v_ref, o_ref):
    q = _bdot(x_ref[...], wq_ref[...].astype(BF16)) * (MEM_HEAD_DIM ** -0.5)
    s = lax.dot_general(q.astype(BF16), k_ref[...], (((1,), (1,)), ((), ())),
                        preferred_element_type=F32)
    m = jnp.max(s, axis=-1, keepdims=True)
    p = jnp.exp(s - m)
    den = jnp.sum(p, axis=-1, keepdims=True)
    o = _bdot((p / den).astype(BF16), v_ref[...])
    o_ref[...] = o.astype(o_ref.dtype)


def _mem_attend(xn, w_in, layer, q_col_block, kv):
    s, d = xn.shape
    bm, bn = BIG_ROW_BLOCK, MEM_HEAD_DIM
    return pl.pallas_call(
        _mem_q_kernel,
        grid=(s // bm, MEM_HEADS),
        in_specs=[
            _resident_rows(bm, d),
            pl.BlockSpec((None, d, bn), lambda i, h: (layer, 0, q_col_block + h)),
            pl.BlockSpec((MEM_TOKENS, bn), lambda i, h: (0, h)),
            pl.BlockSpec((MEM_TOKENS, bn), lambda i, h: (0, MEM_HEADS + h)),
        ],
        out_specs=pl.BlockSpec((bm, bn), lambda i, h: (i, h)),
        out_shape=jax.ShapeDtypeStruct((s, MEM_WIDTH), BF16),
        compiler_params=_params(2),
        name="mem_attend",
    )(xn, w_in, kv, kv)


def _out_proj_kernel(a_ref, b_ref, wa_ref, wb_ref, o_ref):
    for chunk in range(a_ref.shape[0] // CHUNK_ROWS):
        rows = slice(chunk * CHUNK_ROWS, (chunk + 1) * CHUNK_ROWS)
        acc = _bdot(a_ref[rows, :], wa_ref[...].astype(BF16))
        acc = acc + _bdot(b_ref[rows, :], wb_ref[...].astype(BF16))
        o_ref[rows, :] = acc.astype(o_ref.dtype)


def _out_proj(a, b, w, layer):
    s, ka = a.shape
    kb = b.shape[1]
    n = w.shape[2]
    bm, bn = BIG_ROW_BLOCK, 512
    assert ka % kb == 0
    return pl.pallas_call(
        _out_proj_kernel,
        grid=(s // bm, n // bn),
        in_specs=[
            _resident_rows(bm, ka), _resident_rows(bm, kb),
            pl.BlockSpec((None, ka, bn), lambda i, j: (layer, 0, j)),
            pl.BlockSpec((None, kb, bn), lambda i, j: (layer, ka // kb, j)),
        ],
        out_specs=pl.BlockSpec((bm, bn), lambda i, j: (i, j)),
        out_shape=jax.ShapeDtypeStruct((s, n), BF16),
        compiler_params=_params(2),
        name="out_proj",
    )(a, b, w, w)


def _ffn_in_kernel(x_ref, wg_ref, wu_ref, cw_ref, o_ref, ext0_ref, ext1_ref, tails_ref):
    i, j = pl.program_id(0), pl.program_id(1)
    tail = jnp.where(i == 0, 0.0, tails_ref[j])
    for chunk, rows in enumerate(_row_chunks(x_ref.shape[0])):
        ext_ref = (ext0_ref, ext1_ref)[chunk % 2]
        x = x_ref[rows, :]
        g = _bdot(x, wg_ref[...].astype(BF16))
        u = _bdot(x, wu_ref[...].astype(BF16))
        gc, tail = _causal_conv3(g, tail, cw_ref, ext_ref)
        act = gc / (1.0 + jnp.exp(-gc))
        o_ref[rows, :] = (act * u).astype(o_ref.dtype)
    tails_ref[j] = tail


def _ffn_in(xn, w_gate, w_up, conv_w, layer):
    s, d = xn.shape
    bm, bn = BIG_ROW_BLOCK, V7X_MXU_COLS
    nj = D_FF // bn
    wspec = pl.BlockSpec((None, d, bn), lambda i, j: (layer, 0, j))
    ext = pltpu.VMEM((CHUNK_ROWS + HALO, bn), F32)
    return pl.pallas_call(
        _ffn_in_kernel,
        grid=(s // bm, nj),
        in_specs=[
            _resident_rows(bm, d),
            wspec, wspec,
            pl.BlockSpec((None, 3, bn), lambda i, j: (layer, 0, j)),
        ],
        out_specs=pl.BlockSpec((bm, bn), lambda i, j: (i, j)),
        out_shape=jax.ShapeDtypeStruct((s, D_FF), BF16),
        scratch_shapes=[ext, ext, pltpu.VMEM((nj, HALO, bn), F32)],
        compiler_params=_params(2),
        name="ffn_in",
    )(xn, w_gate, w_up, conv_w)


def _ffn_out_kernel(a_ref, w_ref, o_ref, acc_ref, *, kdim):
    k = pl.program_id(2)
    last = pl.num_programs(2) - 1
    bk = a_ref.shape[1]

    def step(first, final):
        valid = kdim - k * bk
        a_cols = lax.broadcasted_iota(jnp.int32, a_ref.shape, 1)
        a = jnp.where(a_cols < valid, a_ref[...], jnp.zeros_like(a_ref))
        w_rows = lax.broadcasted_iota(jnp.int32, (bk, FFN_OUT_COL_CHUNK), 0)
        for c in range(0, o_ref.shape[1], FFN_OUT_COL_CHUNK):
            cols = slice(c, c + FFN_OUT_COL_CHUNK)
            w = jnp.where(w_rows < valid, w_ref[:, cols], 0.0).astype(BF16)
            for rows in _row_chunks(a_ref.shape[0]):
                part = _bdot(a[rows, :], w)
                total = part if first else acc_ref[rows, cols] + part
                if final:
                    o_ref[rows, cols] = total.astype(o_ref.dtype)
                else:
                    acc_ref[rows, cols] = total

    pl.when(k == 0)(lambda: step(True, False))
    pl.when(jnp.logical_and(k > 0, k < last))(lambda: step(False, False))
    pl.when(k == last)(lambda: step(False, True))


FFN_OUT_COL_CHUNK = 512
FFN_OUT_K_BLOCK = 512
FFN_OUT_COLS = 2048


def _ffn_out(a, w_down, layer):
    s, kdim = a.shape
    n = w_down.shape[2]
    bm, bk, bn = BIG_ROW_BLOCK, FFN_OUT_K_BLOCK, FFN_OUT_COLS
    return pl.pallas_call(
        functools.partial(_ffn_out_kernel, kdim=kdim),
        grid=(s // bm, n // bn, pl.cdiv(kdim, bk)),
        in_specs=[
            pl.BlockSpec((bm, bk), lambda i, j, k: (i, k)),
            pl.BlockSpec((None, bk, bn), lambda i, j, k: (layer, k, j)),
        ],
        out_specs=pl.BlockSpec((bm, bn), lambda i, j, k: (i, j)),
        out_shape=jax.ShapeDtypeStruct((s, n), BF16),
        scratch_shapes=[pltpu.VMEM((bm, bn), F32)],
        compiler_params=_params(3),
        name="ffn_out",
    )(a, w_down)


ATTN_IN_COLS = 512
HEADS_PER_ATTN_IN_BLOCK = ATTN_IN_COLS // HEAD_DIM
ATTN_IN_BLOCKS_PER_SECTION = GROUP_WIDTH // ATTN_IN_COLS
ATTN_IN_BLOCKS_PER_GROUP = 3 * ATTN_IN_BLOCKS_PER_SECTION
ATTN_IN_CHUNK_ROWS = 512


def _attn_in_kernel(x_ref, w_ref, cos_ref, sin_ref, o_ref):
    section = pl.program_id(1) // ATTN_IN_BLOCKS_PER_SECTION
    scale = jnp.where(section == 0, HEAD_DIM ** -0.5, 1.0).astype(F32)
    is_v = section == 2
    for chunk in range(x_ref.shape[0] // ATTN_IN_CHUNK_ROWS):
        rows = slice(chunk * ATTN_IN_CHUNK_ROWS, (chunk + 1) * ATTN_IN_CHUNK_ROWS)
        acc = _bdot(x_ref[rows, :], w_ref[...].astype(BF16))
        cos = cos_ref[rows, :] * scale
        sin = sin_ref[rows, :] * scale
        for h in range(HEADS_PER_ATTN_IN_BLOCK):
            cols = slice(h * HEAD_DIM, (h + 1) * HEAD_DIM)
            v = acc[:, cols]
            rot = pltpu.roll(v, HEAD_DIM // 2, axis=1)
            o_ref[rows, cols] = jnp.where(is_v, v, v * cos + rot * sin).astype(o_ref.dtype)


def _attn_in(xn, w_in, cos_tab, sin_tab, layer, group):
    s, d = xn.shape
    bm, bn = BIG_ROW_BLOCK, ATTN_IN_COLS
    nj = ATTN_IN_BLOCKS_PER_GROUP
    tab = pl.BlockSpec((None, bm, HEAD_DIM), lambda i, j: (group, i, 0))
    return pl.pallas_call(
        _attn_in_kernel,
        grid=(s // bm, nj),
        in_specs=[
            _resident_rows(bm, d),
            pl.BlockSpec((None, d, bn), lambda i, j: (layer, 0, group * nj + j)),
            tab, tab,
        ],
        out_specs=pl.BlockSpec((bm, bn), lambda i, j: (i, j)),
        out_shape=jax.ShapeDtypeStruct((s, 3 * GROUP_WIDTH), BF16),
        compiler_params=_params(2),
        name=f"attn_in_g{group}",
    )(xn, w_in, cos_tab, sin_tab)


def _window_attn_kernel(q_ref, k_ref, v_ref, o_ref, lse_ref, k2_ref, v2_ref):
    nb = pl.program_id(1)

    @pl.when(nb == 0)
    def _():
        k2_ref[...] = jnp.zeros_like(k2_ref)
        v2_ref[...] = jnp.zeros_like(v2_ref)

    own_half = nb % 2
    own_rows = pl.ds(pl.multiple_of(own_half * BLOCK, BLOCK), BLOCK)
    k2_ref[own_rows, :] = k_ref[...]
    v2_ref[own_rows, :] = v_ref[...]

    qi = lax.broadcasted_iota(jnp.int32, (BLOCK, 2 * BLOCK), 0)
    col = lax.broadcasted_iota(jnp.int32, (BLOCK, 2 * BLOCK), 1)
    ki = col & (BLOCK - 1)
    in_own_half = (col >> (BLOCK.bit_length() - 1)) == own_half
    valid = jnp.logical_or(
        jnp.logical_and(in_own_half, ki <= qi),
        jnp.logical_and(jnp.logical_not(in_own_half), jnp.logical_and(ki >= qi, nb > 0)))
    lane = lax.broadcasted_iota(jnp.int32, (BLOCK, V7X_LANES), 1)
    ones = jnp.ones((2 * BLOCK, HEAD_DIM), BF16)
    lse_all = jnp.zeros((BLOCK, V7X_LANES), F32)
    nt = (((1,), (1,)), ((), ()))
    for h in range(N_GROUP_HEADS):
        cols = slice(h * HEAD_DIM, (h + 1) * HEAD_DIM)
        s = lax.dot_general(q_ref[:, cols], k2_ref[:, cols], nt, preferred_element_type=F32)
        s = jnp.where(valid, s, NEG_INF)
        m = jnp.max(s, axis=-1, keepdims=True)
        p = jnp.exp(s - m).astype(BF16)
        ov = _bdot(p, jnp.concatenate([v2_ref[:, cols], ones], axis=1))
        den = ov[:, HEAD_DIM:]
        o_ref[:, cols] = (ov[:, :HEAD_DIM] / den).astype(o_ref.dtype)
        lse_all = jnp.where(lane == h, m + jnp.log(den), lse_all)
    lse_ref[...] = lse_all


def _window_attn(h_group, group):
    s = h_group.shape[0]
    _, d = DILATED_GROUPS[group]
    blocks_per_class = s // d // BLOCK
    spec = lambda sec: pl.BlockSpec(
        (BLOCK, GROUP_WIDTH), lambda r, nb: (r * blocks_per_class + nb, sec))
    return pl.pallas_call(
        _window_attn_kernel,
        grid=(d, blocks_per_class),
        in_specs=[spec(0), spec(1), spec(2)],
        out_specs=[
            pl.BlockSpec((BLOCK, GROUP_WIDTH), lambda r, nb: (r * blocks_per_class + nb, 0)),
            pl.BlockSpec((BLOCK, V7X_LANES), lambda r, nb: (r * blocks_per_class + nb, 0)),
        ],
        out_shape=[
            jax.ShapeDtypeStruct((s, GROUP_WIDTH), BF16),
            jax.ShapeDtypeStruct((s, V7X_LANES), F32),
        ],
        scratch_shapes=[pltpu.VMEM((2 * BLOCK, GROUP_WIDTH), BF16)] * 2,
        compiler_params=_params(2),
        name=f"window_attn_g{group}",
    )(h_group, h_group, h_group)


def _bf16_pieces(v):
    hi = v.astype(BF16)
    rest = v - hi.astype(F32)
    mid = rest.astype(BF16)
    lo = (rest - mid.astype(F32)).astype(BF16)
    return hi, mid, lo


def _merge_kernel(o0_ref, o1_ref, o2_ref, l0_ref, l1_ref, l2_ref, out_ref):
    outs = [o0_ref[...].astype(F32)]
    lses = [l0_ref[...]]
    for o_ref, l_ref in ((o1_ref, l1_ref), (o2_ref, l2_ref)):
        d = o_ref.shape[0]
        perm = _row_permutation(PERM_ROWS, d, False)
        o_cm = jnp.concatenate([o_ref[r] for r in range(d)], axis=0)
        l_cm = jnp.concatenate([l_ref[r] for r in range(d)], axis=0)
        outs.append(_bdot(perm, o_cm))
        hi, mid, lo = _bf16_pieces(l_cm)
        lses.append(_bdot(perm, hi) + _bdot(perm, mid) + _bdot(perm, lo))
    l0, l1, l2 = lses
    m = jnp.maximum(jnp.maximum(l0, l1), l2)
    e0, e1, e2 = jnp.exp(l0 - m), jnp.exp(l1 - m), jnp.exp(l2 - m)
    tot = e0 + e1 + e2
    a0, a1, a2 = e0 / tot, e1 / tot, e2 / tot
    for h in range(N_GROUP_HEADS):
        cols = slice(h * HEAD_DIM, (h + 1) * HEAD_DIM)
        merged = (a0[:, h:h + 1] * outs[0][:, cols] + a1[:, h:h + 1] * outs[1][:, cols]
                  + a2[:, h:h + 1] * outs[2][:, cols])
        out_ref[:, cols] = merged.astype(out_ref.dtype)


def _merge_groups(outs, lses):
    s = outs[0].shape[0]
    rows = PERM_ROWS

    def specs(width):
        out = [pl.BlockSpec((rows, width), lambda i: (i, 0))]
        for _, d in DILATED_GROUPS[1:]:
            out.append(pl.BlockSpec((d, rows // d, width), lambda i: (0, i, 0)))
        return out

    class_major = lambda a, d: a.reshape(d, s // d, a.shape[1])
    outs = [outs[0]] + [class_major(o, d) for o, (_, d) in zip(outs[1:], DILATED_GROUPS[1:])]
    lses = [lses[0]] + [class_major(l, d) for l, (_, d) in zip(lses[1:], DILATED_GROUPS[1:])]
    return pl.pallas_call(
        _merge_kernel,
        grid=(s // rows,),
        in_specs=specs(GROUP_WIDTH) + specs(V7X_LANES),
        out_specs=pl.BlockSpec((rows, GROUP_WIDTH), lambda i: (i, 0)),
        out_shape=jax.ShapeDtypeStruct((s, GROUP_WIDTH), BF16),
        compiler_params=_params(1),
        name="merge_groups",
    )(*outs, *lses)


def kernel(x, mem, positions, g_mem, w_mem_kv, g_mix_pre, g_mix_post, g_ffn_pre, g_ffn_post,
           w_conv_in, conv_mix_w, w_conv_out, w_attn_in, w_attn_out,
           w_ffn_gate, w_ffn_up, conv_ffn_w, w_ffn_down):
    b, s, d = x.shape
    assert b == 1 and s == SEQ and d == D_MODEL
    xs = x.reshape(s, d)
    gains = lambda g: g.reshape(DEPTH, 1, d)
    g_mix_pre, g_mix_post = gains(g_mix_pre), gains(g_mix_post)
    g_ffn_pre, g_ffn_post = gains(g_ffn_pre), gains(g_ffn_post)

    kv = _mem_kv(mem.reshape(MEM_TOKENS, d), g_mem, w_mem_kv)
    cos_tab, sin_tab = _rope_tables(positions)

    dilations = tuple(dil for _, dil in DILATED_GROUPS[1:])
    xn = _rmsnorm_bf16(xs, g_mix_pre, 0)
    xn_class_major = ()
    for i in range(DEPTH):
        j = i // 2
        if i % 2 == 0:
            y_mix = _conv_in(xn, w_conv_in, conv_mix_w, j)
            y_mem = _mem_attend(xn, w_conv_in, j, 3 * CONV_WIDTH // MEM_HEAD_DIM, kv)
            y = _out_proj(y_mix, y_mem, w_conv_out, j)
        else:
            lhs = [xn] + [xp.reshape(s, d) for xp in xn_class_major]
            parts = [
                _window_attn(_attn_in(lhs[g], w_attn_in, cos_tab, sin_tab, j, g), g)
                for g in range(N_GROUPS)]
            y_mix = _merge_groups([p[0] for p in parts], [p[1] for p in parts])
            y_mem = _mem_attend(xn, w_attn_in, j, N_SELF // MEM_HEAD_DIM, kv)
            y = _out_proj(y_mix, y_mem, w_attn_out, j)
        xs, xn = _residual_norm(y, xs, g_mix_post, i, g_ffn_pre, i)
        a = _ffn_in(xn, w_ffn_gate, w_ffn_up, conv_ffn_w, i)
        y = _ffn_out(a, w_ffn_down, i)
        if i + 1 == DEPTH:
            xs = _residual_norm(y, xs, g_ffn_post, i)
        elif (i + 1) % 2 == 1:
            xs, xn, *xn_class_major = _residual_norm(
                y, xs, g_ffn_post, i, g_mix_pre, i + 1, dilations=dilations)
        else:
            xs, xn = _residual_norm(y, xs, g_ffn_post, i, g_mix_pre, i + 1)
    return xs.reshape(b, s, d)
```

```python
import functools

import jax
import jax.numpy as jnp
from jax import lax
from jax.experimental import pallas as pl
from jax.experimental.pallas import tpu as pltpu

D_MODEL = 4096
SEQ = 8192
DEPTH = 2
HEAD_DIM = 128
N_GROUP_HEADS = 16
GROUP_WIDTH = N_GROUP_HEADS * HEAD_DIM
DILATED_GROUPS = ((128, 1), (512, 4), (2048, 16))
N_GROUPS = len(DILATED_GROUPS)
N_SELF = N_GROUPS * 3 * GROUP_WIDTH
BLOCK = 128
CONV_WIDTH = 3 * D_MODEL // 4
MEM_TOKENS = 256
MEM_HEADS = 4
MEM_HEAD_DIM = 256
MEM_WIDTH = MEM_HEADS * MEM_HEAD_DIM
D_FF = 11008
ROPE_THETA = 10000.0
EPS = 1e-6
NEG_INF = -1e30

V7X_SUBLANES = 8
V7X_LANES = 128
V7X_MXU_COLS = 256
V7X_VMEM_BYTES = 64 * 1024 * 1024
V7X_VMEM_LIMIT_BYTES = V7X_VMEM_BYTES - 2 * 1024 * 1024

BF16 = jnp.bfloat16
F32 = jnp.float32

ROW_BLOCK = 1024
BIG_ROW_BLOCK = 2048
CHUNK_ROWS = 512
NORM_ROWS = 512
PERM_ROWS = 256
HALO = V7X_SUBLANES


def _params(n_axes):
    return pltpu.CompilerParams(
        dimension_semantics=("arbitrary",) * n_axes,
        vmem_limit_bytes=V7X_VMEM_LIMIT_BYTES,
    )


def _rms_scale(v):
    return lax.rsqrt(jnp.mean(v * v, axis=-1, keepdims=True) + EPS)


def _bdot(a, b):
    return jnp.dot(a, b, preferred_element_type=F32)


def _norm_kernel(x_ref, g_ref, o_ref):
    x = x_ref[...]
    o_ref[...] = (x * _rms_scale(x) * g_ref[...]).astype(o_ref.dtype)


def _rmsnorm_bf16(x, g_all, layer):
    s, d = x.shape
    return pl.pallas_call(
        _norm_kernel,
        grid=(s // NORM_ROWS,),
        in_specs=[
            pl.BlockSpec((NORM_ROWS, d), lambda i: (i, 0)),
            pl.BlockSpec((None, 1, d), lambda i: (layer, 0, 0)),
        ],
        out_specs=pl.BlockSpec((NORM_ROWS, d), lambda i: (i, 0)),
        out_shape=jax.ShapeDtypeStruct((s, d), BF16),
        compiler_params=_params(1),
        name="rmsnorm_bf16",
    )(x, g_all)


def _residual_kernel(y_ref, x_ref, gpost_ref, gnext_ref, xo_ref, xn_ref):
    y = y_ref[...].astype(F32)
    xnew = x_ref[...] + y * _rms_scale(y) * gpost_ref[...]
    xo_ref[...] = xnew
    xn_ref[...] = (xnew * _rms_scale(xnew) * gnext_ref[...]).astype(xn_ref.dtype)


def _row_permutation(rows, d, to_class_major):
    n = rows // d
    out_row = lax.broadcasted_iota(jnp.int32, (rows, rows), 0)
    in_row = lax.broadcasted_iota(jnp.int32, (rows, rows), 1)
    if to_class_major:
        src = (out_row & (n - 1)) * d + (out_row >> (n.bit_length() - 1))
    else:
        src = (out_row & (d - 1)) * n + (out_row >> (d.bit_length() - 1))
    return (in_row == src).astype(BF16)


def _residual_dilated_kernel(y_ref, x_ref, gpost_ref, gnext_ref, xo_ref, xn_ref, *xp_refs):
    y = y_ref[...].astype(F32)
    xnew = x_ref[...] + y * _rms_scale(y) * gpost_ref[...]
    xo_ref[...] = xnew
    xn = (xnew * _rms_scale(xnew) * gnext_ref[...]).astype(BF16)
    xn_ref[...] = xn
    for xp_ref in xp_refs:
        d, n = xp_ref.shape[0], xp_ref.shape[1]
        xp = _bdot(_row_permutation(PERM_ROWS, d, True), xn).astype(BF16)
        for r in range(d):
            xp_ref[r] = xp[r * n:(r + 1) * n, :]


def _residual_last_kernel(y_ref, x_ref, gpost_ref, xo_ref):
    y = y_ref[...].astype(F32)
    xo_ref[...] = x_ref[...] + y * _rms_scale(y) * gpost_ref[...]


def _residual_norm(y, x, g_post, post_layer, g_next=None, next_layer=None, dilations=()):
    s, d = x.shape
    row = pl.BlockSpec((NORM_ROWS, d), lambda i: (i, 0))
    gspec = lambda layer: pl.BlockSpec((None, 1, d), lambda i: (layer, 0, 0))
    if dilations:
        row = pl.BlockSpec((PERM_ROWS, d), lambda i: (i, 0))
        return pl.pallas_call(
            _residual_dilated_kernel,
            grid=(s // PERM_ROWS,),
            in_specs=[row, row, gspec(post_layer), gspec(next_layer)],
            out_specs=[row, row] + [
                pl.BlockSpec((dil, PERM_ROWS // dil, d), lambda i: (0, i, 0)) for dil in dilations],
            out_shape=[jax.ShapeDtypeStruct((s, d), F32), jax.ShapeDtypeStruct((s, d), BF16)] + [
                jax.ShapeDtypeStruct((dil, s // dil, d), BF16) for dil in dilations],
            compiler_params=_params(1),
            name="residual_norm_dilated",
        )(y, x, g_post, g_next)
    if g_next is None:
        return pl.pallas_call(
            _residual_last_kernel,
            grid=(s // NORM_ROWS,),
            in_specs=[row, row, gspec(post_layer)],
            out_specs=row,
            out_shape=jax.ShapeDtypeStruct((s, d), F32),
            compiler_params=_params(1),
            name="residual_last",
        )(y, x, g_post)
    return pl.pallas_call(
        _residual_kernel,
        grid=(s // NORM_ROWS,),
        in_specs=[row, row, gspec(post_layer), gspec(next_layer)],
        out_specs=[row, row],
        out_shape=[jax.ShapeDtypeStruct((s, d), F32), jax.ShapeDtypeStruct((s, d), BF16)],
        compiler_params=_params(1),
        name="residual_norm",
    )(y, x, g_post, g_next)


def _rope_table_kernel(pos_ref, freq_ref, sign_ref, cos_ref, sin_ref):
    ang = pos_ref[...].astype(F32) * freq_ref[...]
    cos_ref[...] = jnp.cos(ang)
    sin_ref[...] = jnp.sin(ang) * sign_ref[...]


def _rope_tables(positions):
    s = positions.shape[-1]
    half = HEAD_DIM // 2
    inv_freq = ROPE_THETA ** (-jnp.arange(half, dtype=F32) * 2.0 / HEAD_DIM)
    freq = jnp.concatenate([inv_freq, inv_freq]).reshape(1, HEAD_DIM)
    sign = jnp.concatenate([-jnp.ones((half,), F32), jnp.ones((half,), F32)]).reshape(1, HEAD_DIM)
    pos = jnp.stack([positions.reshape(s // d, d).T.reshape(s, 1) for _, d in DILATED_GROUPS])
    rows = 1024
    tab = pl.BlockSpec((None, rows, HEAD_DIM), lambda g, i: (g, i, 0))
    const = pl.BlockSpec((1, HEAD_DIM), lambda g, i: (0, 0))
    return pl.pallas_call(
        _rope_table_kernel,
        grid=(N_GROUPS, s // rows),
        in_specs=[pl.BlockSpec((None, rows, 1), lambda g, i: (g, i, 0)), const, const],
        out_specs=[tab, tab],
        out_shape=[jax.ShapeDtypeStruct((N_GROUPS, s, HEAD_DIM), F32)] * 2,
        compiler_params=_params(2),
        name="rope_tables",
    )(pos, freq, sign)


def _mem_kv_kernel(mem_ref, g_ref, w_ref, o_ref):
    m = mem_ref[...]
    mn = (m * _rms_scale(m) * g_ref[...]).astype(BF16)
    o_ref[...] = _bdot(mn, w_ref[...].astype(BF16)).astype(o_ref.dtype)


def _mem_kv(mem, g_mem, w_mem_kv):
    m, d = mem.shape
    n = w_mem_kv.shape[1]
    bn = 512
    return pl.pallas_call(
        _mem_kv_kernel,
        grid=(n // bn,),
        in_specs=[
            pl.BlockSpec((m, d), lambda j: (0, 0)),
            pl.BlockSpec((1, d), lambda j: (0, 0)),
            pl.BlockSpec((d, bn), lambda j: (0, j)),
        ],
        out_specs=pl.BlockSpec((m, bn), lambda j: (0, j)),
        out_shape=jax.ShapeDtypeStruct((m, n), BF16),
        compiler_params=_params(1),
        name="mem_kv",
    )(mem, g_mem.reshape(1, d), w_mem_kv)


def _causal_conv3(v, prev_tail, cw_ref, ext_ref):
    bm = v.shape[0]
    ext_ref[0:HALO, :] = prev_tail
    ext_ref[HALO:HALO + bm, :] = v
    v1 = ext_ref[HALO - 1:HALO - 1 + bm, :]
    v2 = ext_ref[HALO - 2:HALO - 2 + bm, :]
    conv = cw_ref[0:1, :] * v2 + cw_ref[1:2, :] * v1 + cw_ref[2:3, :] * v
    return conv, v[bm - HALO:bm, :]


def _row_chunks(total):
    return [slice(a, a + CHUNK_ROWS) for a in range(0, total, CHUNK_ROWS)]


def _resident_rows(bm, k, buffers=2):
    return pl.BlockSpec((bm, k), lambda i, j: (i, 0), pipeline_mode=pl.Buffered(buffers))


def _conv_in_kernel(x_ref, wb_ref, wc_ref, wu_ref, cw_ref, o_ref, ext0_ref, ext1_ref, tails_ref):
    i, j = pl.program_id(0), pl.program_id(1)
    tail = jnp.where(i == 0, 0.0, tails_ref[j])
    for chunk in range(x_ref.shape[0] // CHUNK_ROWS):
        ext_ref = (ext0_ref, ext1_ref)[chunk % 2]
        rows = slice(chunk * CHUNK_ROWS, (chunk + 1) * CHUNK_ROWS)
        x = x_ref[rows, :]
        b = _bdot(x, wb_ref[...].astype(BF16))
        c = _bdot(x, wc_ref[...].astype(BF16))
        u = _bdot(x, wu_ref[...].astype(BF16))
        conv, tail = _causal_conv3(c * u, tail, cw_ref, ext_ref)
        o_ref[rows, :] = (b * conv).astype(o_ref.dtype)
    tails_ref[j] = tail


def _conv_in(xn, w_in, conv_w, layer):
    s, d = xn.shape
    bm, bn = BIG_ROW_BLOCK, V7X_MXU_COLS
    nj = CONV_WIDTH // bn
    wspec = lambda off: pl.BlockSpec((None, d, bn), lambda i, j: (layer, 0, j + off))
    ext = pltpu.VMEM((CHUNK_ROWS + HALO, bn), F32)
    return pl.pallas_call(
        _conv_in_kernel,
        grid=(s // bm, nj),
        in_specs=[
            _resident_rows(bm, d, buffers=1),
            wspec(0), wspec(nj), wspec(2 * nj),
            pl.BlockSpec((None, 3, bn), lambda i, j: (layer, 0, j)),
        ],
        out_specs=pl.BlockSpec((bm, bn), lambda i, j: (i, j)),
        out_shape=jax.ShapeDtypeStruct((s, CONV_WIDTH), BF16),
        scratch_shapes=[ext, ext, pltpu.VMEM((nj, HALO, bn), F32)],
        compiler_params=_params(2),
        name="conv_in",
    )(xn, w_in, w_in, w_in, conv_w)


def _mem_q_kernel(x_ref, wq_ref, k_ref, v_ref, o_ref):
    q = _bdot(x_ref[...], wq_ref[...].astype(BF16)) * (MEM_HEAD_DIM ** -0.5)
    s = lax.dot_general(q.astype(BF16), k_ref[...], (((1,), (1,)), ((), ())),
                        preferred_element_type=F32)
    m = jnp.max(s, axis=-1, keepdims=True)
    p = jnp.exp(s - m)
    den = jnp.sum(p, axis=-1, keepdims=True)
    o = _bdot((p / den).astype(BF16), v_ref[...])
    o_ref[...] = o.astype(o_ref.dtype)


def _mem_attend(xn, w_in, layer, q_col_block, kv):
    s, d = xn.shape
    bm, bn = BIG_ROW_BLOCK, MEM_HEAD_DIM
    return pl.pallas_call(
        _mem_q_kernel,
        grid=(s // bm, MEM_HEADS),
        in_specs=[
            _resident_rows(bm, d),
            pl.BlockSpec((None, d, bn), lambda i, h: (layer, 0, q_col_block + h)),
            pl.BlockSpec((MEM_TOKENS, bn), lambda i, h: (0, h)),
            pl.BlockSpec((MEM_TOKENS, bn), lambda i, h: (0, MEM_HEADS + h)),
        ],
        out_specs=pl.BlockSpec((bm, bn), lambda i, h: (i, h)),
        out_shape=jax.ShapeDtypeStruct((s, MEM_WIDTH), BF16),
        compiler_params=_params(2),
        name="mem_attend",
    )(xn, w_in, kv, kv)


def _out_proj_kernel(a_ref, b_ref, wa_ref, wb_ref, o_ref):
    for chunk in range(a_ref.shape[0] // CHUNK_ROWS):
        rows = slice(chunk * CHUNK_ROWS, (chunk + 1) * CHUNK_ROWS)
        acc = _bdot(a_ref[rows, :], wa_ref[...].astype(BF16))
        acc = acc + _bdot(b_ref[rows, :], wb_ref[...].astype(BF16))
        o_ref[rows, :] = acc.astype(o_ref.dtype)


def _out_proj(a, b, w, layer):
    s, ka = a.shape
    kb = b.shape[1]
    n = w.shape[2]
    bm, bn = BIG_ROW_BLOCK, 512
    assert ka % kb == 0
    return pl.pallas_call(
        _out_proj_kernel,
        grid=(s // bm, n // bn),
        in_specs=[
            _resident_rows(bm, ka), _resident_rows(bm, kb),
            pl.BlockSpec((None, ka, bn), lambda i, j: (layer, 0, j)),
            pl.BlockSpec((None, kb, bn), lambda i, j: (layer, ka // kb, j)),
        ],
        out_specs=pl.BlockSpec((bm, bn), lambda i, j: (i, j)),
        out_shape=jax.ShapeDtypeStruct((s, n), BF16),
        compiler_params=_params(2),
        name="out_proj",
    )(a, b, w, w)


def _ffn_in_kernel(x_ref, wg_ref, wu_ref, cw_ref, o_ref, ext0_ref, ext1_ref, tails_ref):
    i, j = pl.program_id(0), pl.program_id(1)
    tail = jnp.where(i == 0, 0.0, tails_ref[j])
    for chunk, rows in enumerate(_row_chunks(x_ref.shape[0])):
        ext_ref = (ext0_ref, ext1_ref)[chunk % 2]
        x = x_ref[rows, :]
        g = _bdot(x, wg_ref[...].astype(BF16))
        u = _bdot(x, wu_ref[...].astype(BF16))
        gc, tail = _causal_conv3(g, tail, cw_ref, ext_ref)
        act = gc / (1.0 + jnp.exp(-gc))
        o_ref[rows, :] = (act * u).astype(o_ref.dtype)
    tails_ref[j] = tail


def _ffn_in(xn, w_gate, w_up, conv_w, layer):
    s, d = xn.shape
    bm, bn = BIG_ROW_BLOCK, V7X_MXU_COLS
    nj = D_FF // bn
    wspec = pl.BlockSpec((None, d, bn), lambda i, j: (layer, 0, j))
    ext = pltpu.VMEM((CHUNK_ROWS + HALO, bn), F32)
    return pl.pallas_call(
        _ffn_in_kernel,
        grid=(s // bm, nj),
        in_specs=[
            _resident_rows(bm, d),
            wspec, wspec,
            pl.BlockSpec((None, 3, bn), lambda i, j: (layer, 0, j)),
        ],
        out_specs=pl.BlockSpec((bm, bn), lambda i, j: (i, j)),
        out_shape=jax.ShapeDtypeStruct((s, D_FF), BF16),
        scratch_shapes=[ext, ext, pltpu.VMEM((nj, HALO, bn), F32)],
        compiler_params=_params(2),
        name="ffn_in",
    )(xn, w_gate, w_up, conv_w)


def _ffn_out_kernel(a_ref, w_ref, o_ref, acc_ref, *, kdim):
    k = pl.program_id(2)
    last = pl.num_programs(2) - 1
    bk = a_ref.shape[1]

    def step(first, final):
        valid = kdim - k * bk
        a_cols = lax.broadcasted_iota(jnp.int32, a_ref.shape, 1)
        a = jnp.where(a_cols < valid, a_ref[...], jnp.zeros_like(a_ref))
        w_rows = lax.broadcasted_iota(jnp.int32, (bk, FFN_OUT_COL_CHUNK), 0)
        for c in range(0, o_ref.shape[1], FFN_OUT_COL_CHUNK):
            cols = slice(c, c + FFN_OUT_COL_CHUNK)
            w = jnp.where(w_rows < valid, w_ref[:, cols], 0.0).astype(BF16)
            for rows in _row_chunks(a_ref.shape[0]):
                part = _bdot(a[rows, :], w)
                total = part if first else acc_ref[rows, cols] + part
                if final:
                    o_ref[rows, cols] = total.astype(o_ref.dtype)
                else:
                    acc_ref[rows, cols] = total

    pl.when(k == 0)(lambda: step(True, False))
    pl.when(jnp.logical_and(k > 0, k < last))(lambda: step(False, False))
    pl.when(k == last)(lambda: step(False, True))


FFN_OUT_COL_CHUNK = 512
FFN_OUT_K_BLOCK = 1024
FFN_OUT_COLS = 2048


def _ffn_out(a, w_down, layer):
    s, kdim = a.shape
    n = w_down.shape[2]
    bm, bk, bn = BIG_ROW_BLOCK, FFN_OUT_K_BLOCK, FFN_OUT_COLS
    return pl.pallas_call(
        functools.partial(_ffn_out_kernel, kdim=kdim),
        grid=(s // bm, n // bn, pl.cdiv(kdim, bk)),
        in_specs=[
            pl.BlockSpec((bm, bk), lambda i, j, k: (i, k)),
            pl.BlockSpec((None, bk, bn), lambda i, j, k: (layer, k, j)),
        ],
        out_specs=pl.BlockSpec((bm, bn), lambda i, j, k: (i, j)),
        out_shape=jax.ShapeDtypeStruct((s, n), BF16),
        scratch_shapes=[pltpu.VMEM((bm, bn), F32)],
        compiler_params=_params(3),
        name="ffn_out",
    )(a, w_down)


ATTN_IN_COLS = 512
HEADS_PER_ATTN_IN_BLOCK = ATTN_IN_COLS // HEAD_DIM
ATTN_IN_BLOCKS_PER_SECTION = GROUP_WIDTH // ATTN_IN_COLS
ATTN_IN_BLOCKS_PER_GROUP = 3 * ATTN_IN_BLOCKS_PER_SECTION
ATTN_IN_CHUNK_ROWS = 512


def _attn_in_kernel(x_ref, w_ref, cos_ref, sin_ref, o_ref):
    section = pl.program_id(1) // ATTN_IN_BLOCKS_PER_SECTION
    scale = jnp.where(section == 0, HEAD_DIM ** -0.5, 1.0).astype(F32)
    is_v = section == 2
    for chunk in range(x_ref.shape[0] // ATTN_IN_CHUNK_ROWS):
        rows = slice(chunk * ATTN_IN_CHUNK_ROWS, (chunk + 1) * ATTN_IN_CHUNK_ROWS)
        acc = _bdot(x_ref[rows, :], w_ref[...].astype(BF16))
        cos = cos_ref[rows, :] * scale
        sin = sin_ref[rows, :] * scale
        for h in range(HEADS_PER_ATTN_IN_BLOCK):
            cols = slice(h * HEAD_DIM, (h + 1) * HEAD_DIM)
            v = acc[:, cols]
            rot = pltpu.roll(v, HEAD_DIM // 2, axis=1)
            o_ref[rows, cols] = jnp.where(is_v, v, v * cos + rot * sin).astype(o_ref.dtype)


def _attn_in(xn, w_in, cos_tab, sin_tab, layer, group):
    s, d = xn.shape
    bm, bn = BIG_ROW_BLOCK, ATTN_IN_COLS
    nj = ATTN_IN_BLOCKS_PER_GROUP
    tab = pl.BlockSpec((None, bm, HEAD_DIM), lambda i, j: (group, i, 0))
    return pl.pallas_call(
        _attn_in_kernel,
        grid=(s // bm, nj),
        in_specs=[
            _resident_rows(bm, d),
            pl.BlockSpec((None, d, bn), lambda i, j: (layer, 0, group * nj + j)),
            tab, tab,
        ],
        out_specs=pl.BlockSpec((bm, bn), lambda i, j: (i, j)),
        out_shape=jax.ShapeDtypeStruct((s, 3 * GROUP_WIDTH), BF16),
        compiler_params=_params(2),
        name=f"attn_in_g{group}",
    )(xn, w_in, cos_tab, sin_tab)


def _window_attn_kernel(q_ref, k_ref, v_ref, o_ref, lse_ref, k2_ref, v2_ref):
    nb = pl.program_id(1)

    @pl.when(nb == 0)
    def _():
        k2_ref[...] = jnp.zeros_like(k2_ref)
        v2_ref[...] = jnp.zeros_like(v2_ref)

    own_half = nb % 2
    own_rows = pl.ds(pl.multiple_of(own_half * BLOCK, BLOCK), BLOCK)
    k2_ref[own_rows, :] = k_ref[...]
    v2_ref[own_rows, :] = v_ref[...]

    qi = lax.broadcasted_iota(jnp.int32, (BLOCK, 2 * BLOCK), 0)
    col = lax.broadcasted_iota(jnp.int32, (BLOCK, 2 * BLOCK), 1)
    ki = col & (BLOCK - 1)
    in_own_half = (col >> (BLOCK.bit_length() - 1)) == own_half
    valid = jnp.logical_or(
        jnp.logical_and(in_own_half, ki <= qi),
        jnp.logical_and(jnp.logical_not(in_own_half), jnp.logical_and(ki >= qi, nb > 0)))
    lane = lax.broadcasted_iota(jnp.int32, (BLOCK, V7X_LANES), 1)
    ones = jnp.ones((2 * BLOCK, HEAD_DIM), BF16)
    lse_all = jnp.zeros((BLOCK, V7X_LANES), F32)
    nt = (((1,), (1,)), ((), ()))
    for h in range(N_GROUP_HEADS):
        cols = slice(h * HEAD_DIM, (h + 1) * HEAD_DIM)
        s = lax.dot_general(q_ref[:, cols], k2_ref[:, cols], nt, preferred_element_type=F32)
        s = jnp.where(valid, s, NEG_INF)
        m = jnp.max(s, axis=-1, keepdims=True)
        p = jnp.exp(s - m).astype(BF16)
        ov = _bdot(p, jnp.concatenate([v2_ref[:, cols], ones], axis=1))
        den = ov[:, HEAD_DIM:]
        o_ref[:, cols] = (ov[:, :HEAD_DIM] / den).astype(o_ref.dtype)
        lse_all = jnp.where(lane == h, m + jnp.log(den), lse_all)
    lse_ref[...] = lse_all


def _window_attn(h_group, group):
    s = h_group.shape[0]
    _, d = DILATED_GROUPS[group]
    blocks_per_class = s // d // BLOCK
    spec = lambda sec: pl.BlockSpec(
        (BLOCK, GROUP_WIDTH), lambda r, nb: (r * blocks_per_class + nb, sec))
    return pl.pallas_call(
        _window_attn_kernel,
        grid=(d, blocks_per_class),
        in_specs=[spec(0), spec(1), spec(2)],
        out_specs=[
            pl.BlockSpec((BLOCK, GROUP_WIDTH), lambda r, nb: (r * blocks_per_class + nb, 0)),
            pl.BlockSpec((BLOCK, V7X_LANES), lambda r, nb: (r * blocks_per_class + nb, 0)),
        ],
        out_shape=[
            jax.ShapeDtypeStruct((s, GROUP_WIDTH), BF16),
            jax.ShapeDtypeStruct((s, V7X_LANES), F32),
        ],
        scratch_shapes=[pltpu.VMEM((2 * BLOCK, GROUP_WIDTH), BF16)] * 2,
        compiler_params=_params(2),
        name=f"window_attn_g{group}",
    )(h_group, h_group, h_group)


def _bf16_pieces(v):
    hi = v.astype(BF16)
    rest = v - hi.astype(F32)
    mid = rest.astype(BF16)
    lo = (rest - mid.astype(F32)).astype(BF16)
    return hi, mid, lo


def _merge_kernel(o0_ref, o1_ref, o2_ref, l0_ref, l1_ref, l2_ref, out_ref):
    outs = [o0_ref[...].astype(F32)]
    lses = [l0_ref[...]]
    for o_ref, l_ref in ((o1_ref, l1_ref), (o2_ref, l2_ref)):
        d = o_ref.shape[0]
        perm = _row_permutation(PERM_ROWS, d, False)
        o_cm = jnp.concatenate([o_ref[r] for r in range(d)], axis=0)
        l_cm = jnp.concatenate([l_ref[r] for r in range(d)], axis=0)
        outs.append(_bdot(perm, o_cm))
        hi, mid, lo = _bf16_pieces(l_cm)
        lses.append(_bdot(perm, hi) + _bdot(perm, mid) + _bdot(perm, lo))
    l0, l1, l2 = lses
    m = jnp.maximum(jnp.maximum(l0, l1), l2)
    e0, e1, e2 = jnp.exp(l0 - m), jnp.exp(l1 - m), jnp.exp(l2 - m)
    tot = e0 + e1 + e2
    a0, a1, a2 = e0 / tot, e1 / tot, e2 / tot
    for h in range(N_GROUP_HEADS):
        cols = slice(h * HEAD_DIM, (h + 1) * HEAD_DIM)
        merged = (a0[:, h:h + 1] * outs[0][:, cols] + a1[:, h:h + 1] * outs[1][:, cols]
                  + a2[:, h:h + 1] * outs[2][:, cols])
        out_ref[:, cols] = merged.astype(out_ref.dtype)


def _merge_groups(outs, lses):
    s = outs[0].shape[0]
    rows = PERM_ROWS

    def specs(width):
        out = [pl.BlockSpec((rows, width), lambda i: (i, 0))]
        for _, d in DILATED_GROUPS[1:]:
            out.append(pl.BlockSpec((d, rows // d, width), lambda i: (0, i, 0)))
        return out

    class_major = lambda a, d: a.reshape(d, s // d, a.shape[1])
    outs = [outs[0]] + [class_major(o, d) for o, (_, d) in zip(outs[1:], DILATED_GROUPS[1:])]
    lses = [lses[0]] + [class_major(l, d) for l, (_, d) in zip(lses[1:], DILATED_GROUPS[1:])]
    return pl.pallas_call(
        _merge_kernel,
        grid=(s // rows,),
        in_specs=specs(GROUP_WIDTH) + specs(V7X_LANES),
        out_specs=pl.BlockSpec((rows, GROUP_WIDTH), lambda i: (i, 0)),
        out_shape=jax.ShapeDtypeStruct((s, GROUP_WIDTH), BF16),
        compiler_params=_params(1),
        name="merge_groups",
    )(*outs, *lses)


def kernel(x, mem, positions, g_mem, w_mem_kv, g_mix_pre, g_mix_post, g_ffn_pre, g_ffn_post,
           w_conv_in, conv_mix_w, w_conv_out, w_attn_in, w_attn_out,
           w_ffn_gate, w_ffn_up, conv_ffn_w, w_ffn_down):
    b, s, d = x.shape
    assert b == 1 and s == SEQ and d == D_MODEL
    xs = x.reshape(s, d)
    gains = lambda g: g.reshape(DEPTH, 1, d)
    g_mix_pre, g_mix_post = gains(g_mix_pre), gains(g_mix_post)
    g_ffn_pre, g_ffn_post = gains(g_ffn_pre), gains(g_ffn_post)

    kv = _mem_kv(mem.reshape(MEM_TOKENS, d), g_mem, w_mem_kv)
    cos_tab, sin_tab = _rope_tables(positions)

    dilations = tuple(dil for _, dil in DILATED_GROUPS[1:])
    xn = _rmsnorm_bf16(xs, g_mix_pre, 0)
    xn_class_major = ()
    for i in range(DEPTH):
        j = i // 2
        if i % 2 == 0:
            y_mix = _conv_in(xn, w_conv_in, conv_mix_w, j)
            y_mem = _mem_attend(xn, w_conv_in, j, 3 * CONV_WIDTH // MEM_HEAD_DIM, kv)
            y = _out_proj(y_mix, y_mem, w_conv_out, j)
        else:
            lhs = [xn] + [xp.reshape(s, d) for xp in xn_class_major]
            parts = [
                _window_attn(_attn_in(lhs[g], w_attn_in, cos_tab, sin_tab, j, g), g)
                for g in range(N_GROUPS)]
            y_mix = _merge_groups([p[0] for p in parts], [p[1] for p in parts])
            y_mem = _mem_attend(xn, w_attn_in, j, N_SELF // MEM_HEAD_DIM, kv)
            y = _out_proj(y_mix, y_mem, w_attn_out, j)
        xs, xn = _residual_norm(y, xs, g_mix_post, i, g_ffn_pre, i)
        a = _ffn_in(xn, w_ffn_gate, w_ffn_up, conv_ffn_w, i)
        y = _ffn_out(a, w_ffn_down, i)
        if i + 1 == DEPTH:
            xs = _residual_norm(y, xs, g_ffn_post, i)
        elif (i + 1) % 2 == 1:
            xs, xn, *xn_class_major = _residual_norm(
                y, xs, g_ffn_post, i, g_mix_pre, i + 1, dilations=dilations)
        else:
            xs, xn = _residual_norm(y, xs, g_ffn_post, i, g_mix_pre, i + 1)
    return xs.reshape(b, s, d)
```

```python
import functools

import jax
import jax.numpy as jnp
from jax import lax
from jax.experimental import pallas as pl
from jax.experimental.pallas import tpu as pltpu

D_MODEL = 4096
SEQ = 8192
DEPTH = 2
HEAD_DIM = 128
N_GROUP_HEADS = 16
GROUP_WIDTH = N_GROUP_HEADS * HEAD_DIM
DILATED_GROUPS = ((128, 1), (512, 4), (2048, 16))
N_GROUPS = len(DILATED_GROUPS)
N_SELF = N_GROUPS * 3 * GROUP_WIDTH
BLOCK = 128
CONV_WIDTH = 3 * D_MODEL // 4
MEM_TOKENS = 256
MEM_HEADS = 4
MEM_HEAD_DIM = 256
MEM_WIDTH = MEM_HEADS * MEM_HEAD_DIM
D_FF = 11008
ROPE_THETA = 10000.0
EPS = 1e-6
NEG_INF = -1e30

V7X_SUBLANES = 8
V7X_LANES = 128
V7X_MXU_COLS = 256
V7X_VMEM_BYTES = 64 * 1024 * 1024
V7X_VMEM_LIMIT_BYTES = V7X_VMEM_BYTES - 2 * 1024 * 1024

BF16 = jnp.bfloat16
F32 = jnp.float32

ROW_BLOCK = 1024
BIG_ROW_BLOCK = 2048
CHUNK_ROWS = 512
NORM_ROWS = 512
PERM_ROWS = 256
HALO = V7X_SUBLANES


def _params(n_axes):
    return pltpu.CompilerParams(
        dimension_semantics=("arbitrary",) * n_axes,
        vmem_limit_bytes=V7X_VMEM_LIMIT_BYTES,
    )


def _rms_scale(v):
    return lax.rsqrt(jnp.mean(v * v, axis=-1, keepdims=True) + EPS)


def _bdot(a, b):
    return jnp.dot(a, b, preferred_element_type=F32)


def _norm_kernel(x_ref, g_ref, o_ref):
    x = x_ref[...]
    o_ref[...] = (x * _rms_scale(x) * g_ref[...]).astype(o_ref.dtype)


def _rmsnorm_bf16(x, g_all, layer):
    s, d = x.shape
    return pl.pallas_call(
        _norm_kernel,
        grid=(s // NORM_ROWS,),
        in_specs=[
            pl.BlockSpec((NORM_ROWS, d), lambda i: (i, 0)),
            pl.BlockSpec((None, 1, d), lambda i: (layer, 0, 0)),
        ],
        out_specs=pl.BlockSpec((NORM_ROWS, d), lambda i: (i, 0)),
        out_shape=jax.ShapeDtypeStruct((s, d), BF16),
        compiler_params=_params(1),
        name="rmsnorm_bf16",
    )(x, g_all)


def _residual_kernel(y_ref, x_ref, gpost_ref, gnext_ref, xo_ref, xn_ref):
    y = y_ref[...].astype(F32)
    xnew = x_ref[...] + y * _rms_scale(y) * gpost_ref[...]
    xo_ref[...] = xnew
    xn_ref[...] = (xnew * _rms_scale(xnew) * gnext_ref[...]).astype(xn_ref.dtype)


def _row_permutation(rows, d, to_class_major):
    n = rows // d
    out_row = lax.broadcasted_iota(jnp.int32, (rows, rows), 0)
    in_row = lax.broadcasted_iota(jnp.int32, (rows, rows), 1)
    if to_class_major:
        src = (out_row & (n - 1)) * d + (out_row >> (n.bit_length() - 1))
    else:
        src = (out_row & (d - 1)) * n + (out_row >> (d.bit_length() - 1))
    return (in_row == src).astype(BF16)


def _residual_dilated_kernel(y_ref, x_ref, gpost_ref, gnext_ref, xo_ref, xn_ref, *xp_refs):
    y = y_ref[...].astype(F32)
    xnew = x_ref[...] + y * _rms_scale(y) * gpost_ref[...]
    xo_ref[...] = xnew
    xn = (xnew * _rms_scale(xnew) * gnext_ref[...]).astype(BF16)
    xn_ref[...] = xn
    for xp_ref in xp_refs:
        d, n = xp_ref.shape[0], xp_ref.shape[1]
        xp = _bdot(_row_permutation(PERM_ROWS, d, True), xn).astype(BF16)
        for r in range(d):
            xp_ref[r] = xp[r * n:(r + 1) * n, :]


def _residual_last_kernel(y_ref, x_ref, gpost_ref, xo_ref):
    y = y_ref[...].astype(F32)
    xo_ref[...] = x_ref[...] + y * _rms_scale(y) * gpost_ref[...]


def _residual_norm(y, x, g_post, post_layer, g_next=None, next_layer=None, dilations=()):
    s, d = x.shape
    row = pl.BlockSpec((NORM_ROWS, d), lambda i: (i, 0))
    gspec = lambda layer: pl.BlockSpec((None, 1, d), lambda i: (layer, 0, 0))
    if dilations:
        row = pl.BlockSpec((PERM_ROWS, d), lambda i: (i, 0))
        return pl.pallas_call(
            _residual_dilated_kernel,
            grid=(s // PERM_ROWS,),
            in_specs=[row, row, gspec(post_layer), gspec(next_layer)],
            out_specs=[row, row] + [
                pl.BlockSpec((dil, PERM_ROWS // dil, d), lambda i: (0, i, 0)) for dil in dilations],
            out_shape=[jax.ShapeDtypeStruct((s, d), F32), jax.ShapeDtypeStruct((s, d), BF16)] + [
                jax.ShapeDtypeStruct((dil, s // dil, d), BF16) for dil in dilations],
            compiler_params=_params(1),
            name="residual_norm_dilated",
        )(y, x, g_post, g_next)
    if g_next is None:
        return pl.pallas_call(
            _residual_last_kernel,
            grid=(s // NORM_ROWS,),
            in_specs=[row, row, gspec(post_layer)],
            out_specs=row,
            out_shape=jax.ShapeDtypeStruct((s, d), F32),
            compiler_params=_params(1),
            name="residual_last",
        )(y, x, g_post)
    return pl.pallas_call(
        _residual_kernel,
        grid=(s // NORM_ROWS,),
        in_specs=[row, row, gspec(post_layer), gspec(next_layer)],
        out_specs=[row, row],
        out_shape=[jax.ShapeDtypeStruct((s, d), F32), jax.ShapeDtypeStruct((s, d), BF16)],
        compiler_params=_params(1),
        name="residual_norm",
    )(y, x, g_post, g_next)


def _rope_table_kernel(pos_ref, freq_ref, sign_ref, cos_ref, sin_ref):
    ang = pos_ref[...].astype(F32) * freq_ref[...]
    cos_ref[...] = jnp.cos(ang)
    sin_ref[...] = jnp.sin(ang) * sign_ref[...]


def _rope_tables(positions):
    s = positions.shape[-1]
    half = HEAD_DIM // 2
    inv_freq = ROPE_THETA ** (-jnp.arange(half, dtype=F32) * 2.0 / HEAD_DIM)
    freq = jnp.concatenate([inv_freq, inv_freq]).reshape(1, HEAD_DIM)
    sign = jnp.concatenate([-jnp.ones((half,), F32), jnp.ones((half,), F32)]).reshape(1, HEAD_DIM)
    pos = jnp.stack([positions.reshape(s // d, d).T.reshape(s, 1) for _, d in DILATED_GROUPS])
    rows = 1024
    tab = pl.BlockSpec((None, rows, HEAD_DIM), lambda g, i: (g, i, 0))
    const = pl.BlockSpec((1, HEAD_DIM), lambda g, i: (0, 0))
    return pl.pallas_call(
        _rope_table_kernel,
        grid=(N_GROUPS, s // rows),
        in_specs=[pl.BlockSpec((None, rows, 1), lambda g, i: (g, i, 0)), const, const],
        out_specs=[tab, tab],
        out_shape=[jax.ShapeDtypeStruct((N_GROUPS, s, HEAD_DIM), F32)] * 2,
        compiler_params=_params(2),
        name="rope_tables",
    )(pos, freq, sign)


def _mem_kv_kernel(mem_ref, g_ref, w_ref, o_ref):
    m = mem_ref[...]
    mn = (m * _rms_scale(m) * g_ref[...]).astype(BF16)
    o_ref[...] = _bdot(mn, w_ref[...].astype(BF16)).astype(o_ref.dtype)


def _mem_kv(mem, g_mem, w_mem_kv):
    m, d = mem.shape
    n = w_mem_kv.shape[1]
    bn = 512
    return pl.pallas_call(
        _mem_kv_kernel,
        grid=(n // bn,),
        in_specs=[
            pl.BlockSpec((m, d), lambda j: (0, 0)),
            pl.BlockSpec((1, d), lambda j: (0, 0)),
            pl.BlockSpec((d, bn), lambda j: (0, j)),
        ],
        out_specs=pl.BlockSpec((m, bn), lambda j: (0, j)),
        out_shape=jax.ShapeDtypeStruct((m, n), BF16),
        compiler_params=_params(1),
        name="mem_kv",
    )(mem, g_mem.reshape(1, d), w_mem_kv)


def _causal_conv3(v, prev_tail, cw_ref, ext_ref):
    bm = v.shape[0]
    ext_ref[0:HALO, :] = prev_tail
    ext_ref[HALO:HALO + bm, :] = v
    v1 = ext_ref[HALO - 1:HALO - 1 + bm, :]
    v2 = ext_ref[HALO - 2:HALO - 2 + bm, :]
    conv = cw_ref[0:1, :] * v2 + cw_ref[1:2, :] * v1 + cw_ref[2:3, :] * v
    return conv, v[bm - HALO:bm, :]


def _row_chunks(total):
    return [slice(a, a + CHUNK_ROWS) for a in range(0, total, CHUNK_ROWS)]


def _resident_rows(bm, k, buffers=2):
    return pl.BlockSpec((bm, k), lambda i, j: (i, 0), pipeline_mode=pl.Buffered(buffers))


def _conv_in_kernel(x_ref, wb_ref, wc_ref, wu_ref, cw_ref, o_ref, ext0_ref, ext1_ref, tails_ref):
    i, j = pl.program_id(0), pl.program_id(1)
    tail = jnp.where(i == 0, 0.0, tails_ref[j])
    for chunk in range(x_ref.shape[0] // CHUNK_ROWS):
        ext_ref = (ext0_ref, ext1_ref)[chunk % 2]
        rows = slice(chunk * CHUNK_ROWS, (chunk + 1) * CHUNK_ROWS)
        x = x_ref[rows, :]
        b = _bdot(x, wb_ref[...].astype(BF16))
        c = _bdot(x, wc_ref[...].astype(BF16))
        u = _bdot(x, wu_ref[...].astype(BF16))
        conv, tail = _causal_conv3(c * u, tail, cw_ref, ext_ref)
        o_ref[rows, :] = (b * conv).astype(o_ref.dtype)
    tails_ref[j] = tail


def _conv_in(xn, w_in, conv_w, layer):
    s, d = xn.shape
    bm, bn = BIG_ROW_BLOCK, V7X_MXU_COLS
    nj = CONV_WIDTH // bn
    wspec = lambda off: pl.BlockSpec((None, d, bn), lambda i, j: (layer, 0, j + off))
    ext = pltpu.VMEM((CHUNK_ROWS + HALO, bn), F32)
    return pl.pallas_call(
        _conv_in_kernel,
        grid=(s // bm, nj),
        in_specs=[
            _resident_rows(bm, d, buffers=1),
            wspec(0), wspec(nj), wspec(2 * nj),
            pl.BlockSpec((None, 3, bn), lambda i, j: (layer, 0, j)),
        ],
        out_specs=pl.BlockSpec((bm, bn), lambda i, j: (i, j)),
        out_shape=jax.ShapeDtypeStruct((s, CONV_WIDTH), BF16),
        scratch_shapes=[ext, ext, pltpu.VMEM((nj, HALO, bn), F32)],
        compiler_params=_params(2),
        name="conv_in",
    )(xn, w_in, w_in, w_in, conv_w)


def _mem_q_kernel(x_ref, wq_ref, k_ref, v_ref, o_ref):
    q = _bdot(x_ref[...], wq_ref[...].astype(BF16)) * (MEM_HEAD_DIM ** -0.5)
    s = lax.dot_general(q.astype(BF16), k_ref[...], (((1,), (1,)), ((), ())),
                        preferred_element_type=F32)
    m = jnp.max(s, axis=-1, keepdims=True)
    p = jnp.exp(s - m)
    den = jnp.sum(p, axis=-1, keepdims=True)
    o = _bdot((p / den).astype(BF16), v_ref[...])
    o_ref[...] = o.astype(o_ref.dtype)


def _mem_attend(xn, w_in, layer, q_col_block, kv):
    s, d = xn.shape
    bm, bn = BIG_ROW_BLOCK, MEM_HEAD_DIM
    return pl.pallas_call(
        _mem_q_kernel,
        grid=(s // bm, MEM_HEADS),
        in_specs=[
            _resident_rows(bm, d),
            pl.BlockSpec((None, d, bn), lambda i, h: (layer, 0, q_col_block + h)),
            pl.BlockSpec((MEM_TOKENS, bn), lambda i, h: (0, h)),
            pl.BlockSpec((MEM_TOKENS, bn), lambda i, h: (0, MEM_HEADS + h)),
        ],
        out_specs=pl.BlockSpec((bm, bn), lambda i, h: (i, h)),
        out_shape=jax.ShapeDtypeStruct((s, MEM_WIDTH), BF16),
        compiler_params=_params(2),
        name="mem_attend",
    )(xn, w_in, kv, kv)


def _out_proj_kernel(a_ref, b_ref, wa_ref, wb_ref, o_ref):
    for chunk in range(a_ref.shape[0] // CHUNK_ROWS):
        rows = slice(chunk * CHUNK_ROWS, (chunk + 1) * CHUNK_ROWS)
        acc = _bdot(a_ref[rows, :], wa_ref[...].astype(BF16))
        acc = acc + _bdot(b_ref[rows, :], wb_ref[...].astype(BF16))
        o_ref[rows, :] = acc.astype(o_ref.dtype)


def _out_proj(a, b, w, layer):
    s, ka = a.shape
    kb = b.shape[1]
    n = w.shape[2]
    bm, bn = BIG_ROW_BLOCK, 512
    assert ka % kb == 0
    return pl.pallas_call(
        _out_proj_kernel,
        grid=(s // bm, n // bn),
        in_specs=[
            _resident_rows(bm, ka), _resident_rows(bm, kb),
            pl.BlockSpec((None, ka, bn), lambda i, j: (layer, 0, j)),
            pl.BlockSpec((None, kb, bn), lambda i, j: (layer, ka // kb, j)),
        ],
        out_specs=pl.BlockSpec((bm, bn), lambda i, j: (i, j)),
        out_shape=jax.ShapeDtypeStruct((s, n), BF16),
        compiler_params=_params(2),
        name="out_proj",
    )(a, b, w, w)


def _ffn_in_kernel(x_ref, wg0_ref, wg1_ref, wu0_ref, wu1_ref, cw0_ref, cw1_ref, o_ref,
                   ext0_ref, ext1_ref, tails_ref, *, n_col_blocks):
    i, j = pl.program_id(0), pl.program_id(1)
    bn = wg0_ref.shape[1]
    operands = ((wg0_ref, wu0_ref, cw0_ref), (wg1_ref, wu1_ref, cw1_ref))

    def column_blocks(count):
        chunk = 0
        for half, (wg_ref, wu_ref, cw_ref) in enumerate(operands[:count]):
            col_block = 2 * j + half
            cols = slice(half * bn, (half + 1) * bn)
            tail = jnp.where(i == 0, 0.0, tails_ref[col_block])
            for rows in _row_chunks(x_ref.shape[0]):
                ext_ref = (ext0_ref, ext1_ref)[chunk % 2]
                chunk += 1
                x = x_ref[rows, :]
                g = _bdot(x, wg_ref[...].astype(BF16))
                u = _bdot(x, wu_ref[...].astype(BF16))
                gc, tail = _causal_conv3(g, tail, cw_ref, ext_ref)
                act = gc / (1.0 + jnp.exp(-gc))
                o_ref[rows, cols] = (act * u).astype(o_ref.dtype)
            tails_ref[col_block] = tail

    paired = 2 * j + 1 < n_col_blocks
    pl.when(paired)(lambda: column_blocks(2))
    pl.when(jnp.logical_not(paired))(lambda: column_blocks(1))


def _ffn_in(xn, w_gate, w_up, conv_w, layer):
    s, d = xn.shape
    bm, bn = BIG_ROW_BLOCK, V7X_MXU_COLS
    n_col_blocks = D_FF // bn
    col_block = lambda j, half: jnp.minimum(2 * j + half, n_col_blocks - 1)
    wspec = lambda half: pl.BlockSpec((None, d, bn), lambda i, j: (layer, 0, col_block(j, half)))
    cwspec = lambda half: pl.BlockSpec((None, 3, bn), lambda i, j: (layer, 0, col_block(j, half)))
    ext = pltpu.VMEM((CHUNK_ROWS + HALO, bn), F32)
    return pl.pallas_call(
        functools.partial(_ffn_in_kernel, n_col_blocks=n_col_blocks),
        grid=(s // bm, pl.cdiv(n_col_blocks, 2)),
        in_specs=[
            _resident_rows(bm, d, buffers=1),
            wspec(0), wspec(1), wspec(0), wspec(1), cwspec(0), cwspec(1),
        ],
        out_specs=pl.BlockSpec((bm, 2 * bn), lambda i, j: (i, j)),
        out_shape=jax.ShapeDtypeStruct((s, D_FF), BF16),
        scratch_shapes=[ext, ext, pltpu.VMEM((n_col_blocks, HALO, bn), F32)],
        compiler_params=_params(2),
        name="ffn_in",
    )(xn, w_gate, w_gate, w_up, w_up, conv_w, conv_w)


def _ffn_out_kernel(a_ref, w_ref, o_ref, acc_ref, *, kdim):
    k = pl.program_id(2)
    last = pl.num_programs(2) - 1
    bk = a_ref.shape[1]

    def step(first, final):
        valid = kdim - k * bk
        a_cols = lax.broadcasted_iota(jnp.int32, a_ref.shape, 1)
        a = jnp.where(a_cols < valid, a_ref[...], jnp.zeros_like(a_ref))
        w_rows = lax.broadcasted_iota(jnp.int32, (bk, FFN_OUT_COL_CHUNK), 0)
        for c in range(0, o_ref.shape[1], FFN_OUT_COL_CHUNK):
            cols = slice(c, c + FFN_OUT_COL_CHUNK)
            w = jnp.where(w_rows < valid, w_ref[:, cols], 0.0).astype(BF16)
            for rows in _row_chunks(a_ref.shape[0]):
                part = _bdot(a[rows, :], w)
                total = part if first else acc_ref[rows, cols] + part
                if final:
                    o_ref[rows, cols] = total.astype(o_ref.dtype)
                else:
                    acc_ref[rows, cols] = total

    pl.when(k == 0)(lambda: step(True, False))
    pl.when(jnp.logical_and(k > 0, k < last))(lambda: step(False, False))
    pl.when(k == last)(lambda: step(False, True))


FFN_OUT_COL_CHUNK = 512
FFN_OUT_K_BLOCK = 1024
FFN_OUT_COLS = 2048


def _ffn_out(a, w_down, layer):
    s, kdim = a.shape
    n = w_down.shape[2]
    bm, bk, bn = BIG_ROW_BLOCK, FFN_OUT_K_BLOCK, FFN_OUT_COLS
    return pl.pallas_call(
        functools.partial(_ffn_out_kernel, kdim=kdim),
        grid=(s // bm, n // bn, pl.cdiv(kdim, bk)),
        in_specs=[
            pl.BlockSpec((bm, bk), lambda i, j, k: (i, k)),
            pl.BlockSpec((None, bk, bn), lambda i, j, k: (layer, k, j)),
        ],
        out_specs=pl.BlockSpec((bm, bn), lambda i, j, k: (i, j)),
        out_shape=jax.ShapeDtypeStruct((s, n), BF16),
        scratch_shapes=[pltpu.VMEM((bm, bn), F32)],
        compiler_params=_params(3),
        name="ffn_out",
    )(a, w_down)


ATTN_IN_COLS = 512
HEADS_PER_ATTN_IN_BLOCK = ATTN_IN_COLS // HEAD_DIM
ATTN_IN_BLOCKS_PER_SECTION = GROUP_WIDTH // ATTN_IN_COLS
ATTN_IN_BLOCKS_PER_GROUP = 3 * ATTN_IN_BLOCKS_PER_SECTION
ATTN_IN_CHUNK_ROWS = 512


def _attn_in_kernel(x_ref, w_ref, cos_ref, sin_ref, o_ref):
    section = pl.program_id(1) // ATTN_IN_BLOCKS_PER_SECTION
    scale = jnp.where(section == 0, HEAD_DIM ** -0.5, 1.0).astype(F32)
    is_v = section == 2
    for chunk in range(x_ref.shape[0] // ATTN_IN_CHUNK_ROWS):
        rows = slice(chunk * ATTN_IN_CHUNK_ROWS, (chunk + 1) * ATTN_IN_CHUNK_ROWS)
        acc = _bdot(x_ref[rows, :], w_ref[...].astype(BF16))
        cos = cos_ref[rows, :] * scale
        sin = sin_ref[rows, :] * scale
        for h in range(HEADS_PER_ATTN_IN_BLOCK):
            cols = slice(h * HEAD_DIM, (h + 1) * HEAD_DIM)
            v = acc[:, cols]
            rot = pltpu.roll(v, HEAD_DIM // 2, axis=1)
            o_ref[rows, cols] = jnp.where(is_v, v, v * cos + rot * sin).astype(o_ref.dtype)


def _attn_in(xn, w_in, cos_tab, sin_tab, layer, group):
    s, d = xn.shape
    bm, bn = BIG_ROW_BLOCK, ATTN_IN_COLS
    nj = ATTN_IN_BLOCKS_PER_GROUP
    tab = pl.BlockSpec((None, bm, HEAD_DIM), lambda i, j: (group, i, 0))
    return pl.pallas_call(
        _attn_in_kernel,
        grid=(s // bm, nj),
        in_specs=[
            _resident_rows(bm, d),
            pl.BlockSpec((None, d, bn), lambda i, j: (layer, 0, group * nj + j)),
            tab, tab,
        ],
        out_specs=pl.BlockSpec((bm, bn), lambda i, j: (i, j)),
        out_shape=jax.ShapeDtypeStruct((s, 3 * GROUP_WIDTH), BF16),
        compiler_params=_params(2),
        name=f"attn_in_g{group}",
    )(xn, w_in, cos_tab, sin_tab)


def _window_attn_kernel(q_ref, k_ref, v_ref, o_ref, lse_ref, k2_ref, v2_ref):
    nb = pl.program_id(1)

    @pl.when(nb == 0)
    def _():
        k2_ref[...] = jnp.zeros_like(k2_ref)
        v2_ref[...] = jnp.zeros_like(v2_ref)

    own_half = nb % 2
    own_rows = pl.ds(pl.multiple_of(own_half * BLOCK, BLOCK), BLOCK)
    k2_ref[own_rows, :] = k_ref[...]
    v2_ref[own_rows, :] = v_ref[...]

    qi = lax.broadcasted_iota(jnp.int32, (BLOCK, 2 * BLOCK), 0)
    col = lax.broadcasted_iota(jnp.int32, (BLOCK, 2 * BLOCK), 1)
    ki = col & (BLOCK - 1)
    in_own_half = (col >> (BLOCK.bit_length() - 1)) == own_half
    valid = jnp.logical_or(
        jnp.logical_and(in_own_half, ki <= qi),
        jnp.logical_and(jnp.logical_not(in_own_half), jnp.logical_and(ki >= qi, nb > 0)))
    lane = lax.broadcasted_iota(jnp.int32, (BLOCK, V7X_LANES), 1)
    ones = jnp.ones((2 * BLOCK, HEAD_DIM), BF16)
    lse_all = jnp.zeros((BLOCK, V7X_LANES), F32)
    nt = (((1,), (1,)), ((), ()))
    for h in range(N_GROUP_HEADS):
        cols = slice(h * HEAD_DIM, (h + 1) * HEAD_DIM)
        s = lax.dot_general(q_ref[:, cols], k2_ref[:, cols], nt, preferred_element_type=F32)
        s = jnp.where(valid, s, NEG_INF)
        m = jnp.max(s, axis=-1, keepdims=True)
        p = jnp.exp(s - m).astype(BF16)
        ov = _bdot(p, jnp.concatenate([v2_ref[:, cols], ones], axis=1))
        den = ov[:, HEAD_DIM:]
        o_ref[:, cols] = (ov[:, :HEAD_DIM] / den).astype(o_ref.dtype)
        lse_all = jnp.where(lane == h, m + jnp.log(den), lse_all)
    lse_ref[...] = lse_all


def _window_attn(h_group, group):
    s = h_group.shape[0]
    _, d = DILATED_GROUPS[group]
    blocks_per_class = s // d // BLOCK
    spec = lambda sec: pl.BlockSpec(
        (BLOCK, GROUP_WIDTH), lambda r, nb: (r * blocks_per_class + nb, sec))
    return pl.pallas_call(
        _window_attn_kernel,
        grid=(d, blocks_per_class),
        in_specs=[spec(0), spec(1), spec(2)],
        out_specs=[
            pl.BlockSpec((BLOCK, GROUP_WIDTH), lambda r, nb: (r * blocks_per_class + nb, 0)),
            pl.BlockSpec((BLOCK, V7X_LANES), lambda r, nb: (r * blocks_per_class + nb, 0)),
        ],
        out_shape=[
            jax.ShapeDtypeStruct((s, GROUP_WIDTH), BF16),
            jax.ShapeDtypeStruct((s, V7X_LANES), F32),
        ],
        scratch_shapes=[pltpu.VMEM((2 * BLOCK, GROUP_WIDTH), BF16)] * 2,
        compiler_params=_params(2),
        name=f"window_attn_g{group}",
    )(h_group, h_group, h_group)


def _bf16_pieces(v):
    hi = v.astype(BF16)
    rest = v - hi.astype(F32)
    mid = rest.astype(BF16)
    lo = (rest - mid.astype(F32)).astype(BF16)
    return hi, mid, lo


def _merge_kernel(o0_ref, o1_ref, o2_ref, l0_ref, l1_ref, l2_ref, out_ref):
    outs = [o0_ref[...].astype(F32)]
    lses = [l0_ref[...]]
    for o_ref, l_ref in ((o1_ref, l1_ref), (o2_ref, l2_ref)):
        d = o_ref.shape[0]
        perm = _row_permutation(PERM_ROWS, d, False)
        o_cm = jnp.concatenate([o_ref[r] for r in range(d)], axis=0)
        l_cm = jnp.concatenate([l_ref[r] for r in range(d)], axis=0)
        outs.append(_bdot(perm, o_cm))
        hi, mid, lo = _bf16_pieces(l_cm)
        lses.append(_bdot(perm, hi) + _bdot(perm, mid) + _bdot(perm, lo))
    l0, l1, l2 = lses
    m = jnp.maximum(jnp.maximum(l0, l1), l2)
    e0, e1, e2 = jnp.exp(l0 - m), jnp.exp(l1 - m), jnp.exp(l2 - m)
    tot = e0 + e1 + e2
    a0, a1, a2 = e0 / tot, e1 / tot, e2 / tot
    for h in range(N_GROUP_HEADS):
        cols = slice(h * HEAD_DIM, (h + 1) * HEAD_DIM)
        merged = (a0[:, h:h + 1] * outs[0][:, cols] + a1[:, h:h + 1] * outs[1][:, cols]
                  + a2[:, h:h + 1] * outs[2][:, cols])
        out_ref[:, cols] = merged.astype(out_ref.dtype)


def _merge_groups(outs, lses):
    s = outs[0].shape[0]
    rows = PERM_ROWS

    def specs(width):
        out = [pl.BlockSpec((rows, width), lambda i: (i, 0))]
        for _, d in DILATED_GROUPS[1:]:
            out.append(pl.BlockSpec((d, rows // d, width), lambda i: (0, i, 0)))
        return out

    class_major = lambda a, d: a.reshape(d, s // d, a.shape[1])
    outs = [outs[0]] + [class_major(o, d) for o, (_, d) in zip(outs[1:], DILATED_GROUPS[1:])]
    lses = [lses[0]] + [class_major(l, d) for l, (_, d) in zip(lses[1:], DILATED_GROUPS[1:])]
    return pl.pallas_call(
        _merge_kernel,
        grid=(s // rows,),
        in_specs=specs(GROUP_WIDTH) + specs(V7X_LANES),
        out_specs=pl.BlockSpec((rows, GROUP_WIDTH), lambda i: (i, 0)),
        out_shape=jax.ShapeDtypeStruct((s, GROUP_WIDTH), BF16),
        compiler_params=_params(1),
        name="merge_groups",
    )(*outs, *lses)


def kernel(x, mem, positions, g_mem, w_mem_kv, g_mix_pre, g_mix_post, g_ffn_pre, g_ffn_post,
           w_conv_in, conv_mix_w, w_conv_out, w_attn_in, w_attn_out,
           w_ffn_gate, w_ffn_up, conv_ffn_w, w_ffn_down):
    b, s, d = x.shape
    assert b == 1 and s == SEQ and d == D_MODEL
    xs = x.reshape(s, d)
    gains = lambda g: g.reshape(DEPTH, 1, d)
    g_mix_pre, g_mix_post = gains(g_mix_pre), gains(g_mix_post)
    g_ffn_pre, g_ffn_post = gains(g_ffn_pre), gains(g_ffn_post)

    kv = _mem_kv(mem.reshape(MEM_TOKENS, d), g_mem, w_mem_kv)
    cos_tab, sin_tab = _rope_tables(positions)

    dilations = tuple(dil for _, dil in DILATED_GROUPS[1:])
    xn = _rmsnorm_bf16(xs, g_mix_pre, 0)
    xn_class_major = ()
    for i in range(DEPTH):
        j = i // 2
        if i % 2 == 0:
            y_mix = _conv_in(xn, w_conv_in, conv_mix_w, j)
            y_mem = _mem_attend(xn, w_conv_in, j, 3 * CONV_WIDTH // MEM_HEAD_DIM, kv)
            y = _out_proj(y_mix, y_mem, w_conv_out, j)
        else:
            lhs = [xn] + [xp.reshape(s, d) for xp in xn_class_major]
            parts = [
                _window_attn(_attn_in(lhs[g], w_attn_in, cos_tab, sin_tab, j, g), g)
                for g in range(N_GROUPS)]
            y_mix = _merge_groups([p[0] for p in parts], [p[1] for p in parts])
            y_mem = _mem_attend(xn, w_attn_in, j, N_SELF // MEM_HEAD_DIM, kv)
            y = _out_proj(y_mix, y_mem, w_attn_out, j)
        xs, xn = _residual_norm(y, xs, g_mix_post, i, g_ffn_pre, i)
        a = _ffn_in(xn, w_ffn_gate, w_ffn_up, conv_ffn_w, i)
        y = _ffn_out(a, w_ffn_down, i)
        if i + 1 == DEPTH:
            xs = _residual_norm(y, xs, g_ffn_post, i)
        elif (i + 1) % 2 == 1:
            xs, xn, *xn_class_major = _residual_norm(
                y, xs, g_ffn_post, i, g_mix_pre, i + 1, dilations=dilations)
        else:
            xs, xn = _residual_norm(y, xs, g_ffn_post, i, g_mix_pre, i + 1)
    return xs.reshape(b, s, d)
```

```python
import functools

import jax
import jax.numpy as jnp
from jax import lax
from jax.experimental import pallas as pl
from jax.experimental.pallas import tpu as pltpu

D_MODEL = 4096
SEQ = 8192
DEPTH = 2
HEAD_DIM = 128
N_GROUP_HEADS = 16
GROUP_WIDTH = N_GROUP_HEADS * HEAD_DIM
DILATED_GROUPS = ((128, 1), (512, 4), (2048, 16))
N_GROUPS = len(DILATED_GROUPS)
N_SELF = N_GROUPS * 3 * GROUP_WIDTH
BLOCK = 128
CONV_WIDTH = 3 * D_MODEL // 4
MEM_TOKENS = 256
MEM_HEADS = 4
MEM_HEAD_DIM = 256
MEM_WIDTH = MEM_HEADS * MEM_HEAD_DIM
D_FF = 11008
ROPE_THETA = 10000.0
EPS = 1e-6
NEG_INF = -1e30

V7X_SUBLANES = 8
V7X_LANES = 128
V7X_MXU_COLS = 256
V7X_VMEM_BYTES = 64 * 1024 * 1024
V7X_VMEM_LIMIT_BYTES = V7X_VMEM_BYTES - 2 * 1024 * 1024

BF16 = jnp.bfloat16
F32 = jnp.float32

ROW_BLOCK = 2048
CHUNK_ROWS = 512
OUT_PROJ_COLS = 512
NORM_ROWS = 512
PERM_ROWS = 256
HALO = V7X_SUBLANES


def _params(n_axes):
    return pltpu.CompilerParams(
        dimension_semantics=("arbitrary",) * n_axes,
        vmem_limit_bytes=V7X_VMEM_LIMIT_BYTES,
    )


def _rms_scale(v):
    return lax.rsqrt(jnp.mean(v * v, axis=-1, keepdims=True) + EPS)


def _bdot(a, b):
    return jnp.dot(a, b, preferred_element_type=F32)


def _norm_kernel(x_ref, g_ref, o_ref):
    x = x_ref[...]
    o_ref[...] = (x * _rms_scale(x) * g_ref[...]).astype(o_ref.dtype)


def _rmsnorm_bf16(x, g_all, layer):
    s, d = x.shape
    return pl.pallas_call(
        _norm_kernel,
        grid=(s // NORM_ROWS,),
        in_specs=[
            pl.BlockSpec((NORM_ROWS, d), lambda i: (i, 0)),
            pl.BlockSpec((None, 1, d), lambda i: (layer, 0, 0)),
        ],
        out_specs=pl.BlockSpec((NORM_ROWS, d), lambda i: (i, 0)),
        out_shape=jax.ShapeDtypeStruct((s, d), BF16),
        compiler_params=_params(1),
        name="rmsnorm_bf16",
    )(x, g_all)


def _residual_kernel(y_ref, x_ref, gpost_ref, gnext_ref, xo_ref, xn_ref):
    y = y_ref[...].astype(F32)
    xnew = x_ref[...] + y * _rms_scale(y) * gpost_ref[...]
    xo_ref[...] = xnew
    xn_ref[...] = (xnew * _rms_scale(xnew) * gnext_ref[...]).astype(xn_ref.dtype)


def _row_permutation(rows, d, to_class_major):
    n = rows // d
    out_row = lax.broadcasted_iota(jnp.int32, (rows, rows), 0)
    in_row = lax.broadcasted_iota(jnp.int32, (rows, rows), 1)
    if to_class_major:
        src = (out_row & (n - 1)) * d + (out_row >> (n.bit_length() - 1))
    else:
        src = (out_row & (d - 1)) * n + (out_row >> (d.bit_length() - 1))
    return (in_row == src).astype(BF16)


def _residual_dilated_kernel(y_ref, x_ref, gpost_ref, gnext_ref, xo_ref, xn_ref, *xp_refs):
    y = y_ref[...].astype(F32)
    xnew = x_ref[...] + y * _rms_scale(y) * gpost_ref[...]
    xo_ref[...] = xnew
    xn = (xnew * _rms_scale(xnew) * gnext_ref[...]).astype(BF16)
    xn_ref[...] = xn
    for xp_ref in xp_refs:
        d, n = xp_ref.shape[0], xp_ref.shape[1]
        xp = _bdot(_row_permutation(PERM_ROWS, d, True), xn).astype(BF16)
        for r in range(d):
            xp_ref[r] = xp[r * n:(r + 1) * n, :]


def _residual_last_kernel(y_ref, x_ref, gpost_ref, xo_ref):
    y = y_ref[...].astype(F32)
    xo_ref[...] = x_ref[...] + y * _rms_scale(y) * gpost_ref[...]


def _residual_norm(y, x, g_post, post_layer, g_next=None, next_layer=None, dilations=()):
    s, d = x.shape
    row = pl.BlockSpec((NORM_ROWS, d), lambda i: (i, 0))
    gspec = lambda layer: pl.BlockSpec((None, 1, d), lambda i: (layer, 0, 0))
    if dilations:
        row = pl.BlockSpec((PERM_ROWS, d), lambda i: (i, 0))
        return pl.pallas_call(
            _residual_dilated_kernel,
            grid=(s // PERM_ROWS,),
            in_specs=[row, row, gspec(post_layer), gspec(next_layer)],
            out_specs=[row, row] + [
                pl.BlockSpec((dil, PERM_ROWS // dil, d), lambda i: (0, i, 0)) for dil in dilations],
            out_shape=[jax.ShapeDtypeStruct((s, d), F32), jax.ShapeDtypeStruct((s, d), BF16)] + [
                jax.ShapeDtypeStruct((dil, s // dil, d), BF16) for dil in dilations],
            compiler_params=_params(1),
            name="residual_norm_dilated",
        )(y, x, g_post, g_next)
    if g_next is None:
        return pl.pallas_call(
            _residual_last_kernel,
            grid=(s // NORM_ROWS,),
            in_specs=[row, row, gspec(post_layer)],
            out_specs=row,
            out_shape=jax.ShapeDtypeStruct((s, d), F32),
            compiler_params=_params(1),
            name="residual_last",
        )(y, x, g_post)
    return pl.pallas_call(
        _residual_kernel,
        grid=(s // NORM_ROWS,),
        in_specs=[row, row, gspec(post_layer), gspec(next_layer)],
        out_specs=[row, row],
        out_shape=[jax.ShapeDtypeStruct((s, d), F32), jax.ShapeDtypeStruct((s, d), BF16)],
        compiler_params=_params(1),
        name="residual_norm",
    )(y, x, g_post, g_next)


def _rope_table_kernel(pos_ref, freq_ref, sign_ref, cos_ref, sin_ref):
    ang = pos_ref[...].astype(F32) * freq_ref[...]
    cos_ref[...] = jnp.cos(ang)
    sin_ref[...] = jnp.sin(ang) * sign_ref[...]


def _rope_tables(positions):
    s = positions.shape[-1]
    half = HEAD_DIM // 2
    inv_freq = ROPE_THETA ** (-jnp.arange(half, dtype=F32) * 2.0 / HEAD_DIM)
    freq = jnp.concatenate([inv_freq, inv_freq]).reshape(1, HEAD_DIM)
    sign = jnp.concatenate([-jnp.ones((half,), F32), jnp.ones((half,), F32)]).reshape(1, HEAD_DIM)
    pos = jnp.stack([positions.reshape(s // d, d).T.reshape(s, 1) for _, d in DILATED_GROUPS])
    rows = 1024
    tab = pl.BlockSpec((None, rows, HEAD_DIM), lambda g, i: (g, i, 0))
    const = pl.BlockSpec((1, HEAD_DIM), lambda g, i: (0, 0))
    return pl.pallas_call(
        _rope_table_kernel,
        grid=(N_GROUPS, s // rows),
        in_specs=[pl.BlockSpec((None, rows, 1), lambda g, i: (g, i, 0)), const, const],
        out_specs=[tab, tab],
        out_shape=[jax.ShapeDtypeStruct((N_GROUPS, s, HEAD_DIM), F32)] * 2,
        compiler_params=_params(2),
        name="rope_tables",
    )(pos, freq, sign)


def _mem_kv_kernel(mem_ref, g_ref, w_ref, o_ref):
    m = mem_ref[...]
    mn = (m * _rms_scale(m) * g_ref[...]).astype(BF16)
    o_ref[...] = _bdot(mn, w_ref[...].astype(BF16)).astype(o_ref.dtype)


def _mem_kv(mem, g_mem, w_mem_kv):
    m, d = mem.shape
    n = w_mem_kv.shape[1]
    bn = 512
    return pl.pallas_call(
        _mem_kv_kernel,
        grid=(n // bn,),
        in_specs=[
            pl.BlockSpec((m, d), lambda j: (0, 0)),
            pl.BlockSpec((1, d), lambda j: (0, 0)),
            pl.BlockSpec((d, bn), lambda j: (0, j)),
        ],
        out_specs=pl.BlockSpec((m, bn), lambda j: (0, j)),
        out_shape=jax.ShapeDtypeStruct((m, n), BF16),
        compiler_params=_params(1),
        name="mem_kv",
    )(mem, g_mem.reshape(1, d), w_mem_kv)


def _causal_conv3(v, prev_tail, cw_ref, ext_ref):
    bm = v.shape[0]
    ext_ref[0:HALO, :] = prev_tail
    ext_ref[HALO:HALO + bm, :] = v
    v1 = ext_ref[HALO - 1:HALO - 1 + bm, :]
    v2 = ext_ref[HALO - 2:HALO - 2 + bm, :]
    conv = cw_ref[0:1, :] * v2 + cw_ref[1:2, :] * v1 + cw_ref[2:3, :] * v
    return conv, v[bm - HALO:bm, :]


def _row_chunks(total):
    return [slice(a, a + CHUNK_ROWS) for a in range(0, total, CHUNK_ROWS)]


def _resident_rows(bm, k, buffers=2):
    return pl.BlockSpec((bm, k), lambda i, j: (i, 0), pipeline_mode=pl.Buffered(buffers))


def _conv_in_kernel(x_ref, wb_ref, wc_ref, wu_ref, cw_ref, o_ref, ext0_ref, ext1_ref, tails_ref):
    i, j = pl.program_id(0), pl.program_id(1)
    tail = jnp.where(i == 0, 0.0, tails_ref[j])
    for chunk, rows in enumerate(_row_chunks(x_ref.shape[0])):
        ext_ref = (ext0_ref, ext1_ref)[chunk % 2]
        x = x_ref[rows, :]
        b = _bdot(x, wb_ref[...].astype(BF16))
        c = _bdot(x, wc_ref[...].astype(BF16))
        u = _bdot(x, wu_ref[...].astype(BF16))
        conv, tail = _causal_conv3(c * u, tail, cw_ref, ext_ref)
        o_ref[rows, :] = (b * conv).astype(o_ref.dtype)
    tails_ref[j] = tail


def _conv_in(xn, w_in, conv_w, layer):
    s, d = xn.shape
    bm, bn = ROW_BLOCK, V7X_MXU_COLS
    nj = CONV_WIDTH // bn
    wspec = lambda off: pl.BlockSpec((None, d, bn), lambda i, j: (layer, 0, j + off))
    ext = pltpu.VMEM((CHUNK_ROWS + HALO, bn), F32)
    return pl.pallas_call(
        _conv_in_kernel,
        grid=(s // bm, nj),
        in_specs=[
            _resident_rows(bm, d, buffers=1),
            wspec(0), wspec(nj), wspec(2 * nj),
            pl.BlockSpec((None, 3, bn), lambda i, j: (layer, 0, j)),
        ],
        out_specs=pl.BlockSpec((bm, bn), lambda i, j: (i, j)),
        out_shape=jax.ShapeDtypeStruct((s, CONV_WIDTH), BF16),
        scratch_shapes=[ext, ext, pltpu.VMEM((nj, HALO, bn), F32)],
        compiler_params=_params(2),
        name="conv_in",
    )(xn, w_in, w_in, w_in, conv_w)


def _mem_q_kernel(x_ref, wq_ref, k_ref, v_ref, o_ref):
    q = _bdot(x_ref[...], wq_ref[...].astype(BF16)) * (MEM_HEAD_DIM ** -0.5)
    s = lax.dot_general(q.astype(BF16), k_ref[...], (((1,), (1,)), ((), ())),
                        preferred_element_type=F32)
    m = jnp.max(s, axis=-1, keepdims=True)
    p = jnp.exp(s - m)
    den = jnp.sum(p, axis=-1, keepdims=True)
    o = _bdot((p / den).astype(BF16), v_ref[...])
    o_ref[...] = o.astype(o_ref.dtype)


def _mem_attend(xn, w_in, layer, q_col_block, kv):
    s, d = xn.shape
    bm, bn = ROW_BLOCK, MEM_HEAD_DIM
    return pl.pallas_call(
        _mem_q_kernel,
        grid=(s // bm, MEM_HEADS),
        in_specs=[
            _resident_rows(bm, d),
            pl.BlockSpec((None, d, bn), lambda i, h: (layer, 0, q_col_block + h)),
            pl.BlockSpec((MEM_TOKENS, bn), lambda i, h: (0, h)),
            pl.BlockSpec((MEM_TOKENS, bn), lambda i, h: (0, MEM_HEADS + h)),
        ],
        out_specs=pl.BlockSpec((bm, bn), lambda i, h: (i, h)),
        out_shape=jax.ShapeDtypeStruct((s, MEM_WIDTH), BF16),
        compiler_params=_params(2),
        name="mem_attend",
    )(xn, w_in, kv, kv)


def _out_proj_kernel(a_ref, b_ref, wa_ref, wb_ref, o_ref):
    for rows in _row_chunks(a_ref.shape[0]):
        acc = _bdot(a_ref[rows, :], wa_ref[...].astype(BF16))
        acc = acc + _bdot(b_ref[rows, :], wb_ref[...].astype(BF16))
        o_ref[rows, :] = acc.astype(o_ref.dtype)


def _out_proj(a, b, w, layer):
    s, ka = a.shape
    kb = b.shape[1]
    n = w.shape[2]
    bm, bn = ROW_BLOCK, OUT_PROJ_COLS
    assert ka % kb == 0
    return pl.pallas_call(
        _out_proj_kernel,
        grid=(s // bm, n // bn),
        in_specs=[
            _resident_rows(bm, ka), _resident_rows(bm, kb),
            pl.BlockSpec((None, ka, bn), lambda i, j: (layer, 0, j)),
            pl.BlockSpec((None, kb, bn), lambda i, j: (layer, ka // kb, j)),
        ],
        out_specs=pl.BlockSpec((bm, bn), lambda i, j: (i, j)),
        out_shape=jax.ShapeDtypeStruct((s, n), BF16),
        compiler_params=_params(2),
        name="out_proj",
    )(a, b, w, w)


def _ffn_in_kernel(x_ref, wg0_ref, wg1_ref, wu0_ref, wu1_ref, cw0_ref, cw1_ref, o_ref,
                   ext0_ref, ext1_ref, tails_ref, *, n_col_blocks):
    i, j = pl.program_id(0), pl.program_id(1)
    bn = wg0_ref.shape[1]
    operands = ((wg0_ref, wu0_ref, cw0_ref), (wg1_ref, wu1_ref, cw1_ref))

    def column_blocks(count):
        chunk = 0
        for half, (wg_ref, wu_ref, cw_ref) in enumerate(operands[:count]):
            col_block = 2 * j + half
            cols = slice(half * bn, (half + 1) * bn)
            tail = jnp.where(i == 0, 0.0, tails_ref[col_block])
            for rows in _row_chunks(x_ref.shape[0]):
                ext_ref = (ext0_ref, ext1_ref)[chunk % 2]
                chunk += 1
                x = x_ref[rows, :]
                g = _bdot(x, wg_ref[...].astype(BF16))
                u = _bdot(x, wu_ref[...].astype(BF16))
                gc, tail = _causal_conv3(g, tail, cw_ref, ext_ref)
                act = gc / (1.0 + jnp.exp(-gc))
                o_ref[rows, cols] = (act * u).astype(o_ref.dtype)
            tails_ref[col_block] = tail

    paired = 2 * j + 1 < n_col_blocks
    pl.when(paired)(lambda: column_blocks(2))
    pl.when(jnp.logical_not(paired))(lambda: column_blocks(1))


def _ffn_in(xn, w_gate, w_up, conv_w, layer):
    s, d = xn.shape
    bm, bn = ROW_BLOCK, V7X_MXU_COLS
    n_col_blocks = D_FF // bn
    col_block = lambda j, half: jnp.minimum(2 * j + half, n_col_blocks - 1)
    wspec = lambda half: pl.BlockSpec((None, d, bn), lambda i, j: (layer, 0, col_block(j, half)))
    cwspec = lambda half: pl.BlockSpec((None, 3, bn), lambda i, j: (layer, 0, col_block(j, half)))
    ext = pltpu.VMEM((CHUNK_ROWS + HALO, bn), F32)
    return pl.pallas_call(
        functools.partial(_ffn_in_kernel, n_col_blocks=n_col_blocks),
        grid=(s // bm, pl.cdiv(n_col_blocks, 2)),
        in_specs=[
            _resident_rows(bm, d, buffers=1),
            wspec(0), wspec(1), wspec(0), wspec(1), cwspec(0), cwspec(1),
        ],
        out_specs=pl.BlockSpec((bm, 2 * bn), lambda i, j: (i, j)),
        out_shape=jax.ShapeDtypeStruct((s, D_FF), BF16),
        scratch_shapes=[ext, ext, pltpu.VMEM((n_col_blocks, HALO, bn), F32)],
        compiler_params=_params(2),
        name="ffn_in",
    )(xn, w_gate, w_gate, w_up, w_up, conv_w, conv_w)


FFN_OUT_COL_CHUNK = 512
FFN_OUT_K_BLOCK = 1024
FFN_OUT_COLS = 2048


def _ffn_out_kernel(a_ref, w_ref, o_ref, acc_ref, *, kdim):
    k = pl.program_id(2)
    last = pl.num_programs(2) - 1
    bk = a_ref.shape[1]

    def step(first, final):
        valid = kdim - k * bk
        a_cols = lax.broadcasted_iota(jnp.int32, a_ref.shape, 1)
        a = jnp.where(a_cols < valid, a_ref[...], jnp.zeros_like(a_ref))
        w_rows = lax.broadcasted_iota(jnp.int32, (bk, FFN_OUT_COL_CHUNK), 0)
        for c in range(0, o_ref.shape[1], FFN_OUT_COL_CHUNK):
            cols = slice(c, c + FFN_OUT_COL_CHUNK)
            w = jnp.where(w_rows < valid, w_ref[:, cols], 0.0).astype(BF16)
            for rows in _row_chunks(a_ref.shape[0]):
                part = _bdot(a[rows, :], w)
                total = part if first else acc_ref[rows, cols] + part
                if final:
                    o_ref[rows, cols] = total.astype(o_ref.dtype)
                else:
                    acc_ref[rows, cols] = total

    pl.when(k == 0)(lambda: step(True, False))
    pl.when(jnp.logical_and(k > 0, k < last))(lambda: step(False, False))
    pl.when(k == last)(lambda: step(False, True))


def _ffn_out(a, w_down, layer):
    s, kdim = a.shape
    n = w_down.shape[2]
    bm, bk, bn = ROW_BLOCK, FFN_OUT_K_BLOCK, FFN_OUT_COLS
    return pl.pallas_call(
        functools.partial(_ffn_out_kernel, kdim=kdim),
        grid=(s // bm, n // bn, pl.cdiv(kdim, bk)),
        in_specs=[
            pl.BlockSpec((bm, bk), lambda i, j, k: (i, k)),
            pl.BlockSpec((None, bk, bn), lambda i, j, k: (layer, k, j)),
        ],
        out_specs=pl.BlockSpec((bm, bn), lambda i, j, k: (i, j)),
        out_shape=jax.ShapeDtypeStruct((s, n), BF16),
        scratch_shapes=[pltpu.VMEM((bm, bn), F32)],
        compiler_params=_params(3),
        name="ffn_out",
    )(a, w_down)


ATTN_IN_COLS = 512
HEADS_PER_ATTN_IN_BLOCK = ATTN_IN_COLS // HEAD_DIM
ATTN_IN_BLOCKS_PER_SECTION = GROUP_WIDTH // ATTN_IN_COLS
ATTN_IN_BLOCKS_PER_GROUP = 3 * ATTN_IN_BLOCKS_PER_SECTION


def _attn_in_kernel(x_ref, w_ref, cos_ref, sin_ref, o_ref):
    section = pl.program_id(1) // ATTN_IN_BLOCKS_PER_SECTION
    scale = jnp.where(section == 0, HEAD_DIM ** -0.5, 1.0).astype(F32)
    is_v = section == 2
    for rows in _row_chunks(x_ref.shape[0]):
        acc = _bdot(x_ref[rows, :], w_ref[...].astype(BF16))
        cos = cos_ref[rows, :] * scale
        sin = sin_ref[rows, :] * scale
        for h in range(HEADS_PER_ATTN_IN_BLOCK):
            cols = slice(h * HEAD_DIM, (h + 1) * HEAD_DIM)
            v = acc[:, cols]
            rot = pltpu.roll(v, HEAD_DIM // 2, axis=1)
            o_ref[rows, cols] = jnp.where(is_v, v, v * cos + rot * sin).astype(o_ref.dtype)


def _attn_in(xn, w_in, cos_tab, sin_tab, layer, group):
    s, d = xn.shape
    bm, bn = ROW_BLOCK, ATTN_IN_COLS
    nj = ATTN_IN_BLOCKS_PER_GROUP
    tab = pl.BlockSpec((None, bm, HEAD_DIM), lambda i, j: (group, i, 0))
    return pl.pallas_call(
        _attn_in_kernel,
        grid=(s // bm, nj),
        in_specs=[
            _resident_rows(bm, d),
            pl.BlockSpec((None, d, bn), lambda i, j: (layer, 0, group * nj + j)),
            tab, tab,
        ],
        out_specs=pl.BlockSpec((bm, bn), lambda i, j: (i, j)),
        out_shape=jax.ShapeDtypeStruct((s, 3 * GROUP_WIDTH), BF16),
        compiler_params=_params(2),
        name=f"attn_in_g{group}",
    )(xn, w_in, cos_tab, sin_tab)


def _window_attn_kernel(q_ref, k_ref, v_ref, o_ref, lse_ref, kprev_ref, vprev_ref):
    step = pl.program_id(1)

    @pl.when(step == 0)
    def _():
        kprev_ref[...] = jnp.zeros_like(kprev_ref)
        vprev_ref[...] = jnp.zeros_like(vprev_ref)

    qi = lax.broadcasted_iota(jnp.int32, (BLOCK, 2 * BLOCK), 0)
    col = lax.broadcasted_iota(jnp.int32, (BLOCK, 2 * BLOCK), 1)
    ki = col & (BLOCK - 1)
    is_own = col >= BLOCK
    valid_inner = jnp.logical_or(jnp.logical_and(is_own, ki <= qi),
                                 jnp.logical_and(jnp.logical_not(is_own), ki >= qi))
    valid_first = jnp.logical_and(valid_inner, jnp.logical_or(is_own, step > 0))
    lane = lax.broadcasted_iota(jnp.int32, (BLOCK, V7X_LANES), 1)
    ones = jnp.ones((2 * BLOCK, HEAD_DIM), BF16)
    nt = (((1,), (1,)), ((), ()))
    for sub, valid in enumerate((valid_first, valid_inner)):
        rows = slice(sub * BLOCK, (sub + 1) * BLOCK)
        lse_all = jnp.zeros((BLOCK, V7X_LANES), F32)
        for h in range(N_GROUP_HEADS):
            cols = slice(h * HEAD_DIM, (h + 1) * HEAD_DIM)
            if sub == 0:
                keys = jnp.concatenate([kprev_ref[:, cols], k_ref[rows, cols]], axis=0)
                vals = jnp.concatenate([vprev_ref[:, cols], v_ref[rows, cols]], axis=0)
            else:
                keys, vals = k_ref[:, cols], v_ref[:, cols]
            s = lax.dot_general(q_ref[rows, cols], keys, nt, preferred_element_type=F32)
            s = jnp.where(valid, s, NEG_INF)
            m = jnp.max(s, axis=-1, keepdims=True)
            p = jnp.exp(s - m).astype(BF16)
            ov = _bdot(p, jnp.concatenate([vals, ones], axis=1))
            den = ov[:, HEAD_DIM:]
            o_ref[rows, cols] = (ov[:, :HEAD_DIM] / den).astype(o_ref.dtype)
            lse_all = jnp.where(lane == h, m + jnp.log(den), lse_all)
        lse_ref[rows, :] = lse_all
    kprev_ref[...] = k_ref[BLOCK:2 * BLOCK, :]
    vprev_ref[...] = v_ref[BLOCK:2 * BLOCK, :]


def _window_attn(h_group, group):
    s = h_group.shape[0]
    _, d = DILATED_GROUPS[group]
    rows = 2 * BLOCK
    steps_per_class = s // d // rows
    spec = lambda width, sec: pl.BlockSpec(
        (rows, width), lambda r, step: (r * steps_per_class + step, sec))
    return pl.pallas_call(
        _window_attn_kernel,
        grid=(d, steps_per_class),
        in_specs=[spec(GROUP_WIDTH, 0), spec(GROUP_WIDTH, 1), spec(GROUP_WIDTH, 2)],
        out_specs=[spec(GROUP_WIDTH, 0), spec(V7X_LANES, 0)],
        out_shape=[
            jax.ShapeDtypeStruct((s, GROUP_WIDTH), BF16),
            jax.ShapeDtypeStruct((s, V7X_LANES), F32),
        ],
        scratch_shapes=[pltpu.VMEM((BLOCK, GROUP_WIDTH), BF16)] * 2,
        compiler_params=_params(2),
        name=f"window_attn_g{group}",
    )(h_group, h_group, h_group)


def _bf16_pieces(v):
    hi = v.astype(BF16)
    rest = v - hi.astype(F32)
    mid = rest.astype(BF16)
    lo = (rest - mid.astype(F32)).astype(BF16)
    return hi, mid, lo


def _merge_kernel(o0_ref, o1_ref, o2_ref, l0_ref, l1_ref, l2_ref, out_ref):
    outs = [o0_ref[...].astype(F32)]
    lses = [l0_ref[...]]
    for o_ref, l_ref in ((o1_ref, l1_ref), (o2_ref, l2_ref)):
        d = o_ref.shape[0]
        perm = _row_permutation(PERM_ROWS, d, False)
        o_cm = jnp.concatenate([o_ref[r] for r in range(d)], axis=0)
        l_cm = jnp.concatenate([l_ref[r] for r in range(d)], axis=0)
        outs.append(_bdot(perm, o_cm))
        hi, mid, lo = _bf16_pieces(l_cm)
        lses.append(_bdot(perm, hi) + _bdot(perm, mid) + _bdot(perm, lo))
    l0, l1, l2 = lses
    m = jnp.maximum(jnp.maximum(l0, l1), l2)
    e0, e1, e2 = jnp.exp(l0 - m), jnp.exp(l1 - m), jnp.exp(l2 - m)
    tot = e0 + e1 + e2
    a0, a1, a2 = e0 / tot, e1 / tot, e2 / tot
    for h in range(N_GROUP_HEADS):
        cols = slice(h * HEAD_DIM, (h + 1) * HEAD_DIM)
        merged = (a0[:, h:h + 1] * outs[0][:, cols] + a1[:, h:h + 1] * outs[1][:, cols]
                  + a2[:, h:h + 1] * outs[2][:, cols])
        out_ref[:, cols] = merged.astype(out_ref.dtype)


def _merge_groups(outs, lses):
    s = outs[0].shape[0]
    rows = PERM_ROWS

    def specs(width):
        out = [pl.BlockSpec((rows, width), lambda i: (i, 0))]
        for _, d in DILATED_GROUPS[1:]:
            out.append(pl.BlockSpec((d, rows // d, width), lambda i: (0, i, 0)))
        return out

    class_major = lambda a, d: a.reshape(d, s // d, a.shape[1])
    outs = [outs[0]] + [class_major(o, d) for o, (_, d) in zip(outs[1:], DILATED_GROUPS[1:])]
    lses = [lses[0]] + [class_major(l, d) for l, (_, d) in zip(lses[1:], DILATED_GROUPS[1:])]
    return pl.pallas_call(
        _merge_kernel,
        grid=(s // rows,),
        in_specs=specs(GROUP_WIDTH) + specs(V7X_LANES),
        out_specs=pl.BlockSpec((rows, GROUP_WIDTH), lambda i: (i, 0)),
        out_shape=jax.ShapeDtypeStruct((s, GROUP_WIDTH), BF16),
        compiler_params=_params(1),
        name="merge_groups",
    )(*outs, *lses)


def kernel(x, mem, positions, g_mem, w_mem_kv, g_mix_pre, g_mix_post, g_ffn_pre, g_ffn_post,
           w_conv_in, conv_mix_w, w_conv_out, w_attn_in, w_attn_out,
           w_ffn_gate, w_ffn_up, conv_ffn_w, w_ffn_down):
    b, s, d = x.shape
    assert b == 1 and s == SEQ and d == D_MODEL
    xs = x.reshape(s, d)
    gains = lambda g: g.reshape(DEPTH, 1, d)
    g_mix_pre, g_mix_post = gains(g_mix_pre), gains(g_mix_post)
    g_ffn_pre, g_ffn_post = gains(g_ffn_pre), gains(g_ffn_post)

    kv = _mem_kv(mem.reshape(MEM_TOKENS, d), g_mem, w_mem_kv)
    cos_tab, sin_tab = _rope_tables(positions)

    dilations = tuple(dil for _, dil in DILATED_GROUPS[1:])
    xn = _rmsnorm_bf16(xs, g_mix_pre, 0)
    xn_class_major = ()
    for i in range(DEPTH):
        j = i // 2
        if i % 2 == 0:
            y_mix = _conv_in(xn, w_conv_in, conv_mix_w, j)
            y_mem = _mem_attend(xn, w_conv_in, j, 3 * CONV_WIDTH // MEM_HEAD_DIM, kv)
            y = _out_proj(y_mix, y_mem, w_conv_out, j)
        else:
            lhs = [xn] + [xp.reshape(s, d) for xp in xn_class_major]
            parts = [
                _window_attn(_attn_in(lhs[g], w_attn_in, cos_tab, sin_tab, j, g), g)
                for g in range(N_GROUPS)]
            y_mix = _merge_groups([p[0] for p in parts], [p[1] for p in parts])
            y_mem = _mem_attend(xn, w_attn_in, j, N_SELF // MEM_HEAD_DIM, kv)
            y = _out_proj(y_mix, y_mem, w_attn_out, j)
        xs, xn = _residual_norm(y, xs, g_mix_post, i, g_ffn_pre, i)
        a = _ffn_in(xn, w_ffn_gate, w_ffn_up, conv_ffn_w, i)
        y = _ffn_out(a, w_ffn_down, i)
        if i + 1 == DEPTH:
            xs = _residual_norm(y, xs, g_ffn_post, i)
        elif (i + 1) % 2 == 1:
            xs, xn, *xn_class_major = _residual_norm(
                y, xs, g_ffn_post, i, g_mix_pre, i + 1, dilations=dilations)
        else:
            xs, xn = _residual_norm(y, xs, g_ffn_post, i, g_mix_pre, i + 1)
    return xs.reshape(b, s, d)
```

```python
import functools

import jax
import jax.numpy as jnp
from jax import lax
from jax.experimental import pallas as pl
from jax.experimental.pallas import tpu as pltpu

D_MODEL = 4096
SEQ = 8192
DEPTH = 2
HEAD_DIM = 128
N_GROUP_HEADS = 16
GROUP_WIDTH = N_GROUP_HEADS * HEAD_DIM
DILATED_GROUPS = ((128, 1), (512, 4), (2048, 16))
N_GROUPS = len(DILATED_GROUPS)
N_SELF = N_GROUPS * 3 * GROUP_WIDTH
BLOCK = 128
CONV_WIDTH = 3 * D_MODEL // 4
MEM_TOKENS = 256
MEM_HEADS = 4
MEM_HEAD_DIM = 256
MEM_WIDTH = MEM_HEADS * MEM_HEAD_DIM
D_FF = 11008
ROPE_THETA = 10000.0
EPS = 1e-6
NEG_INF = -1e30

V7X_SUBLANES = 8
V7X_LANES = 128
V7X_MXU_COLS = 256
V7X_VMEM_BYTES = 64 * 1024 * 1024
V7X_VMEM_LIMIT_BYTES = V7X_VMEM_BYTES - 2 * 1024 * 1024

BF16 = jnp.bfloat16
F32 = jnp.float32

ROW_BLOCK = 2048
CHUNK_ROWS = 512
OUT_PROJ_COLS = 512
NORM_ROWS = 512
PERM_ROWS = 256
HALO = V7X_SUBLANES


def _params(n_axes):
    return pltpu.CompilerParams(
        dimension_semantics=("arbitrary",) * n_axes,
        vmem_limit_bytes=V7X_VMEM_LIMIT_BYTES,
    )


def _rms_scale(v):
    return lax.rsqrt(jnp.mean(v * v, axis=-1, keepdims=True) + EPS)


def _bdot(a, b):
    return jnp.dot(a, b, preferred_element_type=F32)


def _norm_kernel(x_ref, g_ref, o_ref):
    x = x_ref[...]
    o_ref[...] = (x * _rms_scale(x) * g_ref[...]).astype(o_ref.dtype)


def _rmsnorm_bf16(x, g_all, layer):
    s, d = x.shape
    return pl.pallas_call(
        _norm_kernel,
        grid=(s // NORM_ROWS,),
        in_specs=[
            pl.BlockSpec((NORM_ROWS, d), lambda i: (i, 0)),
            pl.BlockSpec((None, 1, d), lambda i: (layer, 0, 0)),
        ],
        out_specs=pl.BlockSpec((NORM_ROWS, d), lambda i: (i, 0)),
        out_shape=jax.ShapeDtypeStruct((s, d), BF16),
        compiler_params=_params(1),
        name="rmsnorm_bf16",
    )(x, g_all)


def _residual_kernel(y_ref, x_ref, gpost_ref, gnext_ref, xo_ref, xn_ref):
    y = y_ref[...].astype(F32)
    xnew = x_ref[...] + y * _rms_scale(y) * gpost_ref[...]
    xo_ref[...] = xnew
    xn_ref[...] = (xnew * _rms_scale(xnew) * gnext_ref[...]).astype(xn_ref.dtype)


def _row_permutation(rows, d, to_class_major):
    n = rows // d
    out_row = lax.broadcasted_iota(jnp.int32, (rows, rows), 0)
    in_row = lax.broadcasted_iota(jnp.int32, (rows, rows), 1)
    if to_class_major:
        src = (out_row & (n - 1)) * d + (out_row >> (n.bit_length() - 1))
    else:
        src = (out_row & (d - 1)) * n + (out_row >> (d.bit_length() - 1))
    return (in_row == src).astype(BF16)


def _residual_dilated_kernel(y_ref, x_ref, gpost_ref, gnext_ref, xo_ref, xn_ref, *xp_refs):
    y = y_ref[...].astype(F32)
    xnew = x_ref[...] + y * _rms_scale(y) * gpost_ref[...]
    xo_ref[...] = xnew
    xn = (xnew * _rms_scale(xnew) * gnext_ref[...]).astype(BF16)
    xn_ref[...] = xn
    for xp_ref in xp_refs:
        d, n = xp_ref.shape[0], xp_ref.shape[1]
        xp = _bdot(_row_permutation(PERM_ROWS, d, True), xn).astype(BF16)
        for r in range(d):
            xp_ref[r] = xp[r * n:(r + 1) * n, :]


def _residual_last_kernel(y_ref, x_ref, gpost_ref, xo_ref):
    y = y_ref[...].astype(F32)
    xo_ref[...] = x_ref[...] + y * _rms_scale(y) * gpost_ref[...]


def _residual_norm(y, x, g_post, post_layer, g_next=None, next_layer=None, dilations=()):
    s, d = x.shape
    row = pl.BlockSpec((NORM_ROWS, d), lambda i: (i, 0))
    gspec = lambda layer: pl.BlockSpec((None, 1, d), lambda i: (layer, 0, 0))
    if dilations:
        row = pl.BlockSpec((PERM_ROWS, d), lambda i: (i, 0))
        return pl.pallas_call(
            _residual_dilated_kernel,
            grid=(s // PERM_ROWS,),
            in_specs=[row, row, gspec(post_layer), gspec(next_layer)],
            out_specs=[row, row] + [
                pl.BlockSpec((dil, PERM_ROWS // dil, d), lambda i: (0, i, 0)) for dil in dilations],
            out_shape=[jax.ShapeDtypeStruct((s, d), F32), jax.ShapeDtypeStruct((s, d), BF16)] + [
                jax.ShapeDtypeStruct((dil, s // dil, d), BF16) for dil in dilations],
            compiler_params=_params(1),
            name="residual_norm_dilated",
        )(y, x, g_post, g_next)
    if g_next is None:
        return pl.pallas_call(
            _residual_last_kernel,
            grid=(s // NORM_ROWS,),
            in_specs=[row, row, gspec(post_layer)],
            out_specs=row,
            out_shape=jax.ShapeDtypeStruct((s, d), F32),
            compiler_params=_params(1),
            name="residual_last",
        )(y, x, g_post)
    return pl.pallas_call(
        _residual_kernel,
        grid=(s // NORM_ROWS,),
        in_specs=[row, row, gspec(post_layer), gspec(next_layer)],
        out_specs=[row, row],
        out_shape=[jax.ShapeDtypeStruct((s, d), F32), jax.ShapeDtypeStruct((s, d), BF16)],
        compiler_params=_params(1),
        name="residual_norm",
    )(y, x, g_post, g_next)


def _rope_table_kernel(pos_ref, freq_ref, sign_ref, cos_ref, sin_ref):
    ang = pos_ref[...].astype(F32) * freq_ref[...]
    cos_ref[...] = jnp.cos(ang)
    sin_ref[...] = jnp.sin(ang) * sign_ref[...]


def _rope_tables(positions):
    s = positions.shape[-1]
    half = HEAD_DIM // 2
    inv_freq = ROPE_THETA ** (-jnp.arange(half, dtype=F32) * 2.0 / HEAD_DIM)
    freq = jnp.concatenate([inv_freq, inv_freq]).reshape(1, HEAD_DIM)
    sign = jnp.concatenate([-jnp.ones((half,), F32), jnp.ones((half,), F32)]).reshape(1, HEAD_DIM)
    pos = jnp.stack([positions.reshape(s // d, d).T.reshape(s, 1) for _, d in DILATED_GROUPS])
    rows = 1024
    tab = pl.BlockSpec((None, rows, HEAD_DIM), lambda g, i: (g, i, 0))
    const = pl.BlockSpec((1, HEAD_DIM), lambda g, i: (0, 0))
    return pl.pallas_call(
        _rope_table_kernel,
        grid=(N_GROUPS, s // rows),
        in_specs=[pl.BlockSpec((None, rows, 1), lambda g, i: (g, i, 0)), const, const],
        out_specs=[tab, tab],
        out_shape=[jax.ShapeDtypeStruct((N_GROUPS, s, HEAD_DIM), F32)] * 2,
        compiler_params=_params(2),
        name="rope_tables",
    )(pos, freq, sign)


def _mem_kv_kernel(mem_ref, g_ref, w_ref, o_ref):
    m = mem_ref[...]
    mn = (m * _rms_scale(m) * g_ref[...]).astype(BF16)
    o_ref[...] = _bdot(mn, w_ref[...].astype(BF16)).astype(o_ref.dtype)


def _mem_kv(mem, g_mem, w_mem_kv):
    m, d = mem.shape
    n = w_mem_kv.shape[1]
    bn = 512
    return pl.pallas_call(
        _mem_kv_kernel,
        grid=(n // bn,),
        in_specs=[
            pl.BlockSpec((m, d), lambda j: (0, 0)),
            pl.BlockSpec((1, d), lambda j: (0, 0)),
            pl.BlockSpec((d, bn), lambda j: (0, j)),
        ],
        out_specs=pl.BlockSpec((m, bn), lambda j: (0, j)),
        out_shape=jax.ShapeDtypeStruct((m, n), BF16),
        compiler_params=_params(1),
        name="mem_kv",
    )(mem, g_mem.reshape(1, d), w_mem_kv)


def _causal_conv3(v, prev_tail, cw_ref, ext_ref):
    bm = v.shape[0]
    ext_ref[0:HALO, :] = prev_tail
    ext_ref[HALO:HALO + bm, :] = v
    v1 = ext_ref[HALO - 1:HALO - 1 + bm, :]
    v2 = ext_ref[HALO - 2:HALO - 2 + bm, :]
    conv = cw_ref[0:1, :] * v2 + cw_ref[1:2, :] * v1 + cw_ref[2:3, :] * v
    return conv, v[bm - HALO:bm, :]


def _row_chunks(total):
    return [slice(a, a + CHUNK_ROWS) for a in range(0, total, CHUNK_ROWS)]


def _resident_rows(bm, k, buffers=2):
    return pl.BlockSpec((bm, k), lambda i, j: (i, 0), pipeline_mode=pl.Buffered(buffers))


def _conv_in_kernel(x_ref, wb_ref, wc_ref, wu_ref, cw_ref, o_ref, ext0_ref, ext1_ref, tails_ref):
    i, j = pl.program_id(0), pl.program_id(1)
    tail = jnp.where(i == 0, 0.0, tails_ref[j])
    for chunk, rows in enumerate(_row_chunks(x_ref.shape[0])):
        ext_ref = (ext0_ref, ext1_ref)[chunk % 2]
        x = x_ref[rows, :]
        b = _bdot(x, wb_ref[...].astype(BF16))
        c = _bdot(x, wc_ref[...].astype(BF16))
        u = _bdot(x, wu_ref[...].astype(BF16))
        conv, tail = _causal_conv3(c * u, tail, cw_ref, ext_ref)
        o_ref[rows, :] = (b * conv).astype(o_ref.dtype)
    tails_ref[j] = tail


def _conv_in(xn, w_in, conv_w, layer):
    s, d = xn.shape
    bm, bn = ROW_BLOCK, V7X_MXU_COLS
    nj = CONV_WIDTH // bn
    wspec = lambda off: pl.BlockSpec((None, d, bn), lambda i, j: (layer, 0, j + off))
    ext = pltpu.VMEM((CHUNK_ROWS + HALO, bn), F32)
    return pl.pallas_call(
        _conv_in_kernel,
        grid=(s // bm, nj),
        in_specs=[
            _resident_rows(bm, d, buffers=1),
            wspec(0), wspec(nj), wspec(2 * nj),
            pl.BlockSpec((None, 3, bn), lambda i, j: (layer, 0, j)),
        ],
        out_specs=pl.BlockSpec((bm, bn), lambda i, j: (i, j)),
        out_shape=jax.ShapeDtypeStruct((s, CONV_WIDTH), BF16),
        scratch_shapes=[ext, ext, pltpu.VMEM((nj, HALO, bn), F32)],
        compiler_params=_params(2),
        name="conv_in",
    )(xn, w_in, w_in, w_in, conv_w)


def _mem_q_kernel(x_ref, wq_ref, k_ref, v_ref, o_ref):
    q = _bdot(x_ref[...], wq_ref[...].astype(BF16)) * (MEM_HEAD_DIM ** -0.5)
    s = lax.dot_general(q.astype(BF16), k_ref[...], (((1,), (1,)), ((), ())),
                        preferred_element_type=F32)
    m = jnp.max(s, axis=-1, keepdims=True)
    p = jnp.exp(s - m)
    den = jnp.sum(p, axis=-1, keepdims=True)
    o = _bdot((p / den).astype(BF16), v_ref[...])
    o_ref[...] = o.astype(o_ref.dtype)


def _mem_attend(xn, w_in, layer, q_col_block, kv):
    s, d = xn.shape
    bm, bn = ROW_BLOCK, MEM_HEAD_DIM
    return pl.pallas_call(
        _mem_q_kernel,
        grid=(s // bm, MEM_HEADS),
        in_specs=[
            _resident_rows(bm, d),
            pl.BlockSpec((None, d, bn), lambda i, h: (layer, 0, q_col_block + h)),
            pl.BlockSpec((MEM_TOKENS, bn), lambda i, h: (0, h)),
            pl.BlockSpec((MEM_TOKENS, bn), lambda i, h: (0, MEM_HEADS + h)),
        ],
        out_specs=pl.BlockSpec((bm, bn), lambda i, h: (i, h)),
        out_shape=jax.ShapeDtypeStruct((s, MEM_WIDTH), BF16),
        compiler_params=_params(2),
        name="mem_attend",
    )(xn, w_in, kv, kv)


def _out_proj_kernel(a_ref, b_ref, wa_ref, wb_ref, o_ref):
    for rows in _row_chunks(a_ref.shape[0]):
        acc = _bdot(a_ref[rows, :], wa_ref[...].astype(BF16))
        acc = acc + _bdot(b_ref[rows, :], wb_ref[...].astype(BF16))
        o_ref[rows, :] = acc.astype(o_ref.dtype)


def _out_proj(a, b, w, layer):
    s, ka = a.shape
    kb = b.shape[1]
    n = w.shape[2]
    bm, bn = ROW_BLOCK, OUT_PROJ_COLS
    assert ka % kb == 0
    return pl.pallas_call(
        _out_proj_kernel,
        grid=(s // bm, n // bn),
        in_specs=[
            _resident_rows(bm, ka), _resident_rows(bm, kb),
            pl.BlockSpec((None, ka, bn), lambda i, j: (layer, 0, j)),
            pl.BlockSpec((None, kb, bn), lambda i, j: (layer, ka // kb, j)),
        ],
        out_specs=pl.BlockSpec((bm, bn), lambda i, j: (i, j)),
        out_shape=jax.ShapeDtypeStruct((s, n), BF16),
        compiler_params=_params(2),
        name="out_proj",
    )(a, b, w, w)


def _ffn_in_kernel(x_ref, wg0_ref, wg1_ref, wu0_ref, wu1_ref, cw0_ref, cw1_ref, o_ref,
                   ext0_ref, ext1_ref, tails_ref, *, n_col_blocks):
    i, j = pl.program_id(0), pl.program_id(1)
    bn = wg0_ref.shape[1]
    operands = ((wg0_ref, wu0_ref, cw0_ref), (wg1_ref, wu1_ref, cw1_ref))

    def column_blocks(count):
        chunk = 0
        for half, (wg_ref, wu_ref, cw_ref) in enumerate(operands[:count]):
            col_block = 2 * j + half
            cols = slice(half * bn, (half + 1) * bn)
            tail = jnp.where(i == 0, 0.0, tails_ref[col_block])
            for rows in _row_chunks(x_ref.shape[0]):
                ext_ref = (ext0_ref, ext1_ref)[chunk % 2]
                chunk += 1
                x = x_ref[rows, :]
                g = _bdot(x, wg_ref[...].astype(BF16))
                u = _bdot(x, wu_ref[...].astype(BF16))
                gc, tail = _causal_conv3(g, tail, cw_ref, ext_ref)
                act = gc / (1.0 + jnp.exp(-gc))
                o_ref[rows, cols] = (act * u).astype(o_ref.dtype)
            tails_ref[col_block] = tail

    paired = 2 * j + 1 < n_col_blocks
    pl.when(paired)(lambda: column_blocks(2))
    pl.when(jnp.logical_not(paired))(lambda: column_blocks(1))


def _ffn_in(xn, w_gate, w_up, conv_w, layer):
    s, d = xn.shape
    bm, bn = ROW_BLOCK, V7X_MXU_COLS
    n_col_blocks = D_FF // bn
    col_block = lambda j, half: jnp.minimum(2 * j + half, n_col_blocks - 1)
    wspec = lambda half: pl.BlockSpec((None, d, bn), lambda i, j: (layer, 0, col_block(j, half)))
    cwspec = lambda half: pl.BlockSpec((None, 3, bn), lambda i, j: (layer, 0, col_block(j, half)))
    ext = pltpu.VMEM((CHUNK_ROWS + HALO, bn), F32)
    return pl.pallas_call(
        functools.partial(_ffn_in_kernel, n_col_blocks=n_col_blocks),
        grid=(s // bm, pl.cdiv(n_col_blocks, 2)),
        in_specs=[
            _resident_rows(bm, d, buffers=1),
            wspec(0), wspec(1), wspec(0), wspec(1), cwspec(0), cwspec(1),
        ],
        out_specs=pl.BlockSpec((bm, 2 * bn), lambda i, j: (i, j)),
        out_shape=jax.ShapeDtypeStruct((s, D_FF), BF16),
        scratch_shapes=[ext, ext, pltpu.VMEM((n_col_blocks, HALO, bn), F32)],
        compiler_params=_params(2),
        name="ffn_in",
    )(xn, w_gate, w_gate, w_up, w_up, conv_w, conv_w)


FFN_OUT_COL_CHUNK = 512
FFN_OUT_K_BLOCK = 1024
FFN_OUT_COLS = 2048


def _ffn_out_kernel(a_ref, w_ref, o_ref, acc_ref, *, kdim):
    k = pl.program_id(2)
    last = pl.num_programs(2) - 1
    bk = a_ref.shape[1]

    def step(first, final):
        valid = kdim - k * bk
        a_cols = lax.broadcasted_iota(jnp.int32, a_ref.shape, 1)
        a = jnp.where(a_cols < valid, a_ref[...], jnp.zeros_like(a_ref))
        w_rows = lax.broadcasted_iota(jnp.int32, (bk, FFN_OUT_COL_CHUNK), 0)
        for c in range(0, o_ref.shape[1], FFN_OUT_COL_CHUNK):
            cols = slice(c, c + FFN_OUT_COL_CHUNK)
            w = jnp.where(w_rows < valid, w_ref[:, cols], 0.0).astype(BF16)
            for rows in _row_chunks(a_ref.shape[0]):
                part = _bdot(a[rows, :], w)
                total = part if first else acc_ref[rows, cols] + part
                if final:
                    o_ref[rows, cols] = total.astype(o_ref.dtype)
                else:
                    acc_ref[rows, cols] = total

    pl.when(k == 0)(lambda: step(True, False))
    pl.when(jnp.logical_and(k > 0, k < last))(lambda: step(False, False))
    pl.when(k == last)(lambda: step(False, True))


def _ffn_out(a, w_down, layer):
    s, kdim = a.shape
    n = w_down.shape[2]
    bm, bk, bn = ROW_BLOCK, FFN_OUT_K_BLOCK, FFN_OUT_COLS
    return pl.pallas_call(
        functools.partial(_ffn_out_kernel, kdim=kdim),
        grid=(s // bm, n // bn, pl.cdiv(kdim, bk)),
        in_specs=[
            pl.BlockSpec((bm, bk), lambda i, j, k: (i, k)),
            pl.BlockSpec((None, bk, bn), lambda i, j, k: (layer, k, j)),
        ],
        out_specs=pl.BlockSpec((bm, bn), lambda i, j, k: (i, j)),
        out_shape=jax.ShapeDtypeStruct((s, n), BF16),
        scratch_shapes=[pltpu.VMEM((bm, bn), F32)],
        compiler_params=_params(3),
        name="ffn_out",
    )(a, w_down)


ATTN_IN_COLS = 512
HEADS_PER_ATTN_IN_BLOCK = ATTN_IN_COLS // HEAD_DIM
ATTN_IN_BLOCKS_PER_SECTION = GROUP_WIDTH // ATTN_IN_COLS
ATTN_IN_BLOCKS_PER_GROUP = 3 * ATTN_IN_BLOCKS_PER_SECTION


def _attn_in_kernel(x_ref, w_ref, cos_ref, sin_ref, o_ref):
    section = pl.program_id(1) // ATTN_IN_BLOCKS_PER_SECTION
    scale = jnp.where(section == 0, HEAD_DIM ** -0.5, 1.0).astype(F32)
    is_v = section == 2
    for rows in _row_chunks(x_ref.shape[0]):
        acc = _bdot(x_ref[rows, :], w_ref[...].astype(BF16))
        cos = cos_ref[rows, :] * scale
        sin = sin_ref[rows, :] * scale
        for h in range(HEADS_PER_ATTN_IN_BLOCK):
            cols = slice(h * HEAD_DIM, (h + 1) * HEAD_DIM)
            v = acc[:, cols]
            rot = pltpu.roll(v, HEAD_DIM // 2, axis=1)
            o_ref[rows, cols] = jnp.where(is_v, v, v * cos + rot * sin).astype(o_ref.dtype)


def _attn_in(xn, w_in, cos_tab, sin_tab, layer, group):
    s, d = xn.shape
    bm, bn = ROW_BLOCK, ATTN_IN_COLS
    nj = ATTN_IN_BLOCKS_PER_GROUP
    tab = pl.BlockSpec((None, bm, HEAD_DIM), lambda i, j: (group, i, 0))
    return pl.pallas_call(
        _attn_in_kernel,
        grid=(s // bm, nj),
        in_specs=[
            _resident_rows(bm, d),
            pl.BlockSpec((None, d, bn), lambda i, j: (layer, 0, group * nj + j)),
            tab, tab,
        ],
        out_specs=pl.BlockSpec((bm, bn), lambda i, j: (i, j)),
        out_shape=jax.ShapeDtypeStruct((s, 3 * GROUP_WIDTH), BF16),
        compiler_params=_params(2),
        name=f"attn_in_g{group}",
    )(xn, w_in, cos_tab, sin_tab)


WINDOW_BLOCKS_PER_STEP = 4


def _window_attn_kernel(q_ref, k_ref, v_ref, o_ref, lse_ref, kprev_ref, vprev_ref):
    step = pl.program_id(1)

    @pl.when(step == 0)
    def _():
        kprev_ref[...] = jnp.zeros_like(kprev_ref)
        vprev_ref[...] = jnp.zeros_like(vprev_ref)

    qi = lax.broadcasted_iota(jnp.int32, (BLOCK, 2 * BLOCK), 0)
    col = lax.broadcasted_iota(jnp.int32, (BLOCK, 2 * BLOCK), 1)
    ki = col & (BLOCK - 1)
    is_own = col >= BLOCK
    valid_inner = jnp.logical_or(jnp.logical_and(is_own, ki <= qi),
                                 jnp.logical_and(jnp.logical_not(is_own), ki >= qi))
    valid_first = jnp.logical_and(valid_inner, jnp.logical_or(is_own, step > 0))
    lane = lax.broadcasted_iota(jnp.int32, (BLOCK, V7X_LANES), 1)
    ones = jnp.ones((2 * BLOCK, HEAD_DIM), BF16)
    nt = (((1,), (1,)), ((), ()))
    n_blocks = q_ref.shape[0] // BLOCK
    for sub in range(n_blocks):
        valid = valid_first if sub == 0 else valid_inner
        rows = slice(sub * BLOCK, (sub + 1) * BLOCK)
        lse_all = jnp.zeros((BLOCK, V7X_LANES), F32)
        for h in range(N_GROUP_HEADS):
            cols = slice(h * HEAD_DIM, (h + 1) * HEAD_DIM)
            if sub == 0:
                keys = jnp.concatenate([kprev_ref[:, cols], k_ref[rows, cols]], axis=0)
                vals = jnp.concatenate([vprev_ref[:, cols], v_ref[rows, cols]], axis=0)
            else:
                both = slice((sub - 1) * BLOCK, (sub + 1) * BLOCK)
                keys, vals = k_ref[both, cols], v_ref[both, cols]
            s = lax.dot_general(q_ref[rows, cols], keys, nt, preferred_element_type=F32)
            s = jnp.where(valid, s, NEG_INF)
            m = jnp.max(s, axis=-1, keepdims=True)
            p = jnp.exp(s - m).astype(BF16)
            ov = _bdot(p, jnp.concatenate([vals, ones], axis=1))
            den = ov[:, HEAD_DIM:]
            o_ref[rows, cols] = (ov[:, :HEAD_DIM] / den).astype(o_ref.dtype)
            lse_all = jnp.where(lane == h, m + jnp.log(den), lse_all)
        lse_ref[rows, :] = lse_all
    last_block = slice((n_blocks - 1) * BLOCK, n_blocks * BLOCK)
    kprev_ref[...] = k_ref[last_block, :]
    vprev_ref[...] = v_ref[last_block, :]


def _window_attn(h_group, group):
    s = h_group.shape[0]
    _, d = DILATED_GROUPS[group]
    rows = WINDOW_BLOCKS_PER_STEP * BLOCK
    assert (s // d) % rows == 0
    steps_per_class = s // d // rows
    spec = lambda width, sec: pl.BlockSpec(
        (rows, width), lambda r, step: (r * steps_per_class + step, sec))
    return pl.pallas_call(
        _window_attn_kernel,
        grid=(d, steps_per_class),
        in_specs=[spec(GROUP_WIDTH, 0), spec(GROUP_WIDTH, 1), spec(GROUP_WIDTH, 2)],
        out_specs=[spec(GROUP_WIDTH, 0), spec(V7X_LANES, 0)],
        out_shape=[
            jax.ShapeDtypeStruct((s, GROUP_WIDTH), BF16),
            jax.ShapeDtypeStruct((s, V7X_LANES), F32),
        ],
        scratch_shapes=[pltpu.VMEM((BLOCK, GROUP_WIDTH), BF16)] * 2,
        compiler_params=_params(2),
        name=f"window_attn_g{group}",
    )(h_group, h_group, h_group)


def _bf16_pieces(v):
    hi = v.astype(BF16)
    rest = v - hi.astype(F32)
    mid = rest.astype(BF16)
    lo = (rest - mid.astype(F32)).astype(BF16)
    return hi, mid, lo


def _merge_kernel(o0_ref, o1_ref, o2_ref, l0_ref, l1_ref, l2_ref, out_ref):
    outs = [o0_ref[...].astype(F32)]
    lses = [l0_ref[...]]
    for o_ref, l_ref in ((o1_ref, l1_ref), (o2_ref, l2_ref)):
        d = o_ref.shape[0]
        perm = _row_permutation(PERM_ROWS, d, False)
        o_cm = jnp.concatenate([o_ref[r] for r in range(d)], axis=0)
        l_cm = jnp.concatenate([l_ref[r] for r in range(d)], axis=0)
        outs.append(_bdot(perm, o_cm))
        hi, mid, lo = _bf16_pieces(l_cm)
        lses.append(_bdot(perm, hi) + _bdot(perm, mid) + _bdot(perm, lo))
    l0, l1, l2 = lses
    m = jnp.maximum(jnp.maximum(l0, l1), l2)
    e0, e1, e2 = jnp.exp(l0 - m), jnp.exp(l1 - m), jnp.exp(l2 - m)
    tot = e0 + e1 + e2
    a0, a1, a2 = e0 / tot, e1 / tot, e2 / tot
    for h in range(N_GROUP_HEADS):
        cols = slice(h * HEAD_DIM, (h + 1) * HEAD_DIM)
        merged = (a0[:, h:h + 1] * outs[0][:, cols] + a1[:, h:h + 1] * outs[1][:, cols]
                  + a2[:, h:h + 1] * outs[2][:, cols])
        out_ref[:, cols] = merged.astype(out_ref.dtype)


def _merge_groups(outs, lses):
    s = outs[0].shape[0]
    rows = PERM_ROWS

    def specs(width):
        out = [pl.BlockSpec((rows, width), lambda i: (i, 0))]
        for _, d in DILATED_GROUPS[1:]:
            out.append(pl.BlockSpec((d, rows // d, width), lambda i: (0, i, 0)))
        return out

    class_major = lambda a, d: a.reshape(d, s // d, a.shape[1])
    outs = [outs[0]] + [class_major(o, d) for o, (_, d) in zip(outs[1:], DILATED_GROUPS[1:])]
    lses = [lses[0]] + [class_major(l, d) for l, (_, d) in zip(lses[1:], DILATED_GROUPS[1:])]
    return pl.pallas_call(
        _merge_kernel,
        grid=(s // rows,),
        in_specs=specs(GROUP_WIDTH) + specs(V7X_LANES),
        out_specs=pl.BlockSpec((rows, GROUP_WIDTH), lambda i: (i, 0)),
        out_shape=jax.ShapeDtypeStruct((s, GROUP_WIDTH), BF16),
        compiler_params=_params(1),
        name="merge_groups",
    )(*outs, *lses)


def kernel(x, mem, positions, g_mem, w_mem_kv, g_mix_pre, g_mix_post, g_ffn_pre, g_ffn_post,
           w_conv_in, conv_mix_w, w_conv_out, w_attn_in, w_attn_out,
           w_ffn_gate, w_ffn_up, conv_ffn_w, w_ffn_down):
    b, s, d = x.shape
    assert b == 1 and s == SEQ and d == D_MODEL
    xs = x.reshape(s, d)
    gains = lambda g: g.reshape(DEPTH, 1, d)
    g_mix_pre, g_mix_post = gains(g_mix_pre), gains(g_mix_post)
    g_ffn_pre, g_ffn_post = gains(g_ffn_pre), gains(g_ffn_post)

    kv = _mem_kv(mem.reshape(MEM_TOKENS, d), g_mem, w_mem_kv)
    cos_tab, sin_tab = _rope_tables(positions)

    dilations = tuple(dil for _, dil in DILATED_GROUPS[1:])
    xn = _rmsnorm_bf16(xs, g_mix_pre, 0)
    xn_class_major = ()
    for i in range(DEPTH):
        j = i // 2
        if i % 2 == 0:
            y_mix = _conv_in(xn, w_conv_in, conv_mix_w, j)
            y_mem = _mem_attend(xn, w_conv_in, j, 3 * CONV_WIDTH // MEM_HEAD_DIM, kv)
            y = _out_proj(y_mix, y_mem, w_conv_out, j)
        else:
            lhs = [xn] + [xp.reshape(s, d) for xp in xn_class_major]
            parts = [
                _window_attn(_attn_in(lhs[g], w_attn_in, cos_tab, sin_tab, j, g), g)
                for g in range(N_GROUPS)]
            y_mix = _merge_groups([p[0] for p in parts], [p[1] for p in parts])
            y_mem = _mem_attend(xn, w_attn_in, j, N_SELF // MEM_HEAD_DIM, kv)
            y = _out_proj(y_mix, y_mem, w_attn_out, j)
        xs, xn = _residual_norm(y, xs, g_mix_post, i, g_ffn_pre, i)
        a = _ffn_in(xn, w_ffn_gate, w_ffn_up, conv_ffn_w, i)
        y = _ffn_out(a, w_ffn_down, i)
        if i + 1 == DEPTH:
            xs = _residual_norm(y, xs, g_ffn_post, i)
        elif (i + 1) % 2 == 1:
            xs, xn, *xn_class_major = _residual_norm(
                y, xs, g_ffn_post, i, g_mix_pre, i + 1, dilations=dilations)
        else:
            xs, xn = _residual_norm(y, xs, g_ffn_post, i, g_mix_pre, i + 1)
    return xs.reshape(b, s, d)
```

```python
import functools

import jax
import jax.numpy as jnp
from jax import lax
from jax.experimental import pallas as pl
from jax.experimental.pallas import tpu as pltpu

D_MODEL = 4096
SEQ = 8192
DEPTH = 2
HEAD_DIM = 128
N_GROUP_HEADS = 16
GROUP_WIDTH = N_GROUP_HEADS * HEAD_DIM
DILATED_GROUPS = ((128, 1), (512, 4), (2048, 16))
N_GROUPS = len(DILATED_GROUPS)
N_SELF = N_GROUPS * 3 * GROUP_WIDTH
BLOCK = 128
CONV_WIDTH = 3 * D_MODEL // 4
MEM_TOKENS = 256
MEM_HEADS = 4
MEM_HEAD_DIM = 256
MEM_WIDTH = MEM_HEADS * MEM_HEAD_DIM
D_FF = 11008
ROPE_THETA = 10000.0
EPS = 1e-6
NEG_INF = -1e30

V7X_SUBLANES = 8
V7X_LANES = 128
V7X_MXU_COLS = 256
V7X_VMEM_BYTES = 64 * 1024 * 1024
V7X_VMEM_LIMIT_BYTES = V7X_VMEM_BYTES - 2 * 1024 * 1024

BF16 = jnp.bfloat16
F32 = jnp.float32

ROW_BLOCK = 2048
CHUNK_ROWS = 512
OUT_PROJ_COLS = 512
NORM_ROWS = 512
PERM_ROWS = 256
HALO = V7X_SUBLANES


def _params(n_axes):
    return pltpu.CompilerParams(
        dimension_semantics=("arbitrary",) * n_axes,
        vmem_limit_bytes=V7X_VMEM_LIMIT_BYTES,
    )


def _rms_scale(v):
    return lax.rsqrt(jnp.mean(v * v, axis=-1, keepdims=True) + EPS)


def _bdot(a, b):
    return jnp.dot(a, b, preferred_element_type=F32)


def _norm_kernel(x_ref, g_ref, o_ref):
    x = x_ref[...]
    o_ref[...] = (x * _rms_scale(x) * g_ref[...]).astype(o_ref.dtype)


def _rmsnorm_bf16(x, g_all, layer):
    s, d = x.shape
    return pl.pallas_call(
        _norm_kernel,
        grid=(s // NORM_ROWS,),
        in_specs=[
            pl.BlockSpec((NORM_ROWS, d), lambda i: (i, 0)),
            pl.BlockSpec((None, 1, d), lambda i: (layer, 0, 0)),
        ],
        out_specs=pl.BlockSpec((NORM_ROWS, d), lambda i: (i, 0)),
        out_shape=jax.ShapeDtypeStruct((s, d), BF16),
        compiler_params=_params(1),
        name="rmsnorm_bf16",
    )(x, g_all)


def _residual_kernel(y_ref, x_ref, gpost_ref, gnext_ref, xo_ref, xn_ref):
    y = y_ref[...].astype(F32)
    xnew = x_ref[...] + y * _rms_scale(y) * gpost_ref[...]
    xo_ref[...] = xnew
    xn_ref[...] = (xnew * _rms_scale(xnew) * gnext_ref[...]).astype(xn_ref.dtype)


def _row_permutation(rows, d, to_class_major):
    n = rows // d
    out_row = lax.broadcasted_iota(jnp.int32, (rows, rows), 0)
    in_row = lax.broadcasted_iota(jnp.int32, (rows, rows), 1)
    if to_class_major:
        src = (out_row & (n - 1)) * d + (out_row >> (n.bit_length() - 1))
    else:
        src = (out_row & (d - 1)) * n + (out_row >> (d.bit_length() - 1))
    return (in_row == src).astype(BF16)


def _residual_dilated_kernel(y_ref, x_ref, gpost_ref, gnext_ref, xo_ref, xn_ref, *xp_refs):
    y = y_ref[...].astype(F32)
    xnew = x_ref[...] + y * _rms_scale(y) * gpost_ref[...]
    xo_ref[...] = xnew
    xn = (xnew * _rms_scale(xnew) * gnext_ref[...]).astype(BF16)
    xn_ref[...] = xn
    for xp_ref in xp_refs:
        d, n = xp_ref.shape[0], xp_ref.shape[1]
        xp = _bdot(_row_permutation(PERM_ROWS, d, True), xn).astype(BF16)
        for r in range(d):
            xp_ref[r] = xp[r * n:(r + 1) * n, :]


def _residual_last_kernel(y_ref, x_ref, gpost_ref, xo_ref):
    y = y_ref[...].astype(F32)
    xo_ref[...] = x_ref[...] + y * _rms_scale(y) * gpost_ref[...]


def _residual_norm(y, x, g_post, post_layer, g_next=None, next_layer=None, dilations=()):
    s, d = x.shape
    row = pl.BlockSpec((NORM_ROWS, d), lambda i: (i, 0))
    gspec = lambda layer: pl.BlockSpec((None, 1, d), lambda i: (layer, 0, 0))
    if dilations:
        row = pl.BlockSpec((PERM_ROWS, d), lambda i: (i, 0))
        return pl.pallas_call(
            _residual_dilated_kernel,
            grid=(s // PERM_ROWS,),
            in_specs=[row, row, gspec(post_layer), gspec(next_layer)],
            out_specs=[row, row] + [
                pl.BlockSpec((dil, PERM_ROWS // dil, d), lambda i: (0, i, 0)) for dil in dilations],
            out_shape=[jax.ShapeDtypeStruct((s, d), F32), jax.ShapeDtypeStruct((s, d), BF16)] + [
                jax.ShapeDtypeStruct((dil, s // dil, d), BF16) for dil in dilations],
            compiler_params=_params(1),
            name="residual_norm_dilated",
        )(y, x, g_post, g_next)
    if g_next is None:
        return pl.pallas_call(
            _residual_last_kernel,
            grid=(s // NORM_ROWS,),
            in_specs=[row, row, gspec(post_layer)],
            out_specs=row,
            out_shape=jax.ShapeDtypeStruct((s, d), F32),
            compiler_params=_params(1),
            name="residual_last",
        )(y, x, g_post)
    return pl.pallas_call(
        _residual_kernel,
        grid=(s // NORM_ROWS,),
        in_specs=[row, row, gspec(post_layer), gspec(next_layer)],
        out_specs=[row, row],
        out_shape=[jax.ShapeDtypeStruct((s, d), F32), jax.ShapeDtypeStruct((s, d), BF16)],
        compiler_params=_params(1),
        name="residual_norm",
    )(y, x, g_post, g_next)


def _rope_table_kernel(pos_ref, freq_ref, sign_ref, cos_ref, sin_ref):
    ang = pos_ref[...].astype(F32) * freq_ref[...]
    cos_ref[...] = jnp.cos(ang)
    sin_ref[...] = jnp.sin(ang) * sign_ref[...]


def _rope_tables(positions):
    s = positions.shape[-1]
    half = HEAD_DIM // 2
    inv_freq = ROPE_THETA ** (-jnp.arange(half, dtype=F32) * 2.0 / HEAD_DIM)
    freq = jnp.concatenate([inv_freq, inv_freq]).reshape(1, HEAD_DIM)
    sign = jnp.concatenate([-jnp.ones((half,), F32), jnp.ones((half,), F32)]).reshape(1, HEAD_DIM)
    pos = jnp.stack([positions.reshape(s // d, d).T.reshape(s, 1) for _, d in DILATED_GROUPS])
    rows = 1024
    tab = pl.BlockSpec((None, rows, HEAD_DIM), lambda g, i: (g, i, 0))
    const = pl.BlockSpec((1, HEAD_DIM), lambda g, i: (0, 0))
    return pl.pallas_call(
        _rope_table_kernel,
        grid=(N_GROUPS, s // rows),
        in_specs=[pl.BlockSpec((None, rows, 1), lambda g, i: (g, i, 0)), const, const],
        out_specs=[tab, tab],
        out_shape=[jax.ShapeDtypeStruct((N_GROUPS, s, HEAD_DIM), F32)] * 2,
        compiler_params=_params(2),
        name="rope_tables",
    )(pos, freq, sign)


def _mem_kv_kernel(mem_ref, g_ref, w_ref, o_ref):
    m = mem_ref[...]
    mn = (m * _rms_scale(m) * g_ref[...]).astype(BF16)
    o_ref[...] = _bdot(mn, w_ref[...].astype(BF16)).astype(o_ref.dtype)


def _mem_kv(mem, g_mem, w_mem_kv):
    m, d = mem.shape
    n = w_mem_kv.shape[1]
    bn = 512
    return pl.pallas_call(
        _mem_kv_kernel,
        grid=(n // bn,),
        in_specs=[
            pl.BlockSpec((m, d), lambda j: (0, 0)),
            pl.BlockSpec((1, d), lambda j: (0, 0)),
            pl.BlockSpec((d, bn), lambda j: (0, j)),
        ],
        out_specs=pl.BlockSpec((m, bn), lambda j: (0, j)),
        out_shape=jax.ShapeDtypeStruct((m, n), BF16),
        compiler_params=_params(1),
        name="mem_kv",
    )(mem, g_mem.reshape(1, d), w_mem_kv)


def _causal_conv3(v, prev_tail, cw_ref, ext_ref):
    bm = v.shape[0]
    ext_ref[0:HALO, :] = prev_tail
    ext_ref[HALO:HALO + bm, :] = v
    v1 = ext_ref[HALO - 1:HALO - 1 + bm, :]
    v2 = ext_ref[HALO - 2:HALO - 2 + bm, :]
    conv = cw_ref[0:1, :] * v2 + cw_ref[1:2, :] * v1 + cw_ref[2:3, :] * v
    return conv, v[bm - HALO:bm, :]


def _row_chunks(total):
    return [slice(a, a + CHUNK_ROWS) for a in range(0, total, CHUNK_ROWS)]


def _resident_rows(bm, k, buffers=2):
    return pl.BlockSpec((bm, k), lambda i, j: (i, 0), pipeline_mode=pl.Buffered(buffers))


def _conv_in_kernel(x_ref, wb_ref, wc_ref, wu_ref, cw_ref, o_ref, ext0_ref, ext1_ref, tails_ref):
    i, j = pl.program_id(0), pl.program_id(1)
    tail = jnp.where(i == 0, 0.0, tails_ref[j])
    for chunk, rows in enumerate(_row_chunks(x_ref.shape[0])):
        ext_ref = (ext0_ref, ext1_ref)[chunk % 2]
        x = x_ref[rows, :]
        b = _bdot(x, wb_ref[...].astype(BF16))
        c = _bdot(x, wc_ref[...].astype(BF16))
        u = _bdot(x, wu_ref[...].astype(BF16))
        conv, tail = _causal_conv3(c * u, tail, cw_ref, ext_ref)
        o_ref[rows, :] = (b * conv).astype(o_ref.dtype)
    tails_ref[j] = tail


def _conv_in(xn, w_in, conv_w, layer):
    s, d = xn.shape
    bm, bn = ROW_BLOCK, V7X_MXU_COLS
    nj = CONV_WIDTH // bn
    wspec = lambda off: pl.BlockSpec((None, d, bn), lambda i, j: (layer, 0, j + off))
    ext = pltpu.VMEM((CHUNK_ROWS + HALO, bn), F32)
    return pl.pallas_call(
        _conv_in_kernel,
        grid=(s // bm, nj),
        in_specs=[
            _resident_rows(bm, d, buffers=1),
            wspec(0), wspec(nj), wspec(2 * nj),
            pl.BlockSpec((None, 3, bn), lambda i, j: (layer, 0, j)),
        ],
        out_specs=pl.BlockSpec((bm, bn), lambda i, j: (i, j)),
        out_shape=jax.ShapeDtypeStruct((s, CONV_WIDTH), BF16),
        scratch_shapes=[ext, ext, pltpu.VMEM((nj, HALO, bn), F32)],
        compiler_params=_params(2),
        name="conv_in",
    )(xn, w_in, w_in, w_in, conv_w)


def _mem_q_kernel(x_ref, wq_ref, k_ref, v_ref, o_ref):
    q = _bdot(x_ref[...], wq_ref[...].astype(BF16)) * (MEM_HEAD_DIM ** -0.5)
    s = lax.dot_general(q.astype(BF16), k_ref[...], (((1,), (1,)), ((), ())),
                        preferred_element_type=F32)
    m = jnp.max(s, axis=-1, keepdims=True)
    p = jnp.exp(s - m)
    den = jnp.sum(p, axis=-1, keepdims=True)
    o = _bdot((p / den).astype(BF16), v_ref[...])
    o_ref[...] = o.astype(o_ref.dtype)


def _mem_attend(xn, w_in, layer, q_col_block, kv):
    s, d = xn.shape
    bm, bn = ROW_BLOCK, MEM_HEAD_DIM
    return pl.pallas_call(
        _mem_q_kernel,
        grid=(s // bm, MEM_HEADS),
        in_specs=[
            _resident_rows(bm, d),
            pl.BlockSpec((None, d, bn), lambda i, h: (layer, 0, q_col_block + h)),
            pl.BlockSpec((MEM_TOKENS, bn), lambda i, h: (0, h)),
            pl.BlockSpec((MEM_TOKENS, bn), lambda i, h: (0, MEM_HEADS + h)),
        ],
        out_specs=pl.BlockSpec((bm, bn), lambda i, h: (i, h)),
        out_shape=jax.ShapeDtypeStruct((s, MEM_WIDTH), BF16),
        compiler_params=_params(2),
        name="mem_attend",
    )(xn, w_in, kv, kv)


def _out_proj_kernel(a_ref, b_ref, wa_ref, wb_ref, o_ref):
    for rows in _row_chunks(a_ref.shape[0]):
        acc = _bdot(a_ref[rows, :], wa_ref[...].astype(BF16))
        acc = acc + _bdot(b_ref[rows, :], wb_ref[...].astype(BF16))
        o_ref[rows, :] = acc.astype(o_ref.dtype)


def _out_proj(a, b, w, layer):
    s, ka = a.shape
    kb = b.shape[1]
    n = w.shape[2]
    bm, bn = ROW_BLOCK, OUT_PROJ_COLS
    assert ka % kb == 0
    return pl.pallas_call(
        _out_proj_kernel,
        grid=(s // bm, n // bn),
        in_specs=[
            _resident_rows(bm, ka), _resident_rows(bm, kb),
            pl.BlockSpec((None, ka, bn), lambda i, j: (layer, 0, j)),
            pl.BlockSpec((None, kb, bn), lambda i, j: (layer, ka // kb, j)),
        ],
        out_specs=pl.BlockSpec((bm, bn), lambda i, j: (i, j)),
        out_shape=jax.ShapeDtypeStruct((s, n), BF16),
        compiler_params=_params(2),
        name="out_proj",
    )(a, b, w, w)


def _ffn_in_kernel(x_ref, wg0_ref, wg1_ref, wu0_ref, wu1_ref, cw0_ref, cw1_ref, o_ref,
                   ext0_ref, ext1_ref, tails_ref, *, n_col_blocks):
    i, j = pl.program_id(0), pl.program_id(1)
    bn = wg0_ref.shape[1]
    operands = ((wg0_ref, wu0_ref, cw0_ref), (wg1_ref, wu1_ref, cw1_ref))

    def column_blocks(count):
        chunk = 0
        for half, (wg_ref, wu_ref, cw_ref) in enumerate(operands[:count]):
            col_block = 2 * j + half
            cols = slice(half * bn, (half + 1) * bn)
            tail = jnp.where(i == 0, 0.0, tails_ref[col_block])
            for rows in _row_chunks(x_ref.shape[0]):
                ext_ref = (ext0_ref, ext1_ref)[chunk % 2]
                chunk += 1
                x = x_ref[rows, :]
                g = _bdot(x, wg_ref[...].astype(BF16))
                u = _bdot(x, wu_ref[...].astype(BF16))
                gc, tail = _causal_conv3(g, tail, cw_ref, ext_ref)
                act = gc / (1.0 + jnp.exp(-gc))
                o_ref[rows, cols] = (act * u).astype(o_ref.dtype)
            tails_ref[col_block] = tail

    paired = 2 * j + 1 < n_col_blocks
    pl.when(paired)(lambda: column_blocks(2))
    pl.when(jnp.logical_not(paired))(lambda: column_blocks(1))


def _ffn_in(xn, w_gate, w_up, conv_w, layer):
    s, d = xn.shape
    bm, bn = ROW_BLOCK, V7X_MXU_COLS
    n_col_blocks = D_FF // bn
    col_block = lambda j, half: jnp.minimum(2 * j + half, n_col_blocks - 1)
    wspec = lambda half: pl.BlockSpec((None, d, bn), lambda i, j: (layer, 0, col_block(j, half)))
    cwspec = lambda half: pl.BlockSpec((None, 3, bn), lambda i, j: (layer, 0, col_block(j, half)))
    ext = pltpu.VMEM((CHUNK_ROWS + HALO, bn), F32)
    return pl.pallas_call(
        functools.partial(_ffn_in_kernel, n_col_blocks=n_col_blocks),
        grid=(s // bm, pl.cdiv(n_col_blocks, 2)),
        in_specs=[
            _resident_rows(bm, d, buffers=1),
            wspec(0), wspec(1), wspec(0), wspec(1), cwspec(0), cwspec(1),
        ],
        out_specs=pl.BlockSpec((bm, 2 * bn), lambda i, j: (i, j)),
        out_shape=jax.ShapeDtypeStruct((s, D_FF), BF16),
        scratch_shapes=[ext, ext, pltpu.VMEM((n_col_blocks, HALO, bn), F32)],
        compiler_params=_params(2),
        name="ffn_in",
    )(xn, w_gate, w_gate, w_up, w_up, conv_w, conv_w)


FFN_OUT_COL_CHUNK = 512
FFN_OUT_K_BLOCK = 1024
FFN_OUT_COLS = 2048


def _ffn_out_kernel(a_ref, w_ref, o_ref, acc_ref, *, kdim):
    k = pl.program_id(2)
    last = pl.num_programs(2) - 1
    bk = a_ref.shape[1]

    def step(first, final):
        valid = kdim - k * bk
        a_cols = lax.broadcasted_iota(jnp.int32, a_ref.shape, 1)
        a = jnp.where(a_cols < valid, a_ref[...], jnp.zeros_like(a_ref))
        w_rows = lax.broadcasted_iota(jnp.int32, (bk, FFN_OUT_COL_CHUNK), 0)
        for c in range(0, o_ref.shape[1], FFN_OUT_COL_CHUNK):
            cols = slice(c, c + FFN_OUT_COL_CHUNK)
            w = jnp.where(w_rows < valid, w_ref[:, cols], 0.0).astype(BF16)
            for rows in _row_chunks(a_ref.shape[0]):
                part = _bdot(a[rows, :], w)
                total = part if first else acc_ref[rows, cols] + part
                if final:
                    o_ref[rows, cols] = total.astype(o_ref.dtype)
                else:
                    acc_ref[rows, cols] = total

    pl.when(k == 0)(lambda: step(True, False))
    pl.when(jnp.logical_and(k > 0, k < last))(lambda: step(False, False))
    pl.when(k == last)(lambda: step(False, True))


def _ffn_out(a, w_down, layer):
    s, kdim = a.shape
    n = w_down.shape[2]
    bm, bk, bn = ROW_BLOCK, FFN_OUT_K_BLOCK, FFN_OUT_COLS
    return pl.pallas_call(
        functools.partial(_ffn_out_kernel, kdim=kdim),
        grid=(s // bm, n // bn, pl.cdiv(kdim, bk)),
        in_specs=[
            pl.BlockSpec((bm, bk), lambda i, j, k: (i, k)),
            pl.BlockSpec((None, bk, bn), lambda i, j, k: (layer, k, j)),
        ],
        out_specs=pl.BlockSpec((bm, bn), lambda i, j, k: (i, j)),
        out_shape=jax.ShapeDtypeStruct((s, n), BF16),
        scratch_shapes=[pltpu.VMEM((bm, bn), F32)],
        compiler_params=_params(3),
        name="ffn_out",
    )(a, w_down)


FFN_FUSED_ROWS = 1024


def _ffn_fused_kernel(x_ref, wg_ref, wu_ref, cw_ref, wd_ref, o_ref, acc_ref, ext0_ref, ext1_ref, tails_ref):
    i, j = pl.program_id(0), pl.program_id(1)
    last = pl.num_programs(1) - 1

    def step(first, final):
        tail = jnp.where(i == 0, 0.0, tails_ref[j])
        for chunk, rows in enumerate(_row_chunks(x_ref.shape[0])):
            ext_ref = (ext0_ref, ext1_ref)[chunk % 2]
            x = x_ref[rows, :]
            g = _bdot(x, wg_ref[...].astype(BF16))
            u = _bdot(x, wu_ref[...].astype(BF16))
            gc, tail = _causal_conv3(g, tail, cw_ref, ext_ref)
            a = (gc / (1.0 + jnp.exp(-gc)) * u).astype(BF16)
            for c in range(0, o_ref.shape[1], FFN_OUT_COL_CHUNK):
                cols = slice(c, c + FFN_OUT_COL_CHUNK)
                part = _bdot(a, wd_ref[:, cols].astype(BF16))
                total = part if first else acc_ref[rows, cols] + part
                if final:
                    o_ref[rows, cols] = total.astype(o_ref.dtype)
                else:
                    acc_ref[rows, cols] = total
        tails_ref[j] = tail

    pl.when(j == 0)(lambda: step(True, False))
    pl.when(jnp.logical_and(j > 0, j < last))(lambda: step(False, False))
    pl.when(j == last)(lambda: step(False, True))


def _ffn_fused(xn, w_gate, w_up, conv_w, w_down, layer):
    s, d = xn.shape
    bm, bn = FFN_FUSED_ROWS, V7X_MXU_COLS
    nj = D_FF // bn
    n = w_down.shape[2]
    wspec = pl.BlockSpec((None, d, bn), lambda i, j: (layer, 0, j))
    ext = pltpu.VMEM((CHUNK_ROWS + HALO, bn), F32)
    return pl.pallas_call(
        _ffn_fused_kernel,
        grid=(s // bm, nj),
        in_specs=[
            _resident_rows(bm, d, buffers=1),
            wspec, wspec,
            pl.BlockSpec((None, 3, bn), lambda i, j: (layer, 0, j)),
            pl.BlockSpec((None, bn, n), lambda i, j: (layer, j, 0)),
        ],
        out_specs=pl.BlockSpec((bm, n), lambda i, j: (i, 0), pipeline_mode=pl.Buffered(1)),
        out_shape=jax.ShapeDtypeStruct((s, n), BF16),
        scratch_shapes=[pltpu.VMEM((bm, n), F32), ext, ext, pltpu.VMEM((nj, HALO, bn), F32)],
        compiler_params=_params(2),
        name="ffn_fused",
    )(xn, w_gate, w_up, conv_w, w_down)


ATTN_IN_COLS = 512
HEADS_PER_ATTN_IN_BLOCK = ATTN_IN_COLS // HEAD_DIM
ATTN_IN_BLOCKS_PER_SECTION = GROUP_WIDTH // ATTN_IN_COLS
ATTN_IN_BLOCKS_PER_GROUP = 3 * ATTN_IN_BLOCKS_PER_SECTION


def _attn_in_kernel(x_ref, w_ref, cos_ref, sin_ref, o_ref):
    section = pl.program_id(1) // ATTN_IN_BLOCKS_PER_SECTION
    scale = jnp.where(section == 0, HEAD_DIM ** -0.5, 1.0).astype(F32)
    is_v = section == 2
    for rows in _row_chunks(x_ref.shape[0]):
        acc = _bdot(x_ref[rows, :], w_ref[...].astype(BF16))
        cos = cos_ref[rows, :] * scale
        sin = sin_ref[rows, :] * scale
        for h in range(HEADS_PER_ATTN_IN_BLOCK):
            cols = slice(h * HEAD_DIM, (h + 1) * HEAD_DIM)
            v = acc[:, cols]
            rot = pltpu.roll(v, HEAD_DIM // 2, axis=1)
            o_ref[rows, cols] = jnp.where(is_v, v, v * cos + rot * sin).astype(o_ref.dtype)


def _attn_in(xn, w_in, cos_tab, sin_tab, layer, group):
    s, d = xn.shape
    bm, bn = ROW_BLOCK, ATTN_IN_COLS
    nj = ATTN_IN_BLOCKS_PER_GROUP
    tab = pl.BlockSpec((None, bm, HEAD_DIM), lambda i, j: (group, i, 0))
    return pl.pallas_call(
        _attn_in_kernel,
        grid=(s // bm, nj),
        in_specs=[
            _resident_rows(bm, d),
            pl.BlockSpec((None, d, bn), lambda i, j: (layer, 0, group * nj + j)),
            tab, tab,
        ],
        out_specs=pl.BlockSpec((bm, bn), lambda i, j: (i, j)),
        out_shape=jax.ShapeDtypeStruct((s, 3 * GROUP_WIDTH), BF16),
        compiler_params=_params(2),
        name=f"attn_in_g{group}",
    )(xn, w_in, cos_tab, sin_tab)


WINDOW_BLOCKS_PER_STEP = 4


def _window_attn_kernel(q_ref, k_ref, v_ref, o_ref, lse_ref, kprev_ref, vprev_ref):
    step = pl.program_id(1)

    @pl.when(step == 0)
    def _():
        kprev_ref[...] = jnp.zeros_like(kprev_ref)
        vprev_ref[...] = jnp.zeros_like(vprev_ref)

    qi = lax.broadcasted_iota(jnp.int32, (BLOCK, 2 * BLOCK), 0)
    col = lax.broadcasted_iota(jnp.int32, (BLOCK, 2 * BLOCK), 1)
    ki = col & (BLOCK - 1)
    is_own = col >= BLOCK
    valid_inner = jnp.logical_or(jnp.logical_and(is_own, ki <= qi),
                                 jnp.logical_and(jnp.logical_not(is_own), ki >= qi))
    valid_first = jnp.logical_and(valid_inner, jnp.logical_or(is_own, step > 0))
    lane = lax.broadcasted_iota(jnp.int32, (BLOCK, V7X_LANES), 1)
    ones = jnp.ones((2 * BLOCK, HEAD_DIM), BF16)
    nt = (((1,), (1,)), ((), ()))
    n_blocks = q_ref.shape[0] // BLOCK
    for sub in range(n_blocks):
        valid = valid_first if sub == 0 else valid_inner
        rows = slice(sub * BLOCK, (sub + 1) * BLOCK)
        lse_all = jnp.zeros((BLOCK, V7X_LANES), F32)
        for h in range(N_GROUP_HEADS):
            cols = slice(h * HEAD_DIM, (h + 1) * HEAD_DIM)
            if sub == 0:
                keys = jnp.concatenate([kprev_ref[:, cols], k_ref[rows, cols]], axis=0)
                vals = jnp.concatenate([vprev_ref[:, cols], v_ref[rows, cols]], axis=0)
            else:
                both = slice((sub - 1) * BLOCK, (sub + 1) * BLOCK)
                keys, vals = k_ref[both, cols], v_ref[both, cols]
            s = lax.dot_general(q_ref[rows, cols], keys, nt, preferred_element_type=F32)
            s = jnp.where(valid, s, NEG_INF)
            m = jnp.max(s, axis=-1, keepdims=True)
            p = jnp.exp(s - m).astype(BF16)
            ov = _bdot(p, jnp.concatenate([vals, ones], axis=1))
            den = ov[:, HEAD_DIM:]
            o_ref[rows, cols] = (ov[:, :HEAD_DIM] / den).astype(o_ref.dtype)
            lse_all = jnp.where(lane == h, m + jnp.log(den), lse_all)
        lse_ref[rows, :] = lse_all
    last_block = slice((n_blocks - 1) * BLOCK, n_blocks * BLOCK)
    kprev_ref[...] = k_ref[last_block, :]
    vprev_ref[...] = v_ref[last_block, :]


def _window_attn(h_group, group):
    s = h_group.shape[0]
    _, d = DILATED_GROUPS[group]
    rows = WINDOW_BLOCKS_PER_STEP * BLOCK
    assert (s // d) % rows == 0
    steps_per_class = s // d // rows
    spec = lambda width, sec: pl.BlockSpec(
        (rows, width), lambda r, step: (r * steps_per_class + step, sec))
    return pl.pallas_call(
        _window_attn_kernel,
        grid=(d, steps_per_class),
        in_specs=[spec(GROUP_WIDTH, 0), spec(GROUP_WIDTH, 1), spec(GROUP_WIDTH, 2)],
        out_specs=[spec(GROUP_WIDTH, 0), spec(V7X_LANES, 0)],
        out_shape=[
            jax.ShapeDtypeStruct((s, GROUP_WIDTH), BF16),
            jax.ShapeDtypeStruct((s, V7X_LANES), F32),
        ],
        scratch_shapes=[pltpu.VMEM((BLOCK, GROUP_WIDTH), BF16)] * 2,
        compiler_params=_params(2),
        name=f"window_attn_g{group}",
    )(h_group, h_group, h_group)


def _bf16_pieces(v):
    hi = v.astype(BF16)
    rest = v - hi.astype(F32)
    mid = rest.astype(BF16)
    lo = (rest - mid.astype(F32)).astype(BF16)
    return hi, mid, lo


def _merge_kernel(o0_ref, o1_ref, o2_ref, l0_ref, l1_ref, l2_ref, out_ref):
    outs = [o0_ref[...].astype(F32)]
    lses = [l0_ref[...]]
    for o_ref, l_ref in ((o1_ref, l1_ref), (o2_ref, l2_ref)):
        d = o_ref.shape[0]
        perm = _row_permutation(PERM_ROWS, d, False)
        o_cm = jnp.concatenate([o_ref[r] for r in range(d)], axis=0)
        l_cm = jnp.concatenate([l_ref[r] for r in range(d)], axis=0)
        outs.append(_bdot(perm, o_cm))
        hi, mid, lo = _bf16_pieces(l_cm)
        lses.append(_bdot(perm, hi) + _bdot(perm, mid) + _bdot(perm, lo))
    l0, l1, l2 = lses
    m = jnp.maximum(jnp.maximum(l0, l1), l2)
    e0, e1, e2 = jnp.exp(l0 - m), jnp.exp(l1 - m), jnp.exp(l2 - m)
    tot = e0 + e1 + e2
    a0, a1, a2 = e0 / tot, e1 / tot, e2 / tot
    for h in range(N_GROUP_HEADS):
        cols = slice(h * HEAD_DIM, (h + 1) * HEAD_DIM)
        merged = (a0[:, h:h + 1] * outs[0][:, cols] + a1[:, h:h + 1] * outs[1][:, cols]
                  + a2[:, h:h + 1] * outs[2][:, cols])
        out_ref[:, cols] = merged.astype(out_ref.dtype)


def _merge_groups(outs, lses):
    s = outs[0].shape[0]
    rows = PERM_ROWS

    def specs(width):
        out = [pl.BlockSpec((rows, width), lambda i: (i, 0))]
        for _, d in DILATED_GROUPS[1:]:
            out.append(pl.BlockSpec((d, rows // d, width), lambda i: (0, i, 0)))
        return out

    class_major = lambda a, d: a.reshape(d, s // d, a.shape[1])
    outs = [outs[0]] + [class_major(o, d) for o, (_, d) in zip(outs[1:], DILATED_GROUPS[1:])]
    lses = [lses[0]] + [class_major(l, d) for l, (_, d) in zip(lses[1:], DILATED_GROUPS[1:])]
    return pl.pallas_call(
        _merge_kernel,
        grid=(s // rows,),
        in_specs=specs(GROUP_WIDTH) + specs(V7X_LANES),
        out_specs=pl.BlockSpec((rows, GROUP_WIDTH), lambda i: (i, 0)),
        out_shape=jax.ShapeDtypeStruct((s, GROUP_WIDTH), BF16),
        compiler_params=_params(1),
        name="merge_groups",
    )(*outs, *lses)


def kernel(x, mem, positions, g_mem, w_mem_kv, g_mix_pre, g_mix_post, g_ffn_pre, g_ffn_post,
           w_conv_in, conv_mix_w, w_conv_out, w_attn_in, w_attn_out,
           w_ffn_gate, w_ffn_up, conv_ffn_w, w_ffn_down):
    b, s, d = x.shape
    assert b == 1 and s == SEQ and d == D_MODEL
    xs = x.reshape(s, d)
    gains = lambda g: g.reshape(DEPTH, 1, d)
    g_mix_pre, g_mix_post = gains(g_mix_pre), gains(g_mix_post)
    g_ffn_pre, g_ffn_post = gains(g_ffn_pre), gains(g_ffn_post)

    kv = _mem_kv(mem.reshape(MEM_TOKENS, d), g_mem, w_mem_kv)
    cos_tab, sin_tab = _rope_tables(positions)

    dilations = tuple(dil for _, dil in DILATED_GROUPS[1:])
    xn = _rmsnorm_bf16(xs, g_mix_pre, 0)
    xn_class_major = ()
    for i in range(DEPTH):
        j = i // 2
        if i % 2 == 0:
            y_mix = _conv_in(xn, w_conv_in, conv_mix_w, j)
            y_mem = _mem_attend(xn, w_conv_in, j, 3 * CONV_WIDTH // MEM_HEAD_DIM, kv)
            y = _out_proj(y_mix, y_mem, w_conv_out, j)
        else:
            lhs = [xn] + [xp.reshape(s, d) for xp in xn_class_major]
            parts = [
                _window_attn(_attn_in(lhs[g], w_attn_in, cos_tab, sin_tab, j, g), g)
                for g in range(N_GROUPS)]
            y_mix = _merge_groups([p[0] for p in parts], [p[1] for p in parts])
            y_mem = _mem_attend(xn, w_attn_in, j, N_SELF // MEM_HEAD_DIM, kv)
            y = _out_proj(y_mix, y_mem, w_attn_out, j)
        xs, xn = _residual_norm(y, xs, g_mix_post, i, g_ffn_pre, i)
        y = _ffn_fused(xn, w_ffn_gate, w_ffn_up, conv_ffn_w, w_ffn_down, i)
        if i + 1 == DEPTH:
            xs = _residual_norm(y, xs, g_ffn_post, i)
        elif (i + 1) % 2 == 1:
            xs, xn, *xn_class_major = _residual_norm(
                y, xs, g_ffn_post, i, g_mix_pre, i + 1, dilations=dilations)
        else:
            xs, xn = _residual_norm(y, xs, g_ffn_post, i, g_mix_pre, i + 1)
    return xs.reshape(b, s, d)
```

```python
import functools

import jax
import jax.numpy as jnp
from jax import lax
from jax.experimental import pallas as pl
from jax.experimental.pallas import tpu as pltpu

D_MODEL = 4096
SEQ = 8192
DEPTH = 2
HEAD_DIM = 128
N_GROUP_HEADS = 16
GROUP_WIDTH = N_GROUP_HEADS * HEAD_DIM
DILATED_GROUPS = ((128, 1), (512, 4), (2048, 16))
N_GROUPS = len(DILATED_GROUPS)
N_SELF = N_GROUPS * 3 * GROUP_WIDTH
BLOCK = 128
CONV_WIDTH = 3 * D_MODEL // 4
MEM_TOKENS = 256
MEM_HEADS = 4
MEM_HEAD_DIM = 256
MEM_WIDTH = MEM_HEADS * MEM_HEAD_DIM
D_FF = 11008
ROPE_THETA = 10000.0
EPS = 1e-6
NEG_INF = -1e30

V7X_SUBLANES = 8
V7X_LANES = 128
V7X_MXU_COLS = 256
V7X_VMEM_BYTES = 64 * 1024 * 1024
V7X_VMEM_LIMIT_BYTES = V7X_VMEM_BYTES - 2 * 1024 * 1024

BF16 = jnp.bfloat16
F32 = jnp.float32

ROW_BLOCK = 2048
CHUNK_ROWS = 512
OUT_PROJ_COLS = 512
NORM_ROWS = 512
PERM_ROWS = 256
HALO = V7X_SUBLANES


def _params(n_axes):
    return pltpu.CompilerParams(
        dimension_semantics=("arbitrary",) * n_axes,
        vmem_limit_bytes=V7X_VMEM_LIMIT_BYTES,
    )


def _rms_scale(v):
    return lax.rsqrt(jnp.mean(v * v, axis=-1, keepdims=True) + EPS)


def _bdot(a, b):
    return jnp.dot(a, b, preferred_element_type=F32)


def _norm_kernel(x_ref, g_ref, o_ref):
    x = x_ref[...]
    o_ref[...] = (x * _rms_scale(x) * g_ref[...]).astype(o_ref.dtype)


def _rmsnorm_bf16(x, g_all, layer):
    s, d = x.shape
    return pl.pallas_call(
        _norm_kernel,
        grid=(s // NORM_ROWS,),
        in_specs=[
            pl.BlockSpec((NORM_ROWS, d), lambda i: (i, 0)),
            pl.BlockSpec((None, 1, d), lambda i: (layer, 0, 0)),
        ],
        out_specs=pl.BlockSpec((NORM_ROWS, d), lambda i: (i, 0)),
        out_shape=jax.ShapeDtypeStruct((s, d), BF16),
        compiler_params=_params(1),
        name="rmsnorm_bf16",
    )(x, g_all)


def _residual_kernel(y_ref, x_ref, gpost_ref, gnext_ref, xo_ref, xn_ref):
    y = y_ref[...].astype(F32)
    xnew = x_ref[...] + y * _rms_scale(y) * gpost_ref[...]
    xo_ref[...] = xnew
    xn_ref[...] = (xnew * _rms_scale(xnew) * gnext_ref[...]).astype(xn_ref.dtype)


def _row_permutation(rows, d, to_class_major):
    n = rows // d
    out_row = lax.broadcasted_iota(jnp.int32, (rows, rows), 0)
    in_row = lax.broadcasted_iota(jnp.int32, (rows, rows), 1)
    if to_class_major:
        src = (out_row & (n - 1)) * d + (out_row >> (n.bit_length() - 1))
    else:
        src = (out_row & (d - 1)) * n + (out_row >> (d.bit_length() - 1))
    return (in_row == src).astype(BF16)


def _residual_dilated_kernel(y_ref, x_ref, gpost_ref, gnext_ref, xo_ref, xn_ref, *xp_refs):
    y = y_ref[...].astype(F32)
    xnew = x_ref[...] + y * _rms_scale(y) * gpost_ref[...]
    xo_ref[...] = xnew
    xn = (xnew * _rms_scale(xnew) * gnext_ref[...]).astype(BF16)
    xn_ref[...] = xn
    for xp_ref in xp_refs:
        d, n = xp_ref.shape[0], xp_ref.shape[1]
        xp = _bdot(_row_permutation(PERM_ROWS, d, True), xn).astype(BF16)
        for r in range(d):
            xp_ref[r] = xp[r * n:(r + 1) * n, :]


def _residual_last_kernel(y_ref, x_ref, gpost_ref, xo_ref):
    y = y_ref[...].astype(F32)
    xo_ref[...] = x_ref[...] + y * _rms_scale(y) * gpost_ref[...]


def _residual_norm(y, x, g_post, post_layer, g_next=None, next_layer=None, dilations=()):
    s, d = x.shape
    row = pl.BlockSpec((NORM_ROWS, d), lambda i: (i, 0))
    gspec = lambda layer: pl.BlockSpec((None, 1, d), lambda i: (layer, 0, 0))
    if dilations:
        row = pl.BlockSpec((PERM_ROWS, d), lambda i: (i, 0))
        return pl.pallas_call(
            _residual_dilated_kernel,
            grid=(s // PERM_ROWS,),
            in_specs=[row, row, gspec(post_layer), gspec(next_layer)],
            out_specs=[row, row] + [
                pl.BlockSpec((dil, PERM_ROWS // dil, d), lambda i: (0, i, 0)) for dil in dilations],
            out_shape=[jax.ShapeDtypeStruct((s, d), F32), jax.ShapeDtypeStruct((s, d), BF16)] + [
                jax.ShapeDtypeStruct((dil, s // dil, d), BF16) for dil in dilations],
            compiler_params=_params(1),
            name="residual_norm_dilated",
        )(y, x, g_post, g_next)
    if g_next is None:
        return pl.pallas_call(
            _residual_last_kernel,
            grid=(s // NORM_ROWS,),
            in_specs=[row, row, gspec(post_layer)],
            out_specs=row,
            out_shape=jax.ShapeDtypeStruct((s, d), F32),
            compiler_params=_params(1),
            name="residual_last",
        )(y, x, g_post)
    return pl.pallas_call(
        _residual_kernel,
        grid=(s // NORM_ROWS,),
        in_specs=[row, row, gspec(post_layer), gspec(next_layer)],
        out_specs=[row, row],
        out_shape=[jax.ShapeDtypeStruct((s, d), F32), jax.ShapeDtypeStruct((s, d), BF16)],
        compiler_params=_params(1),
        name="residual_norm",
    )(y, x, g_post, g_next)


def _rope_table_kernel(pos_ref, freq_ref, sign_ref, cos_ref, sin_ref):
    ang = pos_ref[...].astype(F32) * freq_ref[...]
    cos_ref[...] = jnp.cos(ang)
    sin_ref[...] = jnp.sin(ang) * sign_ref[...]


def _rope_tables(positions):
    s = positions.shape[-1]
    half = HEAD_DIM // 2
    inv_freq = ROPE_THETA ** (-jnp.arange(half, dtype=F32) * 2.0 / HEAD_DIM)
    freq = jnp.concatenate([inv_freq, inv_freq]).reshape(1, HEAD_DIM)
    sign = jnp.concatenate([-jnp.ones((half,), F32), jnp.ones((half,), F32)]).reshape(1, HEAD_DIM)
    pos = jnp.stack([positions.reshape(s // d, d).T.reshape(s, 1) for _, d in DILATED_GROUPS])
    rows = 1024
    tab = pl.BlockSpec((None, rows, HEAD_DIM), lambda g, i: (g, i, 0))
    const = pl.BlockSpec((1, HEAD_DIM), lambda g, i: (0, 0))
    return pl.pallas_call(
        _rope_table_kernel,
        grid=(N_GROUPS, s // rows),
        in_specs=[pl.BlockSpec((None, rows, 1), lambda g, i: (g, i, 0)), const, const],
        out_specs=[tab, tab],
        out_shape=[jax.ShapeDtypeStruct((N_GROUPS, s, HEAD_DIM), F32)] * 2,
        compiler_params=_params(2),
        name="rope_tables",
    )(pos, freq, sign)


def _mem_kv_kernel(mem_ref, g_ref, w_ref, o_ref):
    m = mem_ref[...]
    mn = (m * _rms_scale(m) * g_ref[...]).astype(BF16)
    o_ref[...] = _bdot(mn, w_ref[...].astype(BF16)).astype(o_ref.dtype)


def _mem_kv(mem, g_mem, w_mem_kv):
    m, d = mem.shape
    n = w_mem_kv.shape[1]
    bn = 512
    return pl.pallas_call(
        _mem_kv_kernel,
        grid=(n // bn,),
        in_specs=[
            pl.BlockSpec((m, d), lambda j: (0, 0)),
            pl.BlockSpec((1, d), lambda j: (0, 0)),
            pl.BlockSpec((d, bn), lambda j: (0, j)),
        ],
        out_specs=pl.BlockSpec((m, bn), lambda j: (0, j)),
        out_shape=jax.ShapeDtypeStruct((m, n), BF16),
        compiler_params=_params(1),
        name="mem_kv",
    )(mem, g_mem.reshape(1, d), w_mem_kv)


def _causal_conv3(v, prev_tail, cw_ref, ext_ref):
    bm = v.shape[0]
    ext_ref[0:HALO, :] = prev_tail
    ext_ref[HALO:HALO + bm, :] = v
    v1 = ext_ref[HALO - 1:HALO - 1 + bm, :]
    v2 = ext_ref[HALO - 2:HALO - 2 + bm, :]
    conv = cw_ref[0:1, :] * v2 + cw_ref[1:2, :] * v1 + cw_ref[2:3, :] * v
    return conv, v[bm - HALO:bm, :]


def _row_chunks(total):
    return [slice(a, a + CHUNK_ROWS) for a in range(0, total, CHUNK_ROWS)]


def _resident_rows(bm, k, buffers=2):
    return pl.BlockSpec((bm, k), lambda i, j: (i, 0), pipeline_mode=pl.Buffered(buffers))


def _conv_in_kernel(x_ref, wb_ref, wc_ref, wu_ref, cw_ref, o_ref, ext0_ref, ext1_ref, tails_ref):
    i, j = pl.program_id(0), pl.program_id(1)
    tail = jnp.where(i == 0, 0.0, tails_ref[j])
    for chunk, rows in enumerate(_row_chunks(x_ref.shape[0])):
        ext_ref = (ext0_ref, ext1_ref)[chunk % 2]
        x = x_ref[rows, :]
        b = _bdot(x, wb_ref[...].astype(BF16))
        c = _bdot(x, wc_ref[...].astype(BF16))
        u = _bdot(x, wu_ref[...].astype(BF16))
        conv, tail = _causal_conv3(c * u, tail, cw_ref, ext_ref)
        o_ref[rows, :] = (b * conv).astype(o_ref.dtype)
    tails_ref[j] = tail


def _conv_in(xn, w_in, conv_w, layer):
    s, d = xn.shape
    bm, bn = ROW_BLOCK, V7X_MXU_COLS
    nj = CONV_WIDTH // bn
    wspec = lambda off: pl.BlockSpec((None, d, bn), lambda i, j: (layer, 0, j + off))
    ext = pltpu.VMEM((CHUNK_ROWS + HALO, bn), F32)
    return pl.pallas_call(
        _conv_in_kernel,
        grid=(s // bm, nj),
        in_specs=[
            _resident_rows(bm, d, buffers=1),
            wspec(0), wspec(nj), wspec(2 * nj),
            pl.BlockSpec((None, 3, bn), lambda i, j: (layer, 0, j)),
        ],
        out_specs=pl.BlockSpec((bm, bn), lambda i, j: (i, j)),
        out_shape=jax.ShapeDtypeStruct((s, CONV_WIDTH), BF16),
        scratch_shapes=[ext, ext, pltpu.VMEM((nj, HALO, bn), F32)],
        compiler_params=_params(2),
        name="conv_in",
    )(xn, w_in, w_in, w_in, conv_w)


def _mem_q_kernel(x_ref, wq_ref, k_ref, v_ref, o_ref):
    q = _bdot(x_ref[...], wq_ref[...].astype(BF16)) * (MEM_HEAD_DIM ** -0.5)
    s = lax.dot_general(q.astype(BF16), k_ref[...], (((1,), (1,)), ((), ())),
                        preferred_element_type=F32)
    m = jnp.max(s, axis=-1, keepdims=True)
    p = jnp.exp(s - m)
    den = jnp.sum(p, axis=-1, keepdims=True)
    o = _bdot((p / den).astype(BF16), v_ref[...])
    o_ref[...] = o.astype(o_ref.dtype)


def _mem_attend(xn, w_in, layer, q_col_block, kv):
    s, d = xn.shape
    bm, bn = ROW_BLOCK, MEM_HEAD_DIM
    return pl.pallas_call(
        _mem_q_kernel,
        grid=(s // bm, MEM_HEADS),
        in_specs=[
            _resident_rows(bm, d),
            pl.BlockSpec((None, d, bn), lambda i, h: (layer, 0, q_col_block + h)),
            pl.BlockSpec((MEM_TOKENS, bn), lambda i, h: (0, h)),
            pl.BlockSpec((MEM_TOKENS, bn), lambda i, h: (0, MEM_HEADS + h)),
        ],
        out_specs=pl.BlockSpec((bm, bn), lambda i, h: (i, h)),
        out_shape=jax.ShapeDtypeStruct((s, MEM_WIDTH), BF16),
        compiler_params=_params(2),
        name="mem_attend",
    )(xn, w_in, kv, kv)


def _out_proj_kernel(a_ref, b_ref, wa_ref, wb_ref, o_ref, wa_bf_ref, wb_bf_ref):
    @pl.when(pl.program_id(1) == 0)
    def _():
        wa_bf_ref[...] = wa_ref[...].astype(BF16)
        wb_bf_ref[...] = wb_ref[...].astype(BF16)

    for rows in _row_chunks(a_ref.shape[0]):
        acc = _bdot(a_ref[rows, :], wa_bf_ref[...])
        acc = acc + _bdot(b_ref[rows, :], wb_bf_ref[...])
        o_ref[rows, :] = acc.astype(o_ref.dtype)


def _out_proj(a, b, w, layer):
    s, ka = a.shape
    kb = b.shape[1]
    n = w.shape[2]
    bm, bn = ROW_BLOCK, OUT_PROJ_COLS
    assert ka % kb == 0
    return pl.pallas_call(
        _out_proj_kernel,
        grid=(n // bn, s // bm),
        in_specs=[
            pl.BlockSpec((bm, ka), lambda j, i: (i, 0)),
            pl.BlockSpec((bm, kb), lambda j, i: (i, 0)),
            pl.BlockSpec((None, ka, bn), lambda j, i: (layer, 0, j)),
            pl.BlockSpec((None, kb, bn), lambda j, i: (layer, ka // kb, j)),
        ],
        out_specs=pl.BlockSpec((bm, bn), lambda j, i: (i, j)),
        out_shape=jax.ShapeDtypeStruct((s, n), BF16),
        scratch_shapes=[pltpu.VMEM((ka, bn), BF16), pltpu.VMEM((kb, bn), BF16)],
        compiler_params=_params(2),
        name="out_proj",
    )(a, b, w, w)


def _ffn_in_kernel(x_ref, wg0_ref, wg1_ref, wu0_ref, wu1_ref, cw0_ref, cw1_ref, o_ref,
                   ext0_ref, ext1_ref, tails_ref, *, n_col_blocks):
    i, j = pl.program_id(0), pl.program_id(1)
    bn = wg0_ref.shape[1]
    operands = ((wg0_ref, wu0_ref, cw0_ref), (wg1_ref, wu1_ref, cw1_ref))

    def column_blocks(count):
        chunk = 0
        for half, (wg_ref, wu_ref, cw_ref) in enumerate(operands[:count]):
            col_block = 2 * j + half
            cols = slice(half * bn, (half + 1) * bn)
            tail = jnp.where(i == 0, 0.0, tails_ref[col_block])
            for rows in _row_chunks(x_ref.shape[0]):
                ext_ref = (ext0_ref, ext1_ref)[chunk % 2]
                chunk += 1
                x = x_ref[rows, :]
                g = _bdot(x, wg_ref[...].astype(BF16))
                u = _bdot(x, wu_ref[...].astype(BF16))
                gc, tail = _causal_conv3(g, tail, cw_ref, ext_ref)
                act = gc / (1.0 + jnp.exp(-gc))
                o_ref[rows, cols] = (act * u).astype(o_ref.dtype)
            tails_ref[col_block] = tail

    paired = 2 * j + 1 < n_col_blocks
    pl.when(paired)(lambda: column_blocks(2))
    pl.when(jnp.logical_not(paired))(lambda: column_blocks(1))


def _ffn_in(xn, w_gate, w_up, conv_w, layer):
    s, d = xn.shape
    bm, bn = ROW_BLOCK, V7X_MXU_COLS
    n_col_blocks = D_FF // bn
    col_block = lambda j, half: jnp.minimum(2 * j + half, n_col_blocks - 1)
    wspec = lambda half: pl.BlockSpec((None, d, bn), lambda i, j: (layer, 0, col_block(j, half)))
    cwspec = lambda half: pl.BlockSpec((None, 3, bn), lambda i, j: (layer, 0, col_block(j, half)))
    ext = pltpu.VMEM((CHUNK_ROWS + HALO, bn), F32)
    return pl.pallas_call(
        functools.partial(_ffn_in_kernel, n_col_blocks=n_col_blocks),
        grid=(s // bm, pl.cdiv(n_col_blocks, 2)),
        in_specs=[
            _resident_rows(bm, d, buffers=1),
            wspec(0), wspec(1), wspec(0), wspec(1), cwspec(0), cwspec(1),
        ],
        out_specs=pl.BlockSpec((bm, 2 * bn), lambda i, j: (i, j)),
        out_shape=jax.ShapeDtypeStruct((s, D_FF), BF16),
        scratch_shapes=[ext, ext, pltpu.VMEM((n_col_blocks, HALO, bn), F32)],
        compiler_params=_params(2),
        name="ffn_in",
    )(xn, w_gate, w_gate, w_up, w_up, conv_w, conv_w)


FFN_OUT_COL_CHUNK = 512
FFN_OUT_K_BLOCK = 1024
FFN_OUT_COLS = 2048


def _ffn_out_kernel(a_ref, w_ref, o_ref, acc_ref, *, kdim):
    k = pl.program_id(2)
    last = pl.num_programs(2) - 1
    bk = a_ref.shape[1]

    def step(first, final):
        valid = kdim - k * bk
        a_cols = lax.broadcasted_iota(jnp.int32, a_ref.shape, 1)
        a = jnp.where(a_cols < valid, a_ref[...], jnp.zeros_like(a_ref))
        w_rows = lax.broadcasted_iota(jnp.int32, (bk, FFN_OUT_COL_CHUNK), 0)
        for c in range(0, o_ref.shape[1], FFN_OUT_COL_CHUNK):
            cols = slice(c, c + FFN_OUT_COL_CHUNK)
            w = jnp.where(w_rows < valid, w_ref[:, cols], 0.0).astype(BF16)
            for rows in _row_chunks(a_ref.shape[0]):
                part = _bdot(a[rows, :], w)
                total = part if first else acc_ref[rows, cols] + part
                if final:
                    o_ref[rows, cols] = total.astype(o_ref.dtype)
                else:
                    acc_ref[rows, cols] = total

    pl.when(k == 0)(lambda: step(True, False))
    pl.when(jnp.logical_and(k > 0, k < last))(lambda: step(False, False))
    pl.when(k == last)(lambda: step(False, True))


def _ffn_out(a, w_down, layer):
    s, kdim = a.shape
    n = w_down.shape[2]
    bm, bk, bn = ROW_BLOCK, FFN_OUT_K_BLOCK, FFN_OUT_COLS
    return pl.pallas_call(
        functools.partial(_ffn_out_kernel, kdim=kdim),
        grid=(s // bm, n // bn, pl.cdiv(kdim, bk)),
        in_specs=[
            pl.BlockSpec((bm, bk), lambda i, j, k: (i, k)),
            pl.BlockSpec((None, bk, bn), lambda i, j, k: (layer, k, j)),
        ],
        out_specs=pl.BlockSpec((bm, bn), lambda i, j, k: (i, j)),
        out_shape=jax.ShapeDtypeStruct((s, n), BF16),
        scratch_shapes=[pltpu.VMEM((bm, bn), F32)],
        compiler_params=_params(3),
        name="ffn_out",
    )(a, w_down)


ATTN_IN_COLS = 512
HEADS_PER_ATTN_IN_BLOCK = ATTN_IN_COLS // HEAD_DIM
ATTN_IN_BLOCKS_PER_SECTION = GROUP_WIDTH // ATTN_IN_COLS
ATTN_IN_BLOCKS_PER_GROUP = 3 * ATTN_IN_BLOCKS_PER_SECTION


def _attn_in_kernel(x_ref, w_ref, cos_ref, sin_ref, o_ref):
    section = pl.program_id(1) // ATTN_IN_BLOCKS_PER_SECTION
    scale = jnp.where(section == 0, HEAD_DIM ** -0.5, 1.0).astype(F32)
    is_v = section == 2
    for rows in _row_chunks(x_ref.shape[0]):
        acc = _bdot(x_ref[rows, :], w_ref[...].astype(BF16))
        cos = cos_ref[rows, :] * scale
        sin = sin_ref[rows, :] * scale
        for h in range(HEADS_PER_ATTN_IN_BLOCK):
            cols = slice(h * HEAD_DIM, (h + 1) * HEAD_DIM)
            v = acc[:, cols]
            rot = pltpu.roll(v, HEAD_DIM // 2, axis=1)
            o_ref[rows, cols] = jnp.where(is_v, v, v * cos + rot * sin).astype(o_ref.dtype)


def _attn_in(xn, w_in, cos_tab, sin_tab, layer, group):
    s, d = xn.shape
    bm, bn = ROW_BLOCK, ATTN_IN_COLS
    nj = ATTN_IN_BLOCKS_PER_GROUP
    tab = pl.BlockSpec((None, bm, HEAD_DIM), lambda i, j: (group, i, 0))
    return pl.pallas_call(
        _attn_in_kernel,
        grid=(s // bm, nj),
        in_specs=[
            _resident_rows(bm, d),
            pl.BlockSpec((None, d, bn), lambda i, j: (layer, 0, group * nj + j)),
            tab, tab,
        ],
        out_specs=pl.BlockSpec((bm, bn), lambda i, j: (i, j)),
        out_shape=jax.ShapeDtypeStruct((s, 3 * GROUP_WIDTH), BF16),
        compiler_params=_params(2),
        name=f"attn_in_g{group}",
    )(xn, w_in, cos_tab, sin_tab)


WINDOW_BLOCKS_PER_STEP = 4


def _window_attn_kernel(q_ref, k_ref, v_ref, o_ref, lse_ref, kprev_ref, vprev_ref):
    step = pl.program_id(1)

    @pl.when(step == 0)
    def _():
        kprev_ref[...] = jnp.zeros_like(kprev_ref)
        vprev_ref[...] = jnp.zeros_like(vprev_ref)

    qi = lax.broadcasted_iota(jnp.int32, (BLOCK, 2 * BLOCK), 0)
    col = lax.broadcasted_iota(jnp.int32, (BLOCK, 2 * BLOCK), 1)
    ki = col & (BLOCK - 1)
    is_own = col >= BLOCK
    valid_inner = jnp.logical_or(jnp.logical_and(is_own, ki <= qi),
                                 jnp.logical_and(jnp.logical_not(is_own), ki >= qi))
    valid_first = jnp.logical_and(valid_inner, jnp.logical_or(is_own, step > 0))
    lane = lax.broadcasted_iota(jnp.int32, (BLOCK, V7X_LANES), 1)
    ones = jnp.ones((2 * BLOCK, HEAD_DIM), BF16)
    nt = (((1,), (1,)), ((), ()))
    n_blocks = q_ref.shape[0] // BLOCK
    for sub in range(n_blocks):
        valid = valid_first if sub == 0 else valid_inner
        rows = slice(sub * BLOCK, (sub + 1) * BLOCK)
        lse_all = jnp.zeros((BLOCK, V7X_LANES), F32)
        for h in range(N_GROUP_HEADS):
            cols = slice(h * HEAD_DIM, (h + 1) * HEAD_DIM)
            if sub == 0:
                keys = jnp.concatenate([kprev_ref[:, cols], k_ref[rows, cols]], axis=0)
                vals = jnp.concatenate([vprev_ref[:, cols], v_ref[rows, cols]], axis=0)
            else:
                both = slice((sub - 1) * BLOCK, (sub + 1) * BLOCK)
                keys, vals = k_ref[both, cols], v_ref[both, cols]
            s = lax.dot_general(q_ref[rows, cols], keys, nt, preferred_element_type=F32)
            s = jnp.where(valid, s, NEG_INF)
            m = jnp.max(s, axis=-1, keepdims=True)
            p = jnp.exp(s - m).astype(BF16)
            ov = _bdot(p, jnp.concatenate([vals, ones], axis=1))
            den = ov[:, HEAD_DIM:]
            o_ref[rows, cols] = (ov[:, :HEAD_DIM] / den).astype(o_ref.dtype)
            lse_all = jnp.where(lane == h, m + jnp.log(den), lse_all)
        lse_ref[rows, :] = lse_all
    last_block = slice((n_blocks - 1) * BLOCK, n_blocks * BLOCK)
    kprev_ref[...] = k_ref[last_block, :]
    vprev_ref[...] = v_ref[last_block, :]


def _window_attn(h_group, group):
    s = h_group.shape[0]
    _, d = DILATED_GROUPS[group]
    rows = WINDOW_BLOCKS_PER_STEP * BLOCK
    assert (s // d) % rows == 0
    steps_per_class = s // d // rows
    spec = lambda width, sec: pl.BlockSpec(
        (rows, width), lambda r, step: (r * steps_per_class + step, sec))
    return pl.pallas_call(
        _window_attn_kernel,
        grid=(d, steps_per_class),
        in_specs=[spec(GROUP_WIDTH, 0), spec(GROUP_WIDTH, 1), spec(GROUP_WIDTH, 2)],
        out_specs=[spec(GROUP_WIDTH, 0), spec(V7X_LANES, 0)],
        out_shape=[
            jax.ShapeDtypeStruct((s, GROUP_WIDTH), BF16),
            jax.ShapeDtypeStruct((s, V7X_LANES), F32),
        ],
        scratch_shapes=[pltpu.VMEM((BLOCK, GROUP_WIDTH), BF16)] * 2,
        compiler_params=_params(2),
        name=f"window_attn_g{group}",
    )(h_group, h_group, h_group)


def _bf16_pieces(v):
    hi = v.astype(BF16)
    rest = v - hi.astype(F32)
    mid = rest.astype(BF16)
    lo = (rest - mid.astype(F32)).astype(BF16)
    return hi, mid, lo


def _merge_kernel(o0_ref, o1_ref, o2_ref, l0_ref, l1_ref, l2_ref, out_ref):
    outs = [o0_ref[...].astype(F32)]
    lses = [l0_ref[...]]
    for o_ref, l_ref in ((o1_ref, l1_ref), (o2_ref, l2_ref)):
        d = o_ref.shape[0]
        perm = _row_permutation(PERM_ROWS, d, False)
        o_cm = jnp.concatenate([o_ref[r] for r in range(d)], axis=0)
        l_cm = jnp.concatenate([l_ref[r] for r in range(d)], axis=0)
        outs.append(_bdot(perm, o_cm))
        hi, mid, lo = _bf16_pieces(l_cm)
        lses.append(_bdot(perm, hi) + _bdot(perm, mid) + _bdot(perm, lo))
    l0, l1, l2 = lses
    m = jnp.maximum(jnp.maximum(l0, l1), l2)
    e0, e1, e2 = jnp.exp(l0 - m), jnp.exp(l1 - m), jnp.exp(l2 - m)
    tot = e0 + e1 + e2
    a0, a1, a2 = e0 / tot, e1 / tot, e2 / tot
    for h in range(N_GROUP_HEADS):
        cols = slice(h * HEAD_DIM, (h + 1) * HEAD_DIM)
        merged = (a0[:, h:h + 1] * outs[0][:, cols] + a1[:, h:h + 1] * outs[1][:, cols]
                  + a2[:, h:h + 1] * outs[2][:, cols])
        out_ref[:, cols] = merged.astype(out_ref.dtype)


def _merge_groups(outs, lses):
    s = outs[0].shape[0]
    rows = PERM_ROWS

    def specs(width):
        out = [pl.BlockSpec((rows, width), lambda i: (i, 0))]
        for _, d in DILATED_GROUPS[1:]:
            out.append(pl.BlockSpec((d, rows // d, width), lambda i: (0, i, 0)))
        return out

    class_major = lambda a, d: a.reshape(d, s // d, a.shape[1])
    outs = [outs[0]] + [class_major(o, d) for o, (_, d) in zip(outs[1:], DILATED_GROUPS[1:])]
    lses = [lses[0]] + [class_major(l, d) for l, (_, d) in zip(lses[1:], DILATED_GROUPS[1:])]
    return pl.pallas_call(
        _merge_kernel,
        grid=(s // rows,),
        in_specs=specs(GROUP_WIDTH) + specs(V7X_LANES),
        out_specs=pl.BlockSpec((rows, GROUP_WIDTH), lambda i: (i, 0)),
        out_shape=jax.ShapeDtypeStruct((s, GROUP_WIDTH), BF16),
        compiler_params=_params(1),
        name="merge_groups",
    )(*outs, *lses)


def kernel(x, mem, positions, g_mem, w_mem_kv, g_mix_pre, g_mix_post, g_ffn_pre, g_ffn_post,
           w_conv_in, conv_mix_w, w_conv_out, w_attn_in, w_attn_out,
           w_ffn_gate, w_ffn_up, conv_ffn_w, w_ffn_down):
    b, s, d = x.shape
    assert b == 1 and s == SEQ and d == D_MODEL
    xs = x.reshape(s, d)
    gains = lambda g: g.reshape(DEPTH, 1, d)
    g_mix_pre, g_mix_post = gains(g_mix_pre), gains(g_mix_post)
    g_ffn_pre, g_ffn_post = gains(g_ffn_pre), gains(g_ffn_post)

    kv = _mem_kv(mem.reshape(MEM_TOKENS, d), g_mem, w_mem_kv)
    cos_tab, sin_tab = _rope_tables(positions)

    dilations = tuple(dil for _, dil in DILATED_GROUPS[1:])
    xn = _rmsnorm_bf16(xs, g_mix_pre, 0)
    xn_class_major = ()
    for i in range(DEPTH):
        j = i // 2
        if i % 2 == 0:
            y_mix = _conv_in(xn, w_conv_in, conv_mix_w, j)
            y_mem = _mem_attend(xn, w_conv_in, j, 3 * CONV_WIDTH // MEM_HEAD_DIM, kv)
            y = _out_proj(y_mix, y_mem, w_conv_out, j)
        else:
            lhs = [xn] + [xp.reshape(s, d) for xp in xn_class_major]
            parts = [
                _window_attn(_attn_in(lhs[g], w_attn_in, cos_tab, sin_tab, j, g), g)
                for g in range(N_GROUPS)]
            y_mix = _merge_groups([p[0] for p in parts], [p[1] for p in parts])
            y_mem = _mem_attend(xn, w_attn_in, j, N_SELF // MEM_HEAD_DIM, kv)
            y = _out_proj(y_mix, y_mem, w_attn_out, j)
        xs, xn = _residual_norm(y, xs, g_mix_post, i, g_ffn_pre, i)
        a = _ffn_in(xn, w_ffn_gate, w_ffn_up, conv_ffn_w, i)
        y = _ffn_out(a, w_ffn_down, i)
        if i + 1 == DEPTH:
            xs = _residual_norm(y, xs, g_ffn_post, i)
        elif (i + 1) % 2 == 1:
            xs, xn, *xn_class_major = _residual_norm(
                y, xs, g_ffn_post, i, g_mix_pre, i + 1, dilations=dilations)
        else:
            xs, xn = _residual_norm(y, xs, g_ffn_post, i, g_mix_pre, i + 1)
    return xs.reshape(b, s, d)
```
